```python
import jax, jax.numpy as jnp
from jax import lax
import numpy as np

D_MODEL = 2048
BATCH = 4
SEQ = 8192
DEPTH = 1
DEC_BATCH = 32
DEC_SEQ = 16
PAST_LEN = 2048

CHUNK = 64
N_META = 16
MIX_WIDTH = D_MODEL
HG_HEADS = 8
HG_DK = 128
HG_DV = 128
HG_KW = HG_HEADS * HG_DK
HG_WIDTH = HG_HEADS * HG_DV
FOX_HEADS = 8
FOX_DH = 128
FOX_WIDTH = FOX_HEADS * FOX_DH
D_FF = 4 * D_MODEL
Q_BLOCK = 128
EPS = 1e-6
FORGET_BIAS_INIT = 2.0
IN_COLS = 2 * HG_KW + 2 * HG_WIDTH + 3 * FOX_WIDTH + FOX_HEADS
IN_SPLITS = (HG_KW, 2 * HG_KW, 2 * HG_KW + HG_WIDTH, 2 * HG_KW + 2 * HG_WIDTH,
             2 * HG_KW + 2 * HG_WIDTH + FOX_WIDTH, 2 * HG_KW + 2 * HG_WIDTH + 2 * FOX_WIDTH,
             2 * HG_KW + 2 * HG_WIDTH + 3 * FOX_WIDTH)

kernel_name = 'hymba_hgrn2_fox_stream'


def rmsnorm(x, g):
    xf = x.astype(jnp.float32)
    y = xf * lax.rsqrt(jnp.mean(jnp.square(xf), axis=-1, keepdims=True) + EPS)
    return (y * g.astype(jnp.float32)).astype(x.dtype)


def hgrn2_chunkwise(q, k, v, log_f, s0, lead):
    b, t = q.shape[0], q.shape[1]
    n_blk = -(-(lead + t) // CHUNK)
    tail = n_blk * CHUNK - lead - t

    def blocks(a):
        a = jnp.pad(a, ((0, 0), (lead, tail), (0, 0), (0, 0)))
        return a.reshape(b, n_blk, CHUNK, a.shape[2], a.shape[3]).swapaxes(0, 1)

    causal = jnp.tril(jnp.ones((CHUNK, CHUNK), bool))[None, :, :, None, None]

    def step(s, blk):
        qb, kb, vb, gb = blk
        cum = jnp.cumsum(gb, axis=1)
        diff = cum[:, :, None] - cum[:, None, :]
        decay = jnp.exp(jnp.where(causal, diff, -jnp.inf))
        scores = jnp.einsum('bthd,btshd,bshd->bhts', qb, decay, kb)
        o = (jnp.einsum('bthd,bhde->bthe', qb * jnp.exp(cum), s)
             + jnp.einsum('bhts,bshe->bthe', scores, vb))
        last = cum[:, -1]
        s_new = (jnp.exp(last)[..., None] * s
                 + jnp.einsum('bshd,bshe->bhde', kb * jnp.exp(last[:, None] - cum), vb))
        return s_new, o

    s_fin, o = lax.scan(step, s0, (blocks(q), blocks(k), blocks(v), blocks(log_f)))
    o = o.swapaxes(0, 1).reshape(b, n_blk * CHUNK, o.shape[3], o.shape[4])[:, lead:lead + t]
    return o, s_fin


def fox_attention(q, k, v, c_q, c_k, q_offset):
    b, tq = q.shape[0], q.shape[1]
    tk = k.shape[1]
    n_blk = -(-tq // Q_BLOCK)
    padq = n_blk * Q_BLOCK - tq
    qp = jnp.pad(q, ((0, 0), (0, padq), (0, 0), (0, 0)))
    cqp = jnp.pad(c_q, ((0, 0), (0, padq), (0, 0)))
    kpos = jnp.arange(tk)
    ck_t = c_k.transpose(0, 2, 1)[:, :, None, :]
    scale = FOX_DH ** -0.5

    def one_block(i):
        start = i * Q_BLOCK
        qs = lax.dynamic_slice_in_dim(qp, start, Q_BLOCK, axis=1)
        cs = lax.dynamic_slice_in_dim(cqp, start, Q_BLOCK, axis=1)
        logits = (jnp.einsum('bqhd,bkhd->bhqk', qs, k).astype(jnp.float32) * scale
                  + (cs.transpose(0, 2, 1)[..., None] - ck_t))
        qpos = q_offset + start + jnp.arange(Q_BLOCK)
        logits = jnp.where(kpos[None, :] <= qpos[:, None], logits, -jnp.inf)
        p = jax.nn.softmax(logits, axis=-1)
        return jnp.einsum('bhqk,bkhd->bqhd', p.astype(v.dtype), v)

    o = lax.map(one_block, jnp.arange(n_blk))
    return o.swapaxes(0, 1).reshape(b, n_blk * Q_BLOCK, o.shape[3], o.shape[4])[:, :tq]


def trunk_layer(h, past_k, past_v, past_logf, hg_s0, hg_lead, g_mix_pre, w_in, lb, hg_gain,
                f_bias, w_out, g_mix_post, g_mlp_pre, w_up, w_down, g_mlp_post):
    f32 = jnp.float32
    bsz, t = h.shape[0], h.shape[1]
    a = rmsnorm(h, g_mix_pre)
    z = a @ w_in
    hq, hf, hi, hgate, fq, fk, fv, ff = jnp.split(z, IN_SPLITS, axis=-1)

    lbf = lb.astype(f32)
    fl = hf.astype(f32)
    log_f = jnp.log(lbf + (1.0 - lbf) * jax.nn.sigmoid(fl))
    k_in = (1.0 - lbf) * jax.nn.sigmoid(-fl)
    rs = lambda u, d: u.reshape(bsz, t, -1, d)
    o_hg, s_new = hgrn2_chunkwise(rs(jax.nn.silu(hq.astype(f32)), HG_DK), rs(k_in, HG_DK),
                                  rs(hi.astype(f32), HG_DV), rs(log_f, HG_DK),
                                  hg_s0.astype(f32), hg_lead)
    o_hg = (rmsnorm(o_hg, hg_gain.reshape(HG_HEADS, HG_DV)).reshape(bsz, t, HG_WIDTH)
            * jax.nn.silu(hgate.astype(f32)))

    q_new = fq.reshape(bsz, t, FOX_HEADS, FOX_DH)
    k_new = fk.reshape(bsz, t, FOX_HEADS, FOX_DH)
    v_new = fv.reshape(bsz, t, FOX_HEADS, FOX_DH)
    logf_new = jax.nn.log_sigmoid(ff.astype(f32) + f_bias.astype(f32))
    k_all = jnp.concatenate([past_k.astype(k_new.dtype), k_new], axis=1)
    v_all = jnp.concatenate([past_v.astype(v_new.dtype), v_new], axis=1)
    c = jnp.cumsum(jnp.concatenate([past_logf.astype(f32), logf_new], axis=1), axis=1)
    p_len = past_k.shape[1]
    o_fox = fox_attention(q_new, k_all, v_all, c[:, p_len:], c, p_len)

    mix = jnp.concatenate([o_hg.astype(h.dtype), o_fox.reshape(bsz, t, FOX_WIDTH).astype(h.dtype)],
                          axis=-1) @ w_out
    h = h + rmsnorm(mix, g_mix_post)
    m = rmsnorm(h, g_mlp_pre)
    m = jnp.square(jax.nn.relu(m @ w_up)) @ w_down
    h = h + rmsnorm(m, g_mlp_post)
    return h, k_new, v_new, logf_new, s_new


def setup_inputs(seed: int = 0) -> dict:
    key = jax.random.key(seed)
    ks = jax.random.split(key, 20)
    f32 = jnp.float32
    nrm = lambda k, shape, s: jax.random.normal(k, shape, f32) * s
    return {
        'x_prompt': nrm(ks[0], (BATCH, SEQ, D_MODEL), 1.0),
        'x_sample': nrm(ks[1], (DEC_BATCH, DEC_SEQ, D_MODEL), 1.0),
        'cache_fox_k': nrm(ks[2], (DEPTH, DEC_BATCH, PAST_LEN, FOX_HEADS, FOX_DH), 1.0),
        'cache_fox_v': nrm(ks[3], (DEPTH, DEC_BATCH, PAST_LEN, FOX_HEADS, FOX_DH), 1.0),
        'cache_fox_logf': jax.nn.log_sigmoid(FORGET_BIAS_INIT + nrm(ks[4], (DEPTH, DEC_BATCH, PAST_LEN, FOX_HEADS), 1.0)),
        'state_hgrn': nrm(ks[5], (DEPTH, DEC_BATCH, HG_HEADS, HG_DK, HG_DV), 0.5),
        'meta_tokens': nrm(ks[6], (N_META, D_MODEL), 1.0),
        'g_mix_pre': 1.0 + nrm(ks[7], (DEPTH, D_MODEL), 0.05),
        'w_in': nrm(ks[8], (DEPTH, D_MODEL, IN_COLS), D_MODEL ** -0.5),
        'hg_lb_logits': nrm(ks[9], (DEPTH + 1, HG_KW), 0.5),
        'hg_norm_gain': 1.0 + nrm(ks[10], (DEPTH, HG_WIDTH), 0.05),
        'fox_f_bias': FORGET_BIAS_INIT + nrm(ks[11], (DEPTH, FOX_HEADS), 0.1),
        'w_out': nrm(ks[12], (DEPTH, MIX_WIDTH, D_MODEL), MIX_WIDTH ** -0.5),
        'g_mix_post': 1.0 + nrm(ks[13], (DEPTH, D_MODEL), 0.05),
        'g_mlp_pre': 1.0 + nrm(ks[14], (DEPTH, D_MODEL), 0.05),
        'w_up': nrm(ks[15], (DEPTH, D_MODEL, D_FF), D_MODEL ** -0.5),
        'w_down': nrm(ks[16], (DEPTH, D_FF, D_MODEL), D_FF ** -0.5),
        'g_mlp_post': 1.0 + nrm(ks[17], (DEPTH, D_MODEL), 0.05),
    }


def reference(x_prompt, x_sample, cache_fox_k, cache_fox_v, cache_fox_logf, state_hgrn,
              meta_tokens, g_mix_pre, w_in, hg_lb_logits, hg_norm_gain, fox_f_bias, w_out,
              g_mix_post, g_mlp_pre, w_up, w_down, g_mlp_post):
    lb_all = jnp.cumsum(jax.nn.softmax(hg_lb_logits.astype(jnp.float32), axis=0), axis=0)
    bp = x_prompt.shape[0]
    hp = jnp.concatenate([jnp.broadcast_to(meta_tokens.astype(x_prompt.dtype)[None], (bp, N_META, D_MODEL)),
                          x_prompt], axis=1)
    hs = x_sample
    empty_kv = jnp.zeros((bp, 0, FOX_HEADS, FOX_DH), x_prompt.dtype)
    empty_logf = jnp.zeros((bp, 0, FOX_HEADS), jnp.float32)
    s_zero = jnp.zeros((bp, HG_HEADS, HG_DK, HG_DV), jnp.float32)
    lead_prompt = (-N_META) % CHUNK
    lead_sample = cache_fox_k.shape[2] % CHUNK
    kp_l, vp_l, fp_l, sp_l, ks_l, vs_l, fs_l, ss_l = [], [], [], [], [], [], [], []
    for l in range(DEPTH):
        wl = (g_mix_pre[l], w_in[l], lb_all[l], hg_norm_gain[l], fox_f_bias[l], w_out[l],
              g_mix_post[l], g_mlp_pre[l], w_up[l], w_down[l], g_mlp_post[l])
        hp, kp, vp, fp, sp = trunk_layer(hp, empty_kv, empty_kv, empty_logf, s_zero, lead_prompt, *wl)
        hs, kn, vn, fn, sn = trunk_layer(hs, cache_fox_k[l], cache_fox_v[l], cache_fox_logf[l],
                                         state_hgrn[l], lead_sample, *wl)
        kp_l.append(kp); vp_l.append(vp); fp_l.append(fp); sp_l.append(sp)
        ks_l.append(kn); vs_l.append(vn); fs_l.append(fn); ss_l.append(sn)
    y_prompt = hp[:, N_META:]
    return (y_prompt, hs,
            jnp.stack(kp_l), jnp.stack(vp_l), jnp.stack(fp_l), jnp.stack(sp_l),
            jnp.stack(ks_l), jnp.stack(vs_l), jnp.stack(fs_l), jnp.stack(ss_l))
```

```python
import functools

import jax
import jax.numpy as jnp
from jax import lax
from jax.experimental import pallas as pl
from jax.experimental.pallas import tpu as pltpu

F32 = jnp.float32
BF16 = jnp.bfloat16

EPS = 1e-6
HEAD_DIM = 128
LANES = 128
NEG_BIG = -1e30
VMEM_LIMIT = 56 * 1024 * 1024


def _pick(n, pref):
    if n <= pref:
        return n
    t = pref
    while n % t:
        t //= 2
    return t


def _params(sem):
    return pltpu.CompilerParams(dimension_semantics=sem, vmem_limit_bytes=VMEM_LIMIT)


def _log_sigmoid(x):
    return jnp.minimum(x, 0.0) - jnp.log1p(jnp.exp(-jnp.abs(x)))


def _rms(x, gain):
    ms = jnp.mean(x * x, axis=-1, keepdims=True)
    return x * lax.rsqrt(ms + EPS) * gain


def _inproj_kernel(x_ref, g_ref, w_ref, wf_ref, fb_ref, z_ref, lf_ref, a_scr):
    @pl.when(pl.program_id(1) == 0)
    def _():
        a = _rms(x_ref[...], g_ref[...]).astype(BF16)
        a_scr[...] = a
        ff = jnp.dot(a, wf_ref[...], preferred_element_type=F32) + fb_ref[...]
        lf_ref[...] = _log_sigmoid(ff)

    z_ref[...] = jnp.dot(a_scr[...], w_ref[...], preferred_element_type=F32)


def _inproj(x, gain, w_main, w_f, f_bias):
    rows, d = x.shape
    n = w_main.shape[1]
    tm = _pick(rows, 1024)
    tn = _pick(n, 1024)
    return pl.pallas_call(
        _inproj_kernel,
        grid=(rows // tm, n // tn),
        in_specs=[
            pl.BlockSpec((tm, d), lambda i, j: (i, 0)),
            pl.BlockSpec((1, d), lambda i, j: (0, 0)),
            pl.BlockSpec((d, tn), lambda i, j: (0, j)),
            pl.BlockSpec((d, LANES), lambda i, j: (0, 0)),
            pl.BlockSpec((1, LANES), lambda i, j: (0, 0)),
        ],
        out_specs=[
            pl.BlockSpec((tm, tn), lambda i, j: (i, j)),
            pl.BlockSpec((tm, LANES), lambda i, j: (i, 0)),
        ],
        out_shape=[
            jax.ShapeDtypeStruct((rows, n), F32),
            jax.ShapeDtypeStruct((rows, LANES), F32),
        ],
        scratch_shapes=[pltpu.VMEM((tm, d), BF16)],
        compiler_params=_params(("parallel", "arbitrary")),
        name="inproj",
    )(x, gain, w_main, w_f, f_bias)


def _cumsum_kernel(x_ref, init_ref, c_ref, carry):
    @pl.when(pl.program_id(1) == 0)
    def _():
        carry[...] = init_ref[0]

    x = x_ref[0]
    tb = x.shape[0]
    tri = (lax.broadcasted_iota(jnp.int32, (tb, tb), 1)
           <= lax.broadcasted_iota(jnp.int32, (tb, tb), 0)).astype(BF16)
    x_hi = x.astype(BF16)
    r1 = x - x_hi.astype(F32)
    x_mid = r1.astype(BF16)
    x_lo = (r1 - x_mid.astype(F32)).astype(BF16)
    c = (jnp.dot(tri, x_hi, preferred_element_type=F32)
         + jnp.dot(tri, x_mid, preferred_element_type=F32)
         + jnp.dot(tri, x_lo, preferred_element_type=F32)) + carry[...]
    c_ref[0] = c
    carry[...] = c[tb - 1:tb, :]


def _cumsum(x, init):
    nb, t, w = x.shape
    tb = _pick(t, 256)
    return pl.pallas_call(
        _cumsum_kernel,
        grid=(nb, t // tb),
        in_specs=[
            pl.BlockSpec((1, tb, w), lambda b, i: (b, i, 0)),
            pl.BlockSpec((1, 1, w), lambda b, i: (b, 0, 0)),
        ],
        out_specs=pl.BlockSpec((1, tb, w), lambda b, i: (b, i, 0)),
        out_shape=jax.ShapeDtypeStruct((nb, t, w), F32),
        scratch_shapes=[pltpu.VMEM((1, w), F32)],
        compiler_params=_params(("parallel", "arbitrary")),
        name="cumsum",
    )(x, init)


def _hgrn_chunk(hq, hf, hi, lb, st):
    c = hq.shape[0]
    levels = c.bit_length() - 1
    q = hq * jax.nn.sigmoid(hq)
    g = jnp.log(lb + (1.0 - lb) * jax.nn.sigmoid(hf))
    k = (1.0 - lb) * jax.nn.sigmoid(-hf)

    rows = lax.broadcasted_iota(jnp.int32, (c, HEAD_DIM), 0)
    ri = lax.broadcasted_iota(jnp.int32, (c, c), 0)
    ci = lax.broadcasted_iota(jnp.int32, (c, c), 1)
    scores = jnp.where(ri == ci, jnp.sum(q * k, axis=-1, keepdims=True), 0.0)
    pre = g
    for lvl in range(1, levels + 1):
        half = 1 << (lvl - 1)
        upper = (rows & half) != 0
        src = jnp.where((rows & (2 * half - 1)) == (half - 1), pre, 0.0)
        lo = src
        s = 1
        while s < half:
            lo = lo + pltpu.roll(lo, c - s, 0)
            s *= 2
        up = pltpu.roll(src, 1, 0)
        s = 1
        while s < half:
            up = up + pltpu.roll(up, s, 0)
            s *= 2
        mid = lo + up
        w = jnp.exp(jnp.where(upper, pre, mid - pre))
        qt = jnp.where(upper, q * w, 0.0).astype(BF16)
        kt = jnp.where(upper, 0.0, k * w).astype(BF16)
        s_l = lax.dot_general(qt, kt, (((1,), (1,)), ((), ())), preferred_element_type=F32)
        scores = scores + jnp.where((ri >> lvl) == (ci >> lvl), s_l, 0.0)
        pre = jnp.where(upper, pre + mid, pre)
    cum = pre
    v = hi.astype(BF16)
    qe = (q * jnp.exp(cum)).astype(BF16)
    o = (lax.dot_general(qe, st.astype(BF16), (((1,), (1,)), ((), ())), preferred_element_type=F32)
         + jnp.dot(scores.astype(BF16), v, preferred_element_type=F32))
    last = cum[c - 1:c, :]
    kd = (k * jnp.exp(last - cum)).astype(BF16)
    st_new = st * jnp.exp(last) + lax.dot_general(
        v, kd, (((0,), (0,)), ((), ())), preferred_element_type=F32)
    return o, st_new


def _hgrn_kernel(hq_ref, hf_ref, hi_ref, hg_ref, lbl_ref, gain_ref, s0_ref,
                 o_ref, sfin_ref, st_scr, *, chunk):
    ti = pl.program_id(2)

    @pl.when(ti == 0)
    def _():
        st_scr[...] = s0_ref[0, 0]

    lg = lbl_ref[0]
    e = jnp.exp(lg - jnp.max(lg, axis=0, keepdims=True))
    lb = e[0:1, :] / jnp.sum(e, axis=0, keepdims=True)
    gain = gain_ref[0]
    tt = hq_ref.shape[1]
    st = st_scr[...]
    for cidx in range(tt // chunk):
        sl = slice(cidx * chunk, (cidx + 1) * chunk)
        o, st = _hgrn_chunk(hq_ref[0, sl, :], hf_ref[0, sl, :], hi_ref[0, sl, :], lb, st)
        hg = hg_ref[0, sl, :]
        o_ref[0, sl, :] = (_rms(o, gain) * (hg * jax.nn.sigmoid(hg))).astype(o_ref.dtype)
    st_scr[...] = st

    @pl.when(ti == pl.num_programs(2) - 1)
    def _():
        sfin_ref[0, 0] = st


def _hgrn(z3, lb_logits, gain, s0t, *, s0_shared, chunk, tt):
    nb, t, _ = z3.shape
    nh = gain.shape[0]
    nslot = lb_logits.shape[1]
    s0_map = (lambda b, h, i: (0, h, 0, 0)) if s0_shared else (lambda b, h, i: (b, h, 0, 0))

    def col(off):
        return pl.BlockSpec((1, tt, HEAD_DIM), lambda b, h, i: (b, i, off + h))

    return pl.pallas_call(
        functools.partial(_hgrn_kernel, chunk=chunk),
        grid=(nb, nh, t // tt),
        in_specs=[
            col(0), col(nh), col(2 * nh), col(3 * nh),
            pl.BlockSpec((1, nslot, HEAD_DIM), lambda b, h, i: (h, 0, 0)),
            pl.BlockSpec((1, 1, HEAD_DIM), lambda b, h, i: (h, 0, 0)),
            pl.BlockSpec((1, 1, HEAD_DIM, HEAD_DIM), s0_map),
        ],
        out_specs=[
            pl.BlockSpec((1, tt, HEAD_DIM), lambda b, h, i: (b, i, h)),
            pl.BlockSpec((1, 1, HEAD_DIM, HEAD_DIM), lambda b, h, i: (b, h, 0, 0)),
        ],
        out_shape=[
            jax.ShapeDtypeStruct((nb, t, nh * HEAD_DIM), BF16),
            jax.ShapeDtypeStruct((nb, nh, HEAD_DIM, HEAD_DIM), F32),
        ],
        scratch_shapes=[pltpu.VMEM((HEAD_DIM, HEAD_DIM), F32)],
        compiler_params=_params(("parallel", "parallel", "arbitrary")),
        name="hgrn",
    )(z3, z3, z3, z3, lb_logits, gain, s0t)


def _fox_kernel(q_ref, cq_ref, kp_ref, vp_ref, ckp_ref, ks_ref, vs_ref, cks_ref, o_ref,
                *, pchunk, scale):
    h = pl.program_id(1)
    qi = pl.program_id(2)
    tq = q_ref.shape[1]
    q = q_ref[0].astype(BF16)
    lane = lax.broadcasted_iota(jnp.int32, (tq, LANES), 1)
    cq = jnp.sum(jnp.where(lane == h, cq_ref[0], 0.0), axis=-1, keepdims=True)

    def update(carry, kb, vb, ck, mask):
        m, l, acc = carry
        s = lax.dot_general(q, kb.astype(BF16), (((1,), (1,)), ((), ())),
                            preferred_element_type=F32) * scale + (cq - ck)
        if mask is not None:
            s = jnp.where(mask, s, -jnp.inf)
        m_new = jnp.maximum(m, jnp.max(s, axis=-1, keepdims=True))
        alpha = jnp.exp(m - m_new)
        p = jnp.exp(s - m_new)
        l = alpha * l + jnp.sum(p, axis=-1, keepdims=True)
        acc = alpha * acc + jnp.dot(p.astype(BF16), vb.astype(BF16), preferred_element_type=F32)
        return m_new, l, acc

    carry = (jnp.full((tq, 1), NEG_BIG, F32), jnp.zeros((tq, 1), F32),
             jnp.zeros((tq, HEAD_DIM), F32))
    plen = kp_ref.shape[1]
    for pc in range(plen // pchunk):
        sl = slice(pc * pchunk, (pc + 1) * pchunk)
        carry = update(carry, kp_ref[0, sl, :], vp_ref[0, sl, :], ckp_ref[0, 0, :, sl], None)

    def body(j, carry):
        start = pl.multiple_of(j * tq, tq)
        return update(carry, ks_ref[0, pl.ds(start, tq), :], vs_ref[0, pl.ds(start, tq), :],
                      cks_ref[0, 0, :, pl.ds(start, tq)], None)

    if ks_ref.shape[1] == tq:
        start = 0
    else:
        carry = lax.fori_loop(0, qi, body, carry)
        start = pl.multiple_of(qi * tq, tq)
    causal = (lax.broadcasted_iota(jnp.int32, (tq, tq), 1)
              <= lax.broadcasted_iota(jnp.int32, (tq, tq), 0))
    m, l, acc = update(carry, ks_ref[0, pl.ds(start, tq), :], vs_ref[0, pl.ds(start, tq), :],
                       cks_ref[0, 0, :, pl.ds(start, tq)], causal)
    o_ref[0] = (acc / l).astype(o_ref.dtype)


def _fox(z3, b_off, qcol, kcol, vcol, cq, kp, vp, kp_col, vp_col, ckp, cks, *, p_shared, tq):
    nb, t, _ = cq.shape
    nh = cks.shape[1]
    plen = kp.shape[1]
    pchunk = _pick(plen, 512)
    pb = (lambda b: 0) if p_shared else (lambda b: b)
    kern = functools.partial(_fox_kernel, pchunk=pchunk, scale=HEAD_DIM ** -0.5)
    return pl.pallas_call(
        kern,
        grid=(nb, nh, t // tq),
        in_specs=[
            pl.BlockSpec((1, tq, HEAD_DIM), lambda b, h, i: (b + b_off, i, qcol + h)),
            pl.BlockSpec((1, tq, LANES), lambda b, h, i: (b, i, 0)),
            pl.BlockSpec((1, plen, HEAD_DIM), lambda b, h, i: (pb(b), 0, kp_col + h)),
            pl.BlockSpec((1, plen, HEAD_DIM), lambda b, h, i: (pb(b), 0, vp_col + h)),
            pl.BlockSpec((1, 1, 1, plen), lambda b, h, i: (pb(b), h, 0, 0)),
            pl.BlockSpec((1, t, HEAD_DIM), lambda b, h, i: (b + b_off, 0, kcol + h)),
            pl.BlockSpec((1, t, HEAD_DIM), lambda b, h, i: (b + b_off, 0, vcol + h)),
            pl.BlockSpec((1, 1, 1, t), lambda b, h, i: (b, h, 0, 0)),
        ],
        out_specs=pl.BlockSpec((1, tq, HEAD_DIM), lambda b, h, i: (b, i, h)),
        out_shape=jax.ShapeDtypeStruct((nb, t, nh * HEAD_DIM), BF16),
        compiler_params=_params(("parallel", "parallel", "arbitrary")),
        name="fox",
    )(z3, cq, kp, vp, ckp, z3, z3, cks)


def _outproj_kernel(h_ref, a_ref, b_ref, wa_ref, wb_ref, g_ref, o_ref):
    mix = (jnp.dot(a_ref[...], wa_ref[...], preferred_element_type=F32)
           + jnp.dot(b_ref[...], wb_ref[...], preferred_element_type=F32))
    o_ref[...] = h_ref[...] + _rms(mix, g_ref[...])


def _outproj(h, mix_a, mix_b, w_a, w_b, gain):
    rows, d = h.shape
    ka, kb = mix_a.shape[1], mix_b.shape[1]
    tm = _pick(rows, 512)
    return pl.pallas_call(
        _outproj_kernel,
        grid=(rows // tm,),
        in_specs=[
            pl.BlockSpec((tm, d), lambda i: (i, 0)),
            pl.BlockSpec((tm, ka), lambda i: (i, 0)),
            pl.BlockSpec((tm, kb), lambda i: (i, 0)),
            pl.BlockSpec((ka, d), lambda i: (0, 0)),
            pl.BlockSpec((kb, d), lambda i: (0, 0)),
            pl.BlockSpec((1, d), lambda i: (0, 0)),
        ],
        out_specs=pl.BlockSpec((tm, d), lambda i: (i, 0)),
        out_shape=jax.ShapeDtypeStruct((rows, d), F32),
        compiler_params=_params(("parallel",)),
        name="outproj",
    )(h, mix_a, mix_b, w_a, w_b, gain)


def _mlp_kernel(h_ref, gpre_ref, wu_ref, wd_ref, gpost_ref, o_ref, a_scr, acc_scr):
    j = pl.program_id(1)

    @pl.when(j == 0)
    def _():
        a_scr[...] = _rms(h_ref[...], gpre_ref[...]).astype(BF16)
        acc_scr[...] = jnp.zeros_like(acc_scr)

    u = jnp.maximum(jnp.dot(a_scr[...], wu_ref[...], preferred_element_type=F32), 0.0)
    acc_scr[...] += jnp.dot((u * u).astype(BF16), wd_ref[...], preferred_element_type=F32)

    @pl.when(j == pl.num_programs(1) - 1)
    def _():
        o_ref[...] = h_ref[...] + _rms(acc_scr[...], gpost_ref[...])


def _mlp(h, g_pre, w_up, w_down, g_post):
    rows, d = h.shape
    dff = w_up.shape[1]
    tm = _pick(rows, 512)
    tf = _pick(dff, 512)
    return pl.pallas_call(
        _mlp_kernel,
        grid=(rows // tm, dff // tf),
        in_specs=[
            pl.BlockSpec((tm, d), lambda i, j: (i, 0)),
            pl.BlockSpec((1, d), lambda i, j: (0, 0)),
            pl.BlockSpec((d, tf), lambda i, j: (0, j)),
            pl.BlockSpec((tf, d), lambda i, j: (j, 0)),
            pl.BlockSpec((1, d), lambda i, j: (0, 0)),
        ],
        out_specs=pl.BlockSpec((tm, d), lambda i, j: (i, 0)),
        out_shape=jax.ShapeDtypeStruct((rows, d), F32),
        scratch_shapes=[pltpu.VMEM((tm, d), BF16), pltpu.VMEM((tm, d), F32)],
        compiler_params=_params(("parallel", "arbitrary")),
        name="mlp",
    )(h, g_pre, w_up, w_down, g_post)


def _row_form(c_col, nh):
    return jnp.transpose(c_col[:, :, :nh], (0, 2, 1))[:, :, None, :]


def kernel(x_prompt, x_sample, cache_fox_k, cache_fox_v, cache_fox_logf, state_hgrn, meta_tokens,
           g_mix_pre, w_in, hg_lb_logits, hg_norm_gain, fox_f_bias, w_out, g_mix_post, g_mlp_pre,
           w_up, w_down, g_mlp_post):
    bp, seq, d = x_prompt.shape
    bs, dseq, _ = x_sample.shape
    n_meta = meta_tokens.shape[0]
    past = cache_fox_k.shape[2]
    nh = cache_fox_k.shape[3]
    hw = nh * HEAD_DIM
    n_main = w_in.shape[2] - nh
    assert dseq == n_meta, "sample frames and meta tokens share the small-stream kernels"
    assert state_hgrn.shape[2] == nh and n_main == 7 * hw

    w_main = w_in[0, :, :n_main].astype(BF16)
    w_f = jnp.pad(w_in[0, :, n_main:], ((0, 0), (0, LANES - nh))).astype(BF16)
    f_bias = jnp.pad(fox_f_bias[0], (0, LANES - nh))[None, :]
    w_oa = w_out[0, :hw].astype(BF16)
    w_ob = w_out[0, hw:].astype(BF16)
    w_u = w_up[0].astype(BF16)
    w_d = w_down[0].astype(BF16)
    g_pre = g_mix_pre[0][None, :]
    g_post = g_mix_post[0][None, :]
    g_mpre = g_mlp_pre[0][None, :]
    g_mpost = g_mlp_post[0][None, :]
    lb_logits = jnp.transpose(hg_lb_logits.reshape(-1, nh, HEAD_DIM), (1, 0, 2))
    hg_gain = hg_norm_gain[0].reshape(nh, 1, HEAD_DIM)
    qcol, kcol, vcol = 4 * nh, 5 * nh, 6 * nh

    ns = bs + 1
    xs = jnp.concatenate([meta_tokens, x_sample.reshape(bs * dseq, d)], axis=0)
    zs, lfs = _inproj(xs, g_pre, w_main, w_f, f_bias)
    zs3 = zs.reshape(ns, dseq, n_main)
    lfs3 = lfs.reshape(ns, dseq, LANES)

    past_lf = jnp.pad(cache_fox_logf[0], ((0, 0), (0, 0), (0, LANES - nh)))
    c_past = _cumsum(past_lf, jnp.zeros((bs, 1, LANES), F32))
    init_s = jnp.concatenate([jnp.zeros((1, 1, LANES), F32), c_past[:, past - 1:past, :]], axis=0)
    c_small = _cumsum(lfs3, init_s)

    s0_small = jnp.concatenate([jnp.zeros((1,) + state_hgrn.shape[2:], F32), state_hgrn[0]], axis=0)
    s0_small_t = jnp.swapaxes(s0_small, -1, -2)
    mix_hg_s, sfin_s_t = _hgrn(zs3, lb_logits, hg_gain, s0_small_t, s0_shared=False,
                               chunk=dseq, tt=dseq)

    kc = cache_fox_k[0].reshape(bs, past, hw)
    vc = cache_fox_v[0].reshape(bs, past, hw)
    mix_fox_s = _fox(zs3, 1, qcol, kcol, vcol, c_small[1:], kc, vc, 0, 0,
                     _row_form(c_past, nh), _row_form(c_small[1:], nh), p_shared=False, tq=dseq)

    xm = x_prompt.reshape(bp * seq, d)
    zm, lfm = _inproj(xm, g_pre, w_main, w_f, f_bias)
    zm3 = zm.reshape(bp, seq, n_main)
    lfm3 = lfm.reshape(bp, seq, LANES)
    c_meta = c_small[0:1]
    init_m = jnp.broadcast_to(c_meta[:, n_meta - 1:n_meta, :], (bp, 1, LANES))
    c_main = _cumsum(lfm3, init_m)

    mix_hg_m, sfin_m_t = _hgrn(zm3, lb_logits, hg_gain, sfin_s_t[0:1], s0_shared=True,
                               chunk=_pick(seq, 128), tt=_pick(seq, 512))
    mix_fox_m = _fox(zm3, 0, qcol, kcol, vcol, c_main, zs3, zs3, kcol, vcol,
                     _row_form(c_meta, nh), _row_form(c_main, nh), p_shared=True,
                     tq=_pick(seq, 512))

    h1m = _outproj(xm, mix_hg_m.reshape(bp * seq, hw), mix_fox_m.reshape(bp * seq, hw),
                   w_oa, w_ob, g_post)
    y_prompt = _mlp(h1m, g_mpre, w_u, w_d, g_mpost).reshape(bp, seq, d)
    h1s = _outproj(x_sample.reshape(bs * dseq, d), mix_hg_s[1:].reshape(bs * dseq, hw),
                   mix_fox_s.reshape(bs * dseq, hw), w_oa, w_ob, g_post)
    y_sample = _mlp(h1s, g_mpre, w_u, w_d, g_mpost).reshape(bs, dseq, d)

    def with_meta(meta_part, main_part):
        meta_b = jnp.broadcast_to(meta_part[None], (bp,) + meta_part.shape)
        return jnp.concatenate([meta_b, main_part], axis=1)[None]

    k_p = with_meta(zs3[0, :, kcol * HEAD_DIM:kcol * HEAD_DIM + hw],
                    zm3[:, :, kcol * HEAD_DIM:kcol * HEAD_DIM + hw])
    v_p = with_meta(zs3[0, :, vcol * HEAD_DIM:vcol * HEAD_DIM + hw],
                    zm3[:, :, vcol * HEAD_DIM:vcol * HEAD_DIM + hw])
    lf_p = with_meta(lfs3[0, :, :nh], lfm3[:, :, :nh])
    k_s = zs3[1:, :, kcol * HEAD_DIM:kcol * HEAD_DIM + hw]
    v_s = zs3[1:, :, vcol * HEAD_DIM:vcol * HEAD_DIM + hw]
    return (y_prompt, y_sample,
            k_p.reshape(1, bp, n_meta + seq, nh, HEAD_DIM),
            v_p.reshape(1, bp, n_meta + seq, nh, HEAD_DIM),
            lf_p,
            jnp.swapaxes(sfin_m_t, -1, -2)[None],
            k_s.reshape(1, bs, dseq, nh, HEAD_DIM),
            v_s.reshape(1, bs, dseq, nh, HEAD_DIM),
            lfs3[1:, :, :nh][None],
            jnp.swapaxes(sfin_s_t[1:], -1, -2)[None])
```

```python
import functools

import jax
import jax.numpy as jnp
from jax import lax
from jax.experimental import pallas as pl
from jax.experimental.pallas import tpu as pltpu

F32 = jnp.float32
BF16 = jnp.bfloat16

EPS = 1e-6
HEAD_DIM = 128
LANES = 128
NEG_BIG = -1e30
LOG2E = 1.4426950408889634
VMEM_LIMIT = 56 * 1024 * 1024


def _pick(n, pref):
    if n <= pref:
        return n
    t = pref
    while n % t:
        t //= 2
    return t


def _params(sem):
    return pltpu.CompilerParams(dimension_semantics=sem, vmem_limit_bytes=VMEM_LIMIT)


def _log_sigmoid(x):
    return jnp.minimum(x, 0.0) - jnp.log1p(jnp.exp(-jnp.abs(x)))


def _rms(x, gain):
    ms = jnp.mean(x * x, axis=-1, keepdims=True)
    return x * lax.rsqrt(ms + EPS) * gain


def _inproj_kernel(x_ref, g_ref, w_ref, wf_ref, fb_ref, z_ref, lf_ref, a_scr):
    @pl.when(pl.program_id(1) == 0)
    def _():
        a = _rms(x_ref[...], g_ref[...]).astype(BF16)
        a_scr[...] = a
        ff = jnp.dot(a, wf_ref[...], preferred_element_type=F32) + fb_ref[...]
        lf_ref[...] = _log_sigmoid(ff)

    z_ref[...] = jnp.dot(a_scr[...], w_ref[...], preferred_element_type=F32)


def _inproj(x, gain, w_main, w_f, f_bias):
    rows, d = x.shape
    n = w_main.shape[1]
    tm = _pick(rows, 1024)
    tn = _pick(n, 1024)
    return pl.pallas_call(
        _inproj_kernel,
        grid=(rows // tm, n // tn),
        in_specs=[
            pl.BlockSpec((tm, d), lambda i, j: (i, 0)),
            pl.BlockSpec((1, d), lambda i, j: (0, 0)),
            pl.BlockSpec((d, tn), lambda i, j: (0, j)),
            pl.BlockSpec((d, LANES), lambda i, j: (0, 0)),
            pl.BlockSpec((1, LANES), lambda i, j: (0, 0)),
        ],
        out_specs=[
            pl.BlockSpec((tm, tn), lambda i, j: (i, j)),
            pl.BlockSpec((tm, LANES), lambda i, j: (i, 0)),
        ],
        out_shape=[
            jax.ShapeDtypeStruct((rows, n), F32),
            jax.ShapeDtypeStruct((rows, LANES), F32),
        ],
        scratch_shapes=[pltpu.VMEM((tm, d), BF16)],
        compiler_params=_params(("parallel", "arbitrary")),
        name="inproj",
    )(x, gain, w_main, w_f, f_bias)


def _split3(x):
    hi = x.astype(BF16)
    r = x - hi.astype(F32)
    mid = r.astype(BF16)
    lo = (r - mid.astype(F32)).astype(BF16)
    return hi, mid, lo


def _cumsum_kernel(x_ref, init_ref, c_ref, *, chain):
    g, nblk, blk = x_ref.shape
    upper = (lax.broadcasted_iota(jnp.int32, (blk, blk), 0)
             <= lax.broadcasted_iota(jnp.int32, (blk, blk), 1)).astype(BF16)
    if chain:
        before = (lax.broadcasted_iota(jnp.int32, (nblk, nblk), 1)
                  < lax.broadcasted_iota(jnp.int32, (nblk, nblk), 0)).astype(BF16)
    for r in range(g):
        c = sum(jnp.dot(p, upper, preferred_element_type=F32) for p in _split3(x_ref[r]))
        if chain:
            tot = jnp.broadcast_to(c[:, blk - 1:blk], (nblk, blk))
            c = c + sum(jnp.dot(before, p, preferred_element_type=F32) for p in _split3(tot))
        c_ref[r] = c + init_ref[r]


def _cumsum(x, init, *, chain):
    n, nblk, blk = x.shape
    g = _pick(n, 8)
    ni = init.shape[1]
    return pl.pallas_call(
        functools.partial(_cumsum_kernel, chain=chain),
        grid=(n // g,),
        in_specs=[
            pl.BlockSpec((g, nblk, blk), lambda i: (i, 0, 0)),
            pl.BlockSpec((g, ni, blk), lambda i: (i, 0, 0)),
        ],
        out_specs=pl.BlockSpec((g, nblk, blk), lambda i: (i, 0, 0)),
        out_shape=jax.ShapeDtypeStruct((n, nblk, blk), F32),
        compiler_params=_params(("parallel",)),
        name="cumsum",
    )(x, init)


def _cumsum_time(lf, init, nh):
    nb, t = lf.shape[0], lf.shape[1]
    blk = _pick(t, 256)
    x = jnp.transpose(lf[:, :, :nh], (0, 2, 1))
    if t == blk:
        init3 = jnp.broadcast_to(init.reshape(1, nb * nh, 1), (1, nb * nh, blk))
        return _cumsum(x.reshape(1, nb * nh, blk), init3, chain=False).reshape(nb, nh, t)
    init3 = jnp.broadcast_to(init.reshape(nb * nh, 1, 1), (nb * nh, 1, blk))
    return _cumsum(x.reshape(nb * nh, t // blk, blk), init3, chain=True).reshape(nb, nh, t)


def _hgrn_chunk(hq, hf, hi, lb, st):
    c = hq.shape[0]
    levels = c.bit_length() - 1
    q = hq * jax.nn.sigmoid(hq)
    g = jnp.log(lb + (1.0 - lb) * jax.nn.sigmoid(hf))
    k = (1.0 - lb) * jax.nn.sigmoid(-hf)

    rows = lax.broadcasted_iota(jnp.int32, (c, HEAD_DIM), 0)
    ri = lax.broadcasted_iota(jnp.int32, (c, c), 0)
    ci = lax.broadcasted_iota(jnp.int32, (c, c), 1)
    scores = jnp.where(ri == ci, jnp.sum(q * k, axis=-1, keepdims=True), 0.0)
    pre = g
    for lvl in range(1, levels + 1):
        half = 1 << (lvl - 1)
        upper = (rows & half) != 0
        src = jnp.where((rows & (2 * half - 1)) == (half - 1), pre, 0.0)
        lo = src
        s = 1
        while s < half:
            lo = lo + pltpu.roll(lo, c - s, 0)
            s *= 2
        up = pltpu.roll(src, 1, 0)
        s = 1
        while s < half:
            up = up + pltpu.roll(up, s, 0)
            s *= 2
        mid = lo + up
        w = jnp.exp(jnp.where(upper, pre, mid - pre))
        qt = jnp.where(upper, q * w, 0.0).astype(BF16)
        kt = jnp.where(upper, 0.0, k * w).astype(BF16)
        s_l = lax.dot_general(qt, kt, (((1,), (1,)), ((), ())), preferred_element_type=F32)
        scores = scores + jnp.where((ri >> lvl) == (ci >> lvl), s_l, 0.0)
        pre = jnp.where(upper, pre + mid, pre)
    cum = pre
    v = hi.astype(BF16)
    qe = (q * jnp.exp(cum)).astype(BF16)
    o = (lax.dot_general(qe, st.astype(BF16), (((1,), (1,)), ((), ())), preferred_element_type=F32)
         + jnp.dot(scores.astype(BF16), v, preferred_element_type=F32))
    last = cum[c - 1:c, :]
    kd = (k * jnp.exp(last - cum)).astype(BF16)
    st_new = st * jnp.exp(last) + lax.dot_general(
        v, kd, (((0,), (0,)), ((), ())), preferred_element_type=F32)
    return o, st_new


def _hgrn_kernel(hq_ref, hf_ref, hi_ref, hg_ref, lbl_ref, gain_ref, s0_ref,
                 o_ref, sfin_ref, st_scr, *, chunk):
    ti = pl.program_id(2)

    @pl.when(ti == 0)
    def _():
        st_scr[...] = s0_ref[0, 0]

    lg = lbl_ref[0]
    e = jnp.exp(lg - jnp.max(lg, axis=0, keepdims=True))
    lb = e[0:1, :] / jnp.sum(e, axis=0, keepdims=True)
    gain = gain_ref[0]
    tt = hq_ref.shape[1]
    st = st_scr[...]
    for cidx in range(tt // chunk):
        sl = slice(cidx * chunk, (cidx + 1) * chunk)
        o, st = _hgrn_chunk(hq_ref[0, sl, :], hf_ref[0, sl, :], hi_ref[0, sl, :], lb, st)
        hg = hg_ref[0, sl, :]
        o_ref[0, sl, :] = (_rms(o, gain) * (hg * jax.nn.sigmoid(hg))).astype(o_ref.dtype)
    st_scr[...] = st

    @pl.when(ti == pl.num_programs(2) - 1)
    def _():
        sfin_ref[0, 0] = st


def _hgrn(z3, lb_logits, gain, s0t, *, s0_shared, chunk, tt):
    nb, t, _ = z3.shape
    nh = gain.shape[0]
    nslot = lb_logits.shape[1]
    s0_map = (lambda b, h, i: (0, h, 0, 0)) if s0_shared else (lambda b, h, i: (b, h, 0, 0))

    def col(off):
        return pl.BlockSpec((1, tt, HEAD_DIM), lambda b, h, i: (b, i, off + h))

    return pl.pallas_call(
        functools.partial(_hgrn_kernel, chunk=chunk),
        grid=(nb, nh, t // tt),
        in_specs=[
            col(0), col(nh), col(2 * nh), col(3 * nh),
            pl.BlockSpec((1, nslot, HEAD_DIM), lambda b, h, i: (h, 0, 0)),
            pl.BlockSpec((1, 1, HEAD_DIM), lambda b, h, i: (h, 0, 0)),
            pl.BlockSpec((1, 1, HEAD_DIM, HEAD_DIM), s0_map),
        ],
        out_specs=[
            pl.BlockSpec((1, tt, HEAD_DIM), lambda b, h, i: (b, i, h)),
            pl.BlockSpec((1, 1, HEAD_DIM, HEAD_DIM), lambda b, h, i: (b, h, 0, 0)),
        ],
        out_shape=[
            jax.ShapeDtypeStruct((nb, t, nh * HEAD_DIM), BF16),
            jax.ShapeDtypeStruct((nb, nh, HEAD_DIM, HEAD_DIM), F32),
        ],
        scratch_shapes=[pltpu.VMEM((HEAD_DIM, HEAD_DIM), F32)],
        compiler_params=_params(("parallel", "parallel", "arbitrary")),
        name="hgrn",
    )(z3, z3, z3, z3, lb_logits, gain, s0t)


def _fox_kernel(q_ref, cq_ref, kp_ref, vp_ref, ckp_ref, ks_ref, vs_ref, cks_ref, o_ref,
                *, pchunk, scale):
    h = pl.program_id(1)
    qi = pl.program_id(2)
    tq = q_ref.shape[1]
    q = q_ref[0].astype(BF16)
    lane = lax.broadcasted_iota(jnp.int32, (tq, LANES), 1)
    cq = jnp.sum(jnp.where(lane == h, cq_ref[0], 0.0), axis=-1, keepdims=True)

    def update(carry, kb, vb, ck, mask):
        m, l, acc = carry
        s = lax.dot_general(q, kb.astype(BF16), (((1,), (1,)), ((), ())),
                            preferred_element_type=F32) * scale + (cq - ck)
        if mask is not None:
            s = jnp.where(mask, s, -jnp.inf)
        m_new = jnp.maximum(m, jnp.max(s, axis=-1, keepdims=True))
        alpha = jnp.exp(m - m_new)
        p = jnp.exp(s - m_new)
        l = alpha * l + jnp.sum(p, axis=-1, keepdims=True)
        acc = alpha * acc + jnp.dot(p.astype(BF16), vb.astype(BF16), preferred_element_type=F32)
        return m_new, l, acc

    carry = (jnp.full((tq, 1), NEG_BIG, F32), jnp.zeros((tq, 1), F32),
             jnp.zeros((tq, HEAD_DIM), F32))
    plen = kp_ref.shape[1]
    for pc in range(plen // pchunk):
        sl = slice(pc * pchunk, (pc + 1) * pchunk)
        carry = update(carry, kp_ref[0, sl, :], vp_ref[0, sl, :], ckp_ref[0, 0, :, sl], None)

    def body(j, carry):
        start = pl.multiple_of(j * tq, tq)
        return update(carry, ks_ref[0, pl.ds(start, tq), :], vs_ref[0, pl.ds(start, tq), :],
                      cks_ref[0, 0, :, pl.ds(start, tq)], None)

    if ks_ref.shape[1] == tq:
        start = 0
    else:
        carry = lax.fori_loop(0, qi, body, carry)
        start = pl.multiple_of(qi * tq, tq)
    causal = (lax.broadcasted_iota(jnp.int32, (tq, tq), 1)
              <= lax.broadcasted_iota(jnp.int32, (tq, tq), 0))
    m, l, acc = update(carry, ks_ref[0, pl.ds(start, tq), :], vs_ref[0, pl.ds(start, tq), :],
                       cks_ref[0, 0, :, pl.ds(start, tq)], causal)
    o_ref[0] = (acc / l).astype(o_ref.dtype)


def _fox(z3, b_off, qcol, kcol, vcol, cq, kp, vp, kp_col, vp_col, ckp, cks, *, p_shared, tq):
    nb, t, _ = cq.shape
    nh = cks.shape[1]
    plen = kp.shape[1]
    pchunk = _pick(plen, 512)
    pb = (lambda b: 0) if p_shared else (lambda b: b)
    kern = functools.partial(_fox_kernel, pchunk=pchunk, scale=HEAD_DIM ** -0.5)
    return pl.pallas_call(
        kern,
        grid=(nb, nh, t // tq),
        in_specs=[
            pl.BlockSpec((1, tq, HEAD_DIM), lambda b, h, i: (b + b_off, i, qcol + h)),
            pl.BlockSpec((1, tq, LANES), lambda b, h, i: (b, i, 0)),
            pl.BlockSpec((1, plen, HEAD_DIM), lambda b, h, i: (pb(b), 0, kp_col + h)),
            pl.BlockSpec((1, plen, HEAD_DIM), lambda b, h, i: (pb(b), 0, vp_col + h)),
            pl.BlockSpec((1, 1, 1, plen), lambda b, h, i: (pb(b), h, 0, 0)),
            pl.BlockSpec((1, t, HEAD_DIM), lambda b, h, i: (b + b_off, 0, kcol + h)),
            pl.BlockSpec((1, t, HEAD_DIM), lambda b, h, i: (b + b_off, 0, vcol + h)),
            pl.BlockSpec((1, 1, 1, t), lambda b, h, i: (b, h, 0, 0)),
        ],
        out_specs=pl.BlockSpec((1, tq, HEAD_DIM), lambda b, h, i: (b, i, h)),
        out_shape=jax.ShapeDtypeStruct((nb, t, nh * HEAD_DIM), BF16),
        compiler_params=_params(("parallel", "parallel", "arbitrary")),
        name="fox",
    )(z3, cq, kp, vp, ckp, z3, z3, cks)


def _bias_pieces(c):
    hi = c.astype(BF16).astype(F32)
    r = c - hi
    mid = r.astype(BF16).astype(F32)
    return hi, mid, r - mid


def _bias_rows(c, ones_first):
    hi, mid, lo = _bias_pieces(c)
    sub = lax.broadcasted_iota(jnp.int32, (HEAD_DIM, c.shape[1]), 0)
    third = jnp.where(sub >= 3, sub - 3, sub)
    pieces = jnp.where(third == 0, hi, jnp.where(third == 1, mid, lo))
    first, second = (1.0, pieces) if ones_first else (pieces, 1.0)
    return jnp.where(sub < 3, first, jnp.where(sub < 6, second, 0.0))


def _bias_cols(c):
    hi, mid, lo = _bias_pieces(c)
    lane = lax.broadcasted_iota(jnp.int32, (c.shape[0], LANES), 1)
    pieces = jnp.where(lane == 3, hi, jnp.where(lane == 4, mid, lo))
    return jnp.where(lane < 3, 1.0, jnp.where(lane < 6, pieces, 0.0))


def _fox_prompt_kernel(q_ref, cq_ref, k_ref, v_ref, ck_ref, kp_ref, vp_ref, ckp_ref, o_ref,
                       kaug_scr, vt_scr, sa_scr, sb_scr, *, scale):
    h = pl.program_id(1)
    qi = pl.program_id(2)
    tq = q_ref.shape[1]
    t = k_ref.shape[1]

    @pl.when(qi == 0)
    def _():
        def prep(cix, carry):
            r0 = pl.multiple_of(cix * tq, tq)
            ck = ck_ref[0, 0, :, pl.ds(r0, tq)] * LOG2E
            kaug_scr[pl.ds(r0, tq), HEAD_DIM:] = _bias_rows(-ck, True).T.astype(BF16)
            kaug_scr[pl.ds(r0, tq), :HEAD_DIM] = k_ref[0, pl.ds(r0, tq), :].astype(BF16)
            vt_scr[:, pl.ds(r0, tq)] = v_ref[0, pl.ds(r0, tq), :].T.astype(BF16)
            return carry
        lax.fori_loop(0, t // tq, prep, 0)

    qt = (q_ref[0] * (scale * LOG2E)).T.astype(BF16)
    augq = _bias_rows(cq_ref[0, 0] * LOG2E, False).astype(BF16)
    qaug = jnp.concatenate([qt, augq], axis=0)

    nblk = t // tq

    def row0(blk):
        return pl.multiple_of(jnp.minimum(blk, nblk - 1) * tq, tq)

    def logits(blk):
        return jnp.dot(kaug_scr[pl.ds(row0(blk), tq), :], qaug, preferred_element_type=F32)

    def colmax(s):
        return jnp.max(s, axis=0, keepdims=True)

    def absorb(carry, s, smax, pv_fn):
        m, l, acc = carry
        m_new = jnp.maximum(m, smax)
        alpha = jnp.exp2(m - m_new)
        p = jnp.exp2(s - m_new)
        l = alpha * l + jnp.sum(p, axis=0, keepdims=True)
        return m_new, l, alpha * acc + pv_fn(p.astype(BF16))

    def pv_block(blk):
        return lambda p: jnp.dot(vt_scr[:, pl.ds(row0(blk), tq)], p, preferred_element_type=F32)

    carry = (jnp.full((1, tq), NEG_BIG, F32), jnp.zeros((1, tq), F32),
             jnp.zeros((HEAD_DIM, tq), F32))
    lane = lax.broadcasted_iota(jnp.int32, ckp_ref.shape[1:], 1)
    ckp = jnp.sum(jnp.where(lane == h, ckp_ref[0], 0.0), axis=-1, keepdims=True) * LOG2E
    kp = jnp.concatenate([kp_ref[0].astype(BF16), _bias_cols(-ckp).astype(BF16)], axis=1)
    vp = vp_ref[0].astype(BF16)
    sp = jnp.dot(kp, qaug, preferred_element_type=F32)
    carry = absorb(carry, sp, colmax(sp),
                   lambda p: lax.dot_general(vp, p, (((0,), (0,)), ((), ())),
                                             preferred_element_type=F32))

    def stage(blk, s_scr):
        s = logits(blk)
        s_scr[...] = s
        return colmax(s)

    def body(pair, state):
        ma, carry = state
        mb = stage(2 * pair + 1, sb_scr)
        carry = absorb(carry, sa_scr[...], ma, pv_block(2 * pair))
        ma = stage(2 * pair + 2, sa_scr)
        carry = absorb(carry, sb_scr[...], mb, pv_block(2 * pair + 1))
        return ma, carry

    npairs = qi // 2
    _, carry = lax.fori_loop(0, npairs, body, (stage(0, sa_scr), carry))
    e0 = 2 * npairs
    stage(e0 + 1, sb_scr)
    krow = lax.broadcasted_iota(jnp.int32, (tq, tq), 0)
    qcolumn = lax.broadcasted_iota(jnp.int32, (tq, tq), 1)
    for blk, s_scr in ((e0, sa_scr), (e0 + 1, sb_scr)):
        sm = jnp.where(krow + (blk - qi) * tq <= qcolumn, s_scr[...], -jnp.inf)
        carry = absorb(carry, sm, colmax(sm), pv_block(blk))
    m, l, acc = carry
    o_ref[0] = (acc * (1.0 / l)).T.astype(o_ref.dtype)


def _fox_prompt(z3, qcol, kcol, vcol, c_row, zp3, cp_col, *, tq):
    nb, t, _ = z3.shape
    nh = c_row.shape[1]
    plen = zp3.shape[1]
    kern = functools.partial(_fox_prompt_kernel, scale=HEAD_DIM ** -0.5)
    return pl.pallas_call(
        kern,
        grid=(nb, nh, t // tq),
        in_specs=[
            pl.BlockSpec((1, tq, HEAD_DIM), lambda b, h, i: (b, i, qcol + h)),
            pl.BlockSpec((1, 1, 1, tq), lambda b, h, i: (b, h, 0, i)),
            pl.BlockSpec((1, t, HEAD_DIM), lambda b, h, i: (b, 0, kcol + h)),
            pl.BlockSpec((1, t, HEAD_DIM), lambda b, h, i: (b, 0, vcol + h)),
            pl.BlockSpec((1, 1, 1, t), lambda b, h, i: (b, h, 0, 0)),
            pl.BlockSpec((1, plen, HEAD_DIM), lambda b, h, i: (0, 0, kcol + h)),
            pl.BlockSpec((1, plen, HEAD_DIM), lambda b, h, i: (0, 0, vcol + h)),
            pl.BlockSpec((1, plen, LANES), lambda b, h, i: (0, 0, 0)),
        ],
        out_specs=pl.BlockSpec((1, tq, HEAD_DIM), lambda b, h, i: (b, i, h)),
        out_shape=jax.ShapeDtypeStruct((nb, t, nh * HEAD_DIM), BF16),
        scratch_shapes=[pltpu.VMEM((t, 2 * HEAD_DIM), BF16), pltpu.VMEM((HEAD_DIM, t), BF16),
                        pltpu.VMEM((tq, tq), F32), pltpu.VMEM((tq, tq), F32)],
        compiler_params=_params(("parallel", "parallel", "arbitrary")),
        name="fox_prompt",
    )(z3, c_row, z3, z3, c_row, zp3, zp3, cp_col)


def _outproj_kernel(h_ref, a_ref, b_ref, wa_ref, wb_ref, g_ref, o_ref):
    mix = (jnp.dot(a_ref[...], wa_ref[...], preferred_element_type=F32)
           + jnp.dot(b_ref[...], wb_ref[...], preferred_element_type=F32))
    o_ref[...] = h_ref[...] + _rms(mix, g_ref[...])


def _outproj(h, mix_a, mix_b, w_a, w_b, gain):
    rows, d = h.shape
    ka, kb = mix_a.shape[1], mix_b.shape[1]
    tm = _pick(rows, 512)
    return pl.pallas_call(
        _outproj_kernel,
        grid=(rows // tm,),
        in_specs=[
            pl.BlockSpec((tm, d), lambda i: (i, 0)),
            pl.BlockSpec((tm, ka), lambda i: (i, 0)),
            pl.BlockSpec((tm, kb), lambda i: (i, 0)),
            pl.BlockSpec((ka, d), lambda i: (0, 0)),
            pl.BlockSpec((kb, d), lambda i: (0, 0)),
            pl.BlockSpec((1, d), lambda i: (0, 0)),
        ],
        out_specs=pl.BlockSpec((tm, d), lambda i: (i, 0)),
        out_shape=jax.ShapeDtypeStruct((rows, d), F32),
        compiler_params=_params(("parallel",)),
        name="outproj",
    )(h, mix_a, mix_b, w_a, w_b, gain)


def _mlp_kernel(h_ref, gpre_ref, wu_ref, wd_ref, gpost_ref, o_ref, a_scr, acc_scr):
    j = pl.program_id(1)

    @pl.when(j == 0)
    def _():
        a_scr[...] = _rms(h_ref[...], gpre_ref[...]).astype(BF16)
        acc_scr[...] = jnp.zeros_like(acc_scr)

    u = jnp.maximum(jnp.dot(a_scr[...], wu_ref[...], preferred_element_type=F32), 0.0)
    acc_scr[...] += jnp.dot((u * u).astype(BF16), wd_ref[...], preferred_element_type=F32)

    @pl.when(j == pl.num_programs(1) - 1)
    def _():
        o_ref[...] = h_ref[...] + _rms(acc_scr[...], gpost_ref[...])


def _mlp(h, g_pre, w_up, w_down, g_post):
    rows, d = h.shape
    dff = w_up.shape[1]
    tm = _pick(rows, 512)
    tf = _pick(dff, 512)
    return pl.pallas_call(
        _mlp_kernel,
        grid=(rows // tm, dff // tf),
        in_specs=[
            pl.BlockSpec((tm, d), lambda i, j: (i, 0)),
            pl.BlockSpec((1, d), lambda i, j: (0, 0)),
            pl.BlockSpec((d, tf), lambda i, j: (0, j)),
            pl.BlockSpec((tf, d), lambda i, j: (j, 0)),
            pl.BlockSpec((1, d), lambda i, j: (0, 0)),
        ],
        out_specs=pl.BlockSpec((tm, d), lambda i, j: (i, 0)),
        out_shape=jax.ShapeDtypeStruct((rows, d), F32),
        scratch_shapes=[pltpu.VMEM((tm, d), BF16), pltpu.VMEM((tm, d), F32)],
        compiler_params=_params(("parallel", "arbitrary")),
        name="mlp",
    )(h, g_pre, w_up, w_down, g_post)


def _col_form(c_row):
    c_col = jnp.transpose(c_row, (0, 2, 1))
    return jnp.pad(c_col, ((0, 0), (0, 0), (0, LANES - c_col.shape[2])))


def kernel(x_prompt, x_sample, cache_fox_k, cache_fox_v, cache_fox_logf, state_hgrn, meta_tokens,
           g_mix_pre, w_in, hg_lb_logits, hg_norm_gain, fox_f_bias, w_out, g_mix_post, g_mlp_pre,
           w_up, w_down, g_mlp_post):
    bp, seq, d = x_prompt.shape
    bs, dseq, _ = x_sample.shape
    n_meta = meta_tokens.shape[0]
    past = cache_fox_k.shape[2]
    nh = cache_fox_k.shape[3]
    hw = nh * HEAD_DIM
    n_main = w_in.shape[2] - nh
    assert dseq == n_meta, "sample frames and meta tokens share the small-stream kernels"
    assert state_hgrn.shape[2] == nh and n_main == 7 * hw

    w_main = w_in[0, :, :n_main].astype(BF16)
    w_f = jnp.pad(w_in[0, :, n_main:], ((0, 0), (0, LANES - nh))).astype(BF16)
    f_bias = jnp.pad(fox_f_bias[0], (0, LANES - nh))[None, :]
    w_oa = w_out[0, :hw].astype(BF16)
    w_ob = w_out[0, hw:].astype(BF16)
    w_u = w_up[0].astype(BF16)
    w_d = w_down[0].astype(BF16)
    g_pre = g_mix_pre[0][None, :]
    g_post = g_mix_post[0][None, :]
    g_mpre = g_mlp_pre[0][None, :]
    g_mpost = g_mlp_post[0][None, :]
    lb_logits = jnp.transpose(hg_lb_logits.reshape(-1, nh, HEAD_DIM), (1, 0, 2))
    hg_gain = hg_norm_gain[0].reshape(nh, 1, HEAD_DIM)
    qcol, kcol, vcol = 4 * nh, 5 * nh, 6 * nh

    ns = bs + 1
    xs = jnp.concatenate([meta_tokens, x_sample.reshape(bs * dseq, d)], axis=0)
    zs, lfs = _inproj(xs, g_pre, w_main, w_f, f_bias)
    zs3 = zs.reshape(ns, dseq, n_main)
    lfs3 = lfs.reshape(ns, dseq, LANES)

    c_past = _cumsum_time(cache_fox_logf[0], jnp.zeros((bs, nh), F32), nh)
    init_s = jnp.concatenate([jnp.zeros((1, nh), F32), c_past[:, :, past - 1]], axis=0)
    c_small = _cumsum_time(lfs3, init_s, nh)

    s0_small = jnp.concatenate([jnp.zeros((1,) + state_hgrn.shape[2:], F32), state_hgrn[0]], axis=0)
    s0_small_t = jnp.swapaxes(s0_small, -1, -2)
    mix_hg_s, sfin_s_t = _hgrn(zs3, lb_logits, hg_gain, s0_small_t, s0_shared=False,
                               chunk=dseq, tt=dseq)

    kc = cache_fox_k[0].reshape(bs, past, hw)
    vc = cache_fox_v[0].reshape(bs, past, hw)
    mix_fox_s = _fox(zs3, 1, qcol, kcol, vcol, _col_form(c_small[1:]), kc, vc, 0, 0,
                     c_past[:, :, None, :], c_small[1:, :, None, :], p_shared=False, tq=dseq)

    xm = x_prompt.reshape(bp * seq, d)
    zm, lfm = _inproj(xm, g_pre, w_main, w_f, f_bias)
    zm3 = zm.reshape(bp, seq, n_main)
    lfm3 = lfm.reshape(bp, seq, LANES)
    c_meta = c_small[0:1]
    init_m = jnp.broadcast_to(c_meta[:, :, n_meta - 1], (bp, nh))
    c_main = _cumsum_time(lfm3, init_m, nh)

    mix_hg_m, sfin_m_t = _hgrn(zm3, lb_logits, hg_gain, sfin_s_t[0:1], s0_shared=True,
                               chunk=_pick(seq, 128), tt=_pick(seq, 512))
    mix_fox_m = _fox_prompt(zm3, qcol, kcol, vcol, c_main[:, :, None, :], zs3, _col_form(c_meta),
                            tq=_pick(seq, 512))

    h1m = _outproj(xm, mix_hg_m.reshape(bp * seq, hw), mix_fox_m.reshape(bp * seq, hw),
                   w_oa, w_ob, g_post)
    y_prompt = _mlp(h1m, g_mpre, w_u, w_d, g_mpost).reshape(bp, seq, d)
    h1s = _outproj(x_sample.reshape(bs * dseq, d), mix_hg_s[1:].reshape(bs * dseq, hw),
                   mix_fox_s.reshape(bs * dseq, hw), w_oa, w_ob, g_post)
    y_sample = _mlp(h1s, g_mpre, w_u, w_d, g_mpost).reshape(bs, dseq, d)

    def with_meta(meta_part, main_part):
        meta_b = jnp.broadcast_to(meta_part[None], (bp,) + meta_part.shape)
        return jnp.concatenate([meta_b, main_part], axis=1)[None]

    k_p = with_meta(zs3[0, :, kcol * HEAD_DIM:kcol * HEAD_DIM + hw],
                    zm3[:, :, kcol * HEAD_DIM:kcol * HEAD_DIM + hw])
    v_p = with_meta(zs3[0, :, vcol * HEAD_DIM:vcol * HEAD_DIM + hw],
                    zm3[:, :, vcol * HEAD_DIM:vcol * HEAD_DIM + hw])
    lf_p = with_meta(lfs3[0, :, :nh], lfm3[:, :, :nh])
    k_s = zs3[1:, :, kcol * HEAD_DIM:kcol * HEAD_DIM + hw]
    v_s = zs3[1:, :, vcol * HEAD_DIM:vcol * HEAD_DIM + hw]
    return (y_prompt, y_sample,
            k_p.reshape(1, bp, n_meta + seq, nh, HEAD_DIM),
            v_p.reshape(1, bp, n_meta + seq, nh, HEAD_DIM),
            lf_p,
            jnp.swapaxes(sfin_m_t, -1, -2)[None],
            k_s.reshape(1, bs, dseq, nh, HEAD_DIM),
            v_s.reshape(1, bs, dseq, nh, HEAD_DIM),
            lfs3[1:, :, :nh][None],
            jnp.swapaxes(sfin_s_t[1:], -1, -2)[None])
```

```python
import functools

import jax
import jax.numpy as jnp
from jax import lax
from jax.experimental import pallas as pl
from jax.experimental.pallas import tpu as pltpu

F32 = jnp.float32
BF16 = jnp.bfloat16

EPS = 1e-6
HEAD_DIM = 128
LANES = 128
NEG_BIG = -1e30
LOG2E = 1.4426950408889634
VMEM_LIMIT = 56 * 1024 * 1024


def _pick(n, pref):
    if n <= pref:
        return n
    t = pref
    while n % t:
        t //= 2
    return t


def _params(sem):
    return pltpu.CompilerParams(dimension_semantics=sem, vmem_limit_bytes=VMEM_LIMIT)


def _log_sigmoid(x):
    return jnp.minimum(x, 0.0) - jnp.log1p(jnp.exp(-jnp.abs(x)))


def _rms(x, gain):
    ms = jnp.mean(x * x, axis=-1, keepdims=True)
    return x * lax.rsqrt(ms + EPS) * gain


def _inproj_kernel(x_ref, g_ref, w_ref, wf_ref, fb_ref, z_ref, lf_ref, a_scr):
    @pl.when(pl.program_id(1) == 0)
    def _():
        a = _rms(x_ref[...], g_ref[...]).astype(BF16)
        a_scr[...] = a
        ff = jnp.dot(a, wf_ref[...], preferred_element_type=F32) + fb_ref[...]
        lf_ref[...] = _log_sigmoid(ff)

    z_ref[...] = jnp.dot(a_scr[...], w_ref[...], preferred_element_type=F32)


def _inproj(x, gain, w_main, w_f, f_bias):
    rows, d = x.shape
    n = w_main.shape[1]
    tm = _pick(rows, 1024)
    tn = _pick(n, 1024)
    return pl.pallas_call(
        _inproj_kernel,
        grid=(rows // tm, n // tn),
        in_specs=[
            pl.BlockSpec((tm, d), lambda i, j: (i, 0)),
            pl.BlockSpec((1, d), lambda i, j: (0, 0)),
            pl.BlockSpec((d, tn), lambda i, j: (0, j)),
            pl.BlockSpec((d, LANES), lambda i, j: (0, 0)),
            pl.BlockSpec((1, LANES), lambda i, j: (0, 0)),
        ],
        out_specs=[
            pl.BlockSpec((tm, tn), lambda i, j: (i, j)),
            pl.BlockSpec((tm, LANES), lambda i, j: (i, 0)),
        ],
        out_shape=[
            jax.ShapeDtypeStruct((rows, n), F32),
            jax.ShapeDtypeStruct((rows, LANES), F32),
        ],
        scratch_shapes=[pltpu.VMEM((tm, d), BF16)],
        compiler_params=_params(("parallel", "arbitrary")),
        name="inproj",
    )(x, gain, w_main, w_f, f_bias)


def _inproj_kv_kernel(x_ref, g_ref, w_ref, wf_ref, fb_ref, pk_ref, pv_ref,
                      z_ref, lf_ref, k_hbm, v_hbm, a_scr, stage, sems,
                      *, nz, tiles_per_seq):
    i = pl.program_id(0)
    j = pl.program_id(1)
    tm = x_ref.shape[0]
    plen = pk_ref.shape[0]
    last = pl.num_programs(0) - 1

    def tile_copy(slot, dst, tile):
        row = (tile % tiles_per_seq) * tm + plen
        return pltpu.make_async_copy(stage.at[slot], dst.at[tile // tiles_per_seq, pl.ds(row, tm), :],
                                     sems.at[slot])

    def prefix_copy(slot, src, dst, tile):
        return pltpu.make_async_copy(src, dst.at[tile // tiles_per_seq, pl.ds(0, plen), :],
                                     sems.at[2 + slot])

    def opens_sequence(tile):
        return tile % tiles_per_seq == 0

    @pl.when(j == 0)
    def _():
        a = _rms(x_ref[...], g_ref[...]).astype(BF16)
        a_scr[...] = a
        ff = jnp.dot(a, wf_ref[...], preferred_element_type=F32) + fb_ref[...]
        lf_ref[...] = _log_sigmoid(ff)

    tile = jnp.dot(a_scr[...], w_ref[...], preferred_element_type=F32)

    @pl.when(j < nz)
    def _():
        z_ref[...] = tile

    @pl.when(j == nz)
    def _():
        @pl.when(i > 0)
        def _():
            tile_copy(1, v_hbm, i - 1).wait()

            @pl.when(opens_sequence(i - 1))
            def _():
                prefix_copy(1, pv_ref, v_hbm, i - 1).wait()
        stage[0] = tile
        tile_copy(0, k_hbm, i).start()

        @pl.when(opens_sequence(i))
        def _():
            prefix_copy(0, pk_ref, k_hbm, i).start()

    @pl.when(j == nz + 1)
    def _():
        stage[1] = tile
        tile_copy(1, v_hbm, i).start()
        tile_copy(0, k_hbm, i).wait()

        @pl.when(opens_sequence(i))
        def _():
            prefix_copy(1, pv_ref, v_hbm, i).start()
            prefix_copy(0, pk_ref, k_hbm, i).wait()

        @pl.when(i == last)
        def _():
            tile_copy(1, v_hbm, i).wait()

            @pl.when(opens_sequence(i))
            def _():
                prefix_copy(1, pv_ref, v_hbm, i).wait()


def _inproj_kv(x, gain, w_main, w_f, f_bias, prefix_k, prefix_v, *, nseq, nz):
    rows, d = x.shape
    n = w_main.shape[1]
    tn = n // (nz + 2)
    seq = rows // nseq
    plen = prefix_k.shape[0]
    tm = _pick(seq, 1024)
    kern = functools.partial(_inproj_kv_kernel, nz=nz, tiles_per_seq=seq // tm)
    cache = jax.ShapeDtypeStruct((nseq, plen + seq, tn), F32)
    return pl.pallas_call(
        kern,
        grid=(rows // tm, nz + 2),
        in_specs=[
            pl.BlockSpec((tm, d), lambda i, j: (i, 0)),
            pl.BlockSpec((1, d), lambda i, j: (0, 0)),
            pl.BlockSpec((d, tn), lambda i, j: (0, j)),
            pl.BlockSpec((d, LANES), lambda i, j: (0, 0)),
            pl.BlockSpec((1, LANES), lambda i, j: (0, 0)),
            pl.BlockSpec((plen, tn), lambda i, j: (0, 0)),
            pl.BlockSpec((plen, tn), lambda i, j: (0, 0)),
        ],
        out_specs=[
            pl.BlockSpec((tm, tn), lambda i, j: (i, jnp.minimum(j, nz - 1))),
            pl.BlockSpec((tm, LANES), lambda i, j: (i, 0)),
            pl.BlockSpec(memory_space=pl.ANY),
            pl.BlockSpec(memory_space=pl.ANY),
        ],
        out_shape=[
            jax.ShapeDtypeStruct((rows, nz * tn), F32),
            jax.ShapeDtypeStruct((rows, LANES), F32),
            cache, cache,
        ],
        scratch_shapes=[pltpu.VMEM((tm, d), BF16), pltpu.VMEM((2, tm, tn), F32),
                        pltpu.SemaphoreType.DMA((4,))],
        compiler_params=_params(("arbitrary", "arbitrary")),
        name="inproj_kv",
    )(x, gain, w_main, w_f, f_bias, prefix_k, prefix_v)


def _split3(x):
    hi = x.astype(BF16)
    r = x - hi.astype(F32)
    mid = r.astype(BF16)
    lo = (r - mid.astype(F32)).astype(BF16)
    return hi, mid, lo


def _cumsum_kernel(x_ref, init_ref, c_ref, *, chain):
    g, nblk, blk = x_ref.shape
    upper = (lax.broadcasted_iota(jnp.int32, (blk, blk), 0)
             <= lax.broadcasted_iota(jnp.int32, (blk, blk), 1)).astype(BF16)
    if chain:
        before = (lax.broadcasted_iota(jnp.int32, (nblk, nblk), 1)
                  < lax.broadcasted_iota(jnp.int32, (nblk, nblk), 0)).astype(BF16)
    for r in range(g):
        c = sum(jnp.dot(p, upper, preferred_element_type=F32) for p in _split3(x_ref[r]))
        if chain:
            tot = jnp.broadcast_to(c[:, blk - 1:blk], (nblk, blk))
            c = c + sum(jnp.dot(before, p, preferred_element_type=F32) for p in _split3(tot))
        c_ref[r] = c + init_ref[r]


def _cumsum(x, init, *, chain):
    n, nblk, blk = x.shape
    g = _pick(n, 8)
    ni = init.shape[1]
    return pl.pallas_call(
        functools.partial(_cumsum_kernel, chain=chain),
        grid=(n // g,),
        in_specs=[
            pl.BlockSpec((g, nblk, blk), lambda i: (i, 0, 0)),
            pl.BlockSpec((g, ni, blk), lambda i: (i, 0, 0)),
        ],
        out_specs=pl.BlockSpec((g, nblk, blk), lambda i: (i, 0, 0)),
        out_shape=jax.ShapeDtypeStruct((n, nblk, blk), F32),
        compiler_params=_params(("parallel",)),
        name="cumsum",
    )(x, init)


def _cumsum_time(lf, init, nh):
    nb, t = lf.shape[0], lf.shape[1]
    blk = _pick(t, 256)
    x = jnp.transpose(lf[:, :, :nh], (0, 2, 1))
    if t == blk:
        init3 = jnp.broadcast_to(init.reshape(1, nb * nh, 1), (1, nb * nh, blk))
        return _cumsum(x.reshape(1, nb * nh, blk), init3, chain=False).reshape(nb, nh, t)
    init3 = jnp.broadcast_to(init.reshape(nb * nh, 1, 1), (nb * nh, 1, blk))
    return _cumsum(x.reshape(nb * nh, t // blk, blk), init3, chain=True).reshape(nb, nh, t)


def _block_mid(pre, half):
    c = pre.shape[0]
    if half >= 8:
        x3 = pre.reshape(c // (2 * half), 2 * half, HEAD_DIM)
        return jnp.broadcast_to(x3[:, half - 1:half, :], x3.shape).reshape(c, HEAD_DIM)
    x3 = pre.reshape(c // 8, 8, HEAD_DIM)
    sub = lax.broadcasted_iota(jnp.int32, x3.shape, 1)
    mid = None
    for blk in range(8 // (2 * half)):
        src = 2 * half * blk + half - 1
        piece = jnp.broadcast_to(x3[:, src:src + 1, :], x3.shape)
        mid = piece if mid is None else jnp.where(sub >= 2 * half * blk, piece, mid)
    return mid.reshape(c, HEAD_DIM)


def _hgrn_masks(c):
    rows = lax.broadcasted_iota(jnp.int32, (c, HEAD_DIM), 0)
    ri = lax.broadcasted_iota(jnp.int32, (c, c), 0)
    ci = lax.broadcasted_iota(jnp.int32, (c, c), 1)
    levels = c.bit_length() - 1
    upper = [(rows & (1 << (lvl - 1))) != 0 for lvl in range(1, levels + 1)]
    same = [(ri >> lvl) == (ci >> lvl) for lvl in range(1, levels + 1)]
    return ri == ci, upper, same


def _hgrn_chunk(hq, hf, hi, lb, st, masks):
    c = hq.shape[0]
    diag, upper_masks, same_masks = masks
    q = hq * jax.nn.sigmoid(hq)
    g = jnp.log(lb + (1.0 - lb) * jax.nn.sigmoid(hf))
    k = (1.0 - lb) * jax.nn.sigmoid(-hf)

    scores = jnp.where(diag, jnp.sum(q * k, axis=-1, keepdims=True), 0.0)
    pre = g
    for lvl, (upper, same) in enumerate(zip(upper_masks, same_masks), start=1):
        mid = _block_mid(pre, 1 << (lvl - 1))
        w = jnp.exp(jnp.where(upper, pre, mid - pre))
        qt = jnp.where(upper, q * w, 0.0).astype(BF16)
        kt = jnp.where(upper, 0.0, k * w).astype(BF16)
        s_l = lax.dot_general(qt, kt, (((1,), (1,)), ((), ())), preferred_element_type=F32)
        scores = scores + jnp.where(same, s_l, 0.0)
        pre = jnp.where(upper, pre + mid, pre)
    cum = pre
    v = hi.astype(BF16)
    qe = (q * jnp.exp(cum)).astype(BF16)
    o = (lax.dot_general(qe, st.astype(BF16), (((1,), (1,)), ((), ())), preferred_element_type=F32)
         + jnp.dot(scores.astype(BF16), v, preferred_element_type=F32))
    last = cum[c - 1:c, :]
    kd = (k * jnp.exp(last - cum)).astype(BF16)
    st_new = st * jnp.exp(last) + lax.dot_general(
        v, kd, (((0,), (0,)), ((), ())), preferred_element_type=F32)
    return o, st_new


def _hgrn_kernel(hq_ref, hf_ref, hi_ref, hg_ref, lbl_ref, gain_ref, s0_ref,
                 o_ref, sfin_ref, st_scr, *, chunk):
    ti = pl.program_id(2)

    @pl.when(ti == 0)
    def _():
        st_scr[...] = s0_ref[0, 0]

    lg = lbl_ref[0]
    e = jnp.exp(lg - jnp.max(lg, axis=0, keepdims=True))
    lb = e[0:1, :] / jnp.sum(e, axis=0, keepdims=True)
    gain = gain_ref[0]
    tt = hq_ref.shape[1]
    st = st_scr[...]
    masks = _hgrn_masks(chunk)
    for cidx in range(tt // chunk):
        sl = slice(cidx * chunk, (cidx + 1) * chunk)
        o, st = _hgrn_chunk(hq_ref[0, sl, :], hf_ref[0, sl, :], hi_ref[0, sl, :], lb, st, masks)
        hg = hg_ref[0, sl, :]
        o_ref[0, sl, :] = (_rms(o, gain) * (hg * jax.nn.sigmoid(hg))).astype(o_ref.dtype)
    st_scr[...] = st

    @pl.when(ti == pl.num_programs(2) - 1)
    def _():
        sfin_ref[0, 0] = st


def _hgrn(z3, lb_logits, gain, s0t, *, s0_shared, chunk, tt):
    nb, t, _ = z3.shape
    nh = gain.shape[0]
    nslot = lb_logits.shape[1]
    s0_map = (lambda b, h, i: (0, h, 0, 0)) if s0_shared else (lambda b, h, i: (b, h, 0, 0))

    def col(off):
        return pl.BlockSpec((1, tt, HEAD_DIM), lambda b, h, i: (b, i, off + h))

    return pl.pallas_call(
        functools.partial(_hgrn_kernel, chunk=chunk),
        grid=(nb, nh, t // tt),
        in_specs=[
            col(0), col(nh), col(2 * nh), col(3 * nh),
            pl.BlockSpec((1, nslot, HEAD_DIM), lambda b, h, i: (h, 0, 0)),
            pl.BlockSpec((1, 1, HEAD_DIM), lambda b, h, i: (h, 0, 0)),
            pl.BlockSpec((1, 1, HEAD_DIM, HEAD_DIM), s0_map),
        ],
        out_specs=[
            pl.BlockSpec((1, tt, HEAD_DIM), lambda b, h, i: (b, i, h)),
            pl.BlockSpec((1, 1, HEAD_DIM, HEAD_DIM), lambda b, h, i: (b, h, 0, 0)),
        ],
        out_shape=[
            jax.ShapeDtypeStruct((nb, t, nh * HEAD_DIM), BF16),
            jax.ShapeDtypeStruct((nb, nh, HEAD_DIM, HEAD_DIM), F32),
        ],
        scratch_shapes=[pltpu.VMEM((HEAD_DIM, HEAD_DIM), F32)],
        compiler_params=_params(("parallel", "parallel", "arbitrary")),
        name="hgrn",
    )(z3, z3, z3, z3, lb_logits, gain, s0t)


def _fox_kernel(q_ref, cq_ref, kp_ref, vp_ref, ckp_ref, ks_ref, vs_ref, cks_ref, o_ref,
                *, pchunk, scale):
    h = pl.program_id(1)
    qi = pl.program_id(2)
    tq = q_ref.shape[1]
    q = q_ref[0].astype(BF16)
    lane = lax.broadcasted_iota(jnp.int32, (tq, LANES), 1)
    cq = jnp.sum(jnp.where(lane == h, cq_ref[0], 0.0), axis=-1, keepdims=True)

    def update(carry, kb, vb, ck, mask):
        m, l, acc = carry
        s = lax.dot_general(q, kb.astype(BF16), (((1,), (1,)), ((), ())),
                            preferred_element_type=F32) * scale + (cq - ck)
        if mask is not None:
            s = jnp.where(mask, s, -jnp.inf)
        m_new = jnp.maximum(m, jnp.max(s, axis=-1, keepdims=True))
        alpha = jnp.exp(m - m_new)
        p = jnp.exp(s - m_new)
        l = alpha * l + jnp.sum(p, axis=-1, keepdims=True)
        acc = alpha * acc + jnp.dot(p.astype(BF16), vb.astype(BF16), preferred_element_type=F32)
        return m_new, l, acc

    carry = (jnp.full((tq, 1), NEG_BIG, F32), jnp.zeros((tq, 1), F32),
             jnp.zeros((tq, HEAD_DIM), F32))
    plen = kp_ref.shape[1]
    for pc in range(plen // pchunk):
        sl = slice(pc * pchunk, (pc + 1) * pchunk)
        carry = update(carry, kp_ref[0, sl, :], vp_ref[0, sl, :], ckp_ref[0, 0, :, sl], None)

    def body(j, carry):
        start = pl.multiple_of(j * tq, tq)
        return update(carry, ks_ref[0, pl.ds(start, tq), :], vs_ref[0, pl.ds(start, tq), :],
                      cks_ref[0, 0, :, pl.ds(start, tq)], None)

    if ks_ref.shape[1] == tq:
        start = 0
    else:
        carry = lax.fori_loop(0, qi, body, carry)
        start = pl.multiple_of(qi * tq, tq)
    causal = (lax.broadcasted_iota(jnp.int32, (tq, tq), 1)
              <= lax.broadcasted_iota(jnp.int32, (tq, tq), 0))
    m, l, acc = update(carry, ks_ref[0, pl.ds(start, tq), :], vs_ref[0, pl.ds(start, tq), :],
                       cks_ref[0, 0, :, pl.ds(start, tq)], causal)
    o_ref[0] = (acc / l).astype(o_ref.dtype)


def _fox(z3, b_off, qcol, kcol, vcol, cq, kp, vp, kp_col, vp_col, ckp, cks, *, p_shared, tq):
    nb, t, _ = cq.shape
    nh = cks.shape[1]
    plen = kp.shape[1]
    pchunk = _pick(plen, 512)
    pb = (lambda b: 0) if p_shared else (lambda b: b)
    kern = functools.partial(_fox_kernel, pchunk=pchunk, scale=HEAD_DIM ** -0.5)
    return pl.pallas_call(
        kern,
        grid=(nb, nh, t // tq),
        in_specs=[
            pl.BlockSpec((1, tq, HEAD_DIM), lambda b, h, i: (b + b_off, i, qcol + h)),
            pl.BlockSpec((1, tq, LANES), lambda b, h, i: (b, i, 0)),
            pl.BlockSpec((1, plen, HEAD_DIM), lambda b, h, i: (pb(b), 0, kp_col + h)),
            pl.BlockSpec((1, plen, HEAD_DIM), lambda b, h, i: (pb(b), 0, vp_col + h)),
            pl.BlockSpec((1, 1, 1, plen), lambda b, h, i: (pb(b), h, 0, 0)),
            pl.BlockSpec((1, t, HEAD_DIM), lambda b, h, i: (b + b_off, 0, kcol + h)),
            pl.BlockSpec((1, t, HEAD_DIM), lambda b, h, i: (b + b_off, 0, vcol + h)),
            pl.BlockSpec((1, 1, 1, t), lambda b, h, i: (b, h, 0, 0)),
        ],
        out_specs=pl.BlockSpec((1, tq, HEAD_DIM), lambda b, h, i: (b, i, h)),
        out_shape=jax.ShapeDtypeStruct((nb, t, nh * HEAD_DIM), BF16),
        compiler_params=_params(("parallel", "parallel", "arbitrary")),
        name="fox",
    )(z3, cq, kp, vp, ckp, z3, z3, cks)


def _bias_pieces(c):
    hi = c.astype(BF16).astype(F32)
    r = c - hi
    mid = r.astype(BF16).astype(F32)
    return hi, mid, r - mid


def _bias_rows(c, ones_first):
    hi, mid, lo = _bias_pieces(c)
    sub = lax.broadcasted_iota(jnp.int32, (HEAD_DIM, c.shape[1]), 0)
    third = jnp.where(sub >= 3, sub - 3, sub)
    pieces = jnp.where(third == 0, hi, jnp.where(third == 1, mid, lo))
    first, second = (1.0, pieces) if ones_first else (pieces, 1.0)
    return jnp.where(sub < 3, first, jnp.where(sub < 6, second, 0.0))


def _bias_cols(c):
    hi, mid, lo = _bias_pieces(c)
    lane = lax.broadcasted_iota(jnp.int32, (c.shape[0], LANES), 1)
    pieces = jnp.where(lane == 3, hi, jnp.where(lane == 4, mid, lo))
    return jnp.where(lane < 3, 1.0, jnp.where(lane < 6, pieces, 0.0))


def _fox_prompt_kernel(q_ref, cq_ref, k_ref, v_ref, ck_ref, ckp_ref, o_ref,
                       kaug_scr, vt_scr, sa_scr, sb_scr, acc_scr, *, scale):
    h = pl.program_id(1)
    qi = pl.program_id(2)
    tq = q_ref.shape[1]
    tk = sa_scr.shape[0]
    plen = ckp_ref.shape[1]
    t = k_ref.shape[1] - plen

    @pl.when(qi == 0)
    def _():
        def prep(cix, carry):
            r0 = pl.multiple_of(cix * tk, tk)
            src = pl.ds(pl.multiple_of(plen + r0, 8), tk)
            ck = ck_ref[0, 0, :, pl.ds(r0, tk)] * LOG2E
            kaug_scr[pl.ds(r0, tk), HEAD_DIM:] = _bias_rows(-ck, True).T.astype(BF16)
            kaug_scr[pl.ds(r0, tk), :HEAD_DIM] = k_ref[0, src, :].astype(BF16)
            vt_scr[:, pl.ds(r0, tk)] = v_ref[0, src, :].T.astype(BF16)
            return carry
        lax.fori_loop(0, t // tk, prep, 0)

    qt = (q_ref[0] * (scale * LOG2E)).T.astype(BF16)
    augq = _bias_rows(cq_ref[0, 0] * LOG2E, False).astype(BF16)
    qaug = jnp.concatenate([qt, augq], axis=0)

    def row0(blk):
        return pl.multiple_of(blk * tk, tk)

    def colmax(s):
        return jnp.max(s, axis=0, keepdims=True)

    def stage(blk, s_scr):
        s = jnp.dot(kaug_scr[pl.ds(row0(blk), tk), :], qaug, preferred_element_type=F32)
        s_scr[...] = s
        return colmax(s)

    def pv_block(blk):
        return lambda p: jnp.dot(vt_scr[:, pl.ds(row0(blk), tk)], p, preferred_element_type=F32)

    def absorb(ml, blocks):
        m, l = ml
        m_new = m
        for _, smax, _ in blocks:
            m_new = jnp.maximum(m_new, smax)
        alpha = jnp.exp2(m - m_new)
        l = alpha * l
        pv = None
        for s, _, pv_fn in blocks:
            p = jnp.exp2(s - m_new)
            l = l + jnp.sum(p, axis=0, keepdims=True)
            term = pv_fn(p.astype(BF16))
            pv = term if pv is None else pv + term
        acc_scr[...] = alpha * acc_scr[...] + pv
        return m_new, l

    acc_scr[...] = jnp.zeros_like(acc_scr)
    ml = (jnp.full((1, tq), NEG_BIG, F32), jnp.zeros((1, tq), F32))

    def body(pair, state):
        ma, ml = state
        mb = stage(2 * pair + 1, sb_scr)
        ml = absorb(ml, [(sa_scr[...], ma, pv_block(2 * pair))])
        ma = stage(2 * pair + 2, sa_scr)
        ml = absorb(ml, [(sb_scr[...], mb, pv_block(2 * pair + 1))])
        return ma, ml

    _, ml = lax.fori_loop(0, qi, body, (stage(0, sa_scr), ml))

    stage(2 * qi + 1, sb_scr)
    krow = lax.broadcasted_iota(jnp.int32, (tk, tq), 0)
    qcolumn = lax.broadcasted_iota(jnp.int32, (tk, tq), 1)
    s_a = jnp.where(krow <= qcolumn, sa_scr[...], -jnp.inf)
    s_b = jnp.where(krow + tk <= qcolumn, sb_scr[...], -jnp.inf)
    lane = lax.broadcasted_iota(jnp.int32, ckp_ref.shape[1:], 1)
    ckp = jnp.sum(jnp.where(lane == h, ckp_ref[0], 0.0), axis=-1, keepdims=True) * LOG2E
    kp = jnp.concatenate([k_ref[0, :plen, :].astype(BF16), _bias_cols(-ckp).astype(BF16)], axis=1)
    vp = v_ref[0, :plen, :].astype(BF16)
    s_p = jnp.dot(kp, qaug, preferred_element_type=F32)
    m, l = absorb(ml, [
        (s_p, colmax(s_p), lambda p: lax.dot_general(vp, p, (((0,), (0,)), ((), ())),
                                                     preferred_element_type=F32)),
        (s_a, colmax(s_a), pv_block(2 * qi)),
        (s_b, colmax(s_b), pv_block(2 * qi + 1)),
    ])
    o_ref[0] = (acc_scr[...] * (1.0 / l)).T.astype(o_ref.dtype)


def _fox_prompt(z3, qcol, k_cache, v_cache, c_row, cp_col, *, tq):
    nb, t, _ = z3.shape
    nh = c_row.shape[1]
    plen = cp_col.shape[1]
    kern = functools.partial(_fox_prompt_kernel, scale=HEAD_DIM ** -0.5)
    return pl.pallas_call(
        kern,
        grid=(nb, nh, t // tq),
        in_specs=[
            pl.BlockSpec((1, tq, HEAD_DIM), lambda b, h, i: (b, i, qcol + h)),
            pl.BlockSpec((1, 1, 1, tq), lambda b, h, i: (b, h, 0, i)),
            pl.BlockSpec((1, plen + t, HEAD_DIM), lambda b, h, i: (b, 0, h)),
            pl.BlockSpec((1, plen + t, HEAD_DIM), lambda b, h, i: (b, 0, h)),
            pl.BlockSpec((1, 1, 1, t), lambda b, h, i: (b, h, 0, 0)),
            pl.BlockSpec((1, plen, LANES), lambda b, h, i: (0, 0, 0)),
        ],
        out_specs=pl.BlockSpec((1, tq, HEAD_DIM), lambda b, h, i: (b, i, h)),
        out_shape=jax.ShapeDtypeStruct((nb, t, nh * HEAD_DIM), BF16),
        scratch_shapes=[pltpu.VMEM((t, 2 * HEAD_DIM), BF16), pltpu.VMEM((HEAD_DIM, t), BF16),
                        pltpu.VMEM((tq // 2, tq), F32), pltpu.VMEM((tq // 2, tq), F32),
                        pltpu.VMEM((HEAD_DIM, tq), F32)],
        compiler_params=_params(("parallel", "parallel", "arbitrary")),
        name="fox_prompt",
    )(z3, c_row, k_cache, v_cache, c_row, cp_col)


def _outproj_kernel(h_ref, a_ref, b_ref, wa_ref, wb_ref, g_ref, o_ref):
    mix = (jnp.dot(a_ref[...], wa_ref[...], preferred_element_type=F32)
           + jnp.dot(b_ref[...], wb_ref[...], preferred_element_type=F32))
    o_ref[...] = h_ref[...] + _rms(mix, g_ref[...])


def _outproj(h, mix_a, mix_b, w_a, w_b, gain):
    rows, d = h.shape
    ka, kb = mix_a.shape[1], mix_b.shape[1]
    tm = _pick(rows, 512)
    return pl.pallas_call(
        _outproj_kernel,
        grid=(rows // tm,),
        in_specs=[
            pl.BlockSpec((tm, d), lambda i: (i, 0)),
            pl.BlockSpec((tm, ka), lambda i: (i, 0)),
            pl.BlockSpec((tm, kb), lambda i: (i, 0)),
            pl.BlockSpec((ka, d), lambda i: (0, 0)),
            pl.BlockSpec((kb, d), lambda i: (0, 0)),
            pl.BlockSpec((1, d), lambda i: (0, 0)),
        ],
        out_specs=pl.BlockSpec((tm, d), lambda i: (i, 0)),
        out_shape=jax.ShapeDtypeStruct((rows, d), F32),
        compiler_params=_params(("parallel",)),
        name="outproj",
    )(h, mix_a, mix_b, w_a, w_b, gain)


def _mlp_kernel(h_ref, gpre_ref, wu_ref, wd_ref, gpost_ref, o_ref, a_scr, acc_scr):
    j = pl.program_id(1)

    @pl.when(j == 0)
    def _():
        a_scr[...] = _rms(h_ref[...], gpre_ref[...]).astype(BF16)
        acc_scr[...] = jnp.zeros_like(acc_scr)

    u = jnp.maximum(jnp.dot(a_scr[...], wu_ref[...], preferred_element_type=F32), 0.0)
    acc_scr[...] += jnp.dot((u * u).astype(BF16), wd_ref[...], preferred_element_type=F32)

    @pl.when(j == pl.num_programs(1) - 1)
    def _():
        o_ref[...] = h_ref[...] + _rms(acc_scr[...], gpost_ref[...])


def _mlp(h, g_pre, w_up, w_down, g_post):
    rows, d = h.shape
    dff = w_up.shape[1]
    tm = _pick(rows, 512)
    tf = _pick(dff, 512)
    return pl.pallas_call(
        _mlp_kernel,
        grid=(rows // tm, dff // tf),
        in_specs=[
            pl.BlockSpec((tm, d), lambda i, j: (i, 0)),
            pl.BlockSpec((1, d), lambda i, j: (0, 0)),
            pl.BlockSpec((d, tf), lambda i, j: (0, j)),
            pl.BlockSpec((tf, d), lambda i, j: (j, 0)),
            pl.BlockSpec((1, d), lambda i, j: (0, 0)),
        ],
        out_specs=pl.BlockSpec((tm, d), lambda i, j: (i, 0)),
        out_shape=jax.ShapeDtypeStruct((rows, d), F32),
        scratch_shapes=[pltpu.VMEM((tm, d), BF16), pltpu.VMEM((tm, d), F32)],
        compiler_params=_params(("parallel", "arbitrary")),
        name="mlp",
    )(h, g_pre, w_up, w_down, g_post)


def _col_form(c_row):
    c_col = jnp.transpose(c_row, (0, 2, 1))
    return jnp.pad(c_col, ((0, 0), (0, 0), (0, LANES - c_col.shape[2])))


def kernel(x_prompt, x_sample, cache_fox_k, cache_fox_v, cache_fox_logf, state_hgrn, meta_tokens,
           g_mix_pre, w_in, hg_lb_logits, hg_norm_gain, fox_f_bias, w_out, g_mix_post, g_mlp_pre,
           w_up, w_down, g_mlp_post):
    bp, seq, d = x_prompt.shape
    bs, dseq, _ = x_sample.shape
    n_meta = meta_tokens.shape[0]
    past = cache_fox_k.shape[2]
    nh = cache_fox_k.shape[3]
    hw = nh * HEAD_DIM
    n_main = w_in.shape[2] - nh
    assert dseq == n_meta, "sample frames and meta tokens share the small-stream kernels"
    assert state_hgrn.shape[2] == nh and n_main == 7 * hw

    w_main = w_in[0, :, :n_main].astype(BF16)
    w_f = jnp.pad(w_in[0, :, n_main:], ((0, 0), (0, LANES - nh))).astype(BF16)
    f_bias = jnp.pad(fox_f_bias[0], (0, LANES - nh))[None, :]
    w_oa = w_out[0, :hw].astype(BF16)
    w_ob = w_out[0, hw:].astype(BF16)
    w_u = w_up[0].astype(BF16)
    w_d = w_down[0].astype(BF16)
    g_pre = g_mix_pre[0][None, :]
    g_post = g_mix_post[0][None, :]
    g_mpre = g_mlp_pre[0][None, :]
    g_mpost = g_mlp_post[0][None, :]
    lb_logits = jnp.transpose(hg_lb_logits.reshape(-1, nh, HEAD_DIM), (1, 0, 2))
    hg_gain = hg_norm_gain[0].reshape(nh, 1, HEAD_DIM)
    qcol, kcol, vcol = 4 * nh, 5 * nh, 6 * nh

    ns = bs + 1
    xs = jnp.concatenate([meta_tokens, x_sample.reshape(bs * dseq, d)], axis=0)
    zs, lfs = _inproj(xs, g_pre, w_main, w_f, f_bias)
    zs3 = zs.reshape(ns, dseq, n_main)
    lfs3 = lfs.reshape(ns, dseq, LANES)

    c_past = _cumsum_time(cache_fox_logf[0], jnp.zeros((bs, nh), F32), nh)
    init_s = jnp.concatenate([jnp.zeros((1, nh), F32), c_past[:, :, past - 1]], axis=0)
    c_small = _cumsum_time(lfs3, init_s, nh)

    s0_small = jnp.concatenate([jnp.zeros((1,) + state_hgrn.shape[2:], F32), state_hgrn[0]], axis=0)
    s0_small_t = jnp.swapaxes(s0_small, -1, -2)
    mix_hg_s, sfin_s_t = _hgrn(zs3, lb_logits, hg_gain, s0_small_t, s0_shared=False,
                               chunk=dseq, tt=dseq)

    kc = cache_fox_k[0].reshape(bs, past, hw)
    vc = cache_fox_v[0].reshape(bs, past, hw)
    mix_fox_s = _fox(zs3, 1, qcol, kcol, vcol, _col_form(c_small[1:]), kc, vc, 0, 0,
                     c_past[:, :, None, :], c_small[1:, :, None, :], p_shared=False, tq=dseq)

    xm = x_prompt.reshape(bp * seq, d)
    meta_k = zs3[0, :, kcol * HEAD_DIM:kcol * HEAD_DIM + hw]
    meta_v = zs3[0, :, vcol * HEAD_DIM:vcol * HEAD_DIM + hw]
    zm, lfm, k_cache, v_cache = _inproj_kv(xm, g_pre, w_main, w_f, f_bias, meta_k, meta_v,
                                           nseq=bp, nz=kcol // nh)
    zm3 = zm.reshape(bp, seq, kcol * HEAD_DIM)
    lfm3 = lfm.reshape(bp, seq, LANES)
    c_meta = c_small[0:1]
    init_m = jnp.broadcast_to(c_meta[:, :, n_meta - 1], (bp, nh))
    c_main = _cumsum_time(lfm3, init_m, nh)

    mix_hg_m, sfin_m_t = _hgrn(zm3, lb_logits, hg_gain, sfin_s_t[0:1], s0_shared=True,
                               chunk=_pick(seq, 128), tt=_pick(seq, 512))
    mix_fox_m = _fox_prompt(zm3, qcol, k_cache, v_cache, c_main[:, :, None, :], _col_form(c_meta),
                            tq=_pick(seq, 1024))

    h1m = _outproj(xm, mix_hg_m.reshape(bp * seq, hw), mix_fox_m.reshape(bp * seq, hw),
                   w_oa, w_ob, g_post)
    y_prompt = _mlp(h1m, g_mpre, w_u, w_d, g_mpost).reshape(bp, seq, d)
    h1s = _outproj(x_sample.reshape(bs * dseq, d), mix_hg_s[1:].reshape(bs * dseq, hw),
                   mix_fox_s.reshape(bs * dseq, hw), w_oa, w_ob, g_post)
    y_sample = _mlp(h1s, g_mpre, w_u, w_d, g_mpost).reshape(bs, dseq, d)

    meta_lf = jnp.broadcast_to(lfs3[0:1, :, :nh], (bp, n_meta, nh))
    lf_p = jnp.concatenate([meta_lf, lfm3[:, :, :nh]], axis=1)[None]
    k_s = zs3[1:, :, kcol * HEAD_DIM:kcol * HEAD_DIM + hw]
    v_s = zs3[1:, :, vcol * HEAD_DIM:vcol * HEAD_DIM + hw]
    return (y_prompt, y_sample,
            k_cache.reshape(1, bp, n_meta + seq, nh, HEAD_DIM),
            v_cache.reshape(1, bp, n_meta + seq, nh, HEAD_DIM),
            lf_p,
            jnp.swapaxes(sfin_m_t, -1, -2)[None],
            k_s.reshape(1, bs, dseq, nh, HEAD_DIM),
            v_s.reshape(1, bs, dseq, nh, HEAD_DIM),
            lfs3[1:, :, :nh][None],
            jnp.swapaxes(sfin_s_t[1:], -1, -2)[None])
```

```python
import functools

import jax
import jax.numpy as jnp
from jax import lax
from jax.experimental import pallas as pl
from jax.experimental.pallas import tpu as pltpu

F32 = jnp.float32
BF16 = jnp.bfloat16

EPS = 1e-6
HEAD_DIM = 128
LANES = 128
NEG_BIG = -1e30
LOG2E = 1.4426950408889634
VMEM_LIMIT = 56 * 1024 * 1024


def _pick(n, pref):
    if n <= pref:
        return n
    t = pref
    while n % t:
        t //= 2
    return t


def _params(sem):
    return pltpu.CompilerParams(dimension_semantics=sem, vmem_limit_bytes=VMEM_LIMIT)


def _log_sigmoid(x):
    return jnp.minimum(x, 0.0) - jnp.log1p(jnp.exp(-jnp.abs(x)))


def _rms(x, gain):
    ms = jnp.mean(x * x, axis=-1, keepdims=True)
    return x * lax.rsqrt(ms + EPS) * gain


def _inproj_kernel(x_ref, g_ref, w_ref, wf_ref, fb_ref, z_ref, lf_ref, a_scr):
    @pl.when(pl.program_id(1) == 0)
    def _():
        a = _rms(x_ref[...], g_ref[...]).astype(BF16)
        a_scr[...] = a
        ff = jnp.dot(a, wf_ref[...], preferred_element_type=F32) + fb_ref[...]
        lf_ref[...] = _log_sigmoid(ff)

    z_ref[...] = jnp.dot(a_scr[...], w_ref[...], preferred_element_type=F32)


def _inproj(x, gain, w_main, w_f, f_bias):
    rows, d = x.shape
    n = w_main.shape[1]
    tm = _pick(rows, 1024)
    tn = _pick(n, 1024)
    return pl.pallas_call(
        _inproj_kernel,
        grid=(rows // tm, n // tn),
        in_specs=[
            pl.BlockSpec((tm, d), lambda i, j: (i, 0)),
            pl.BlockSpec((1, d), lambda i, j: (0, 0)),
            pl.BlockSpec((d, tn), lambda i, j: (0, j)),
            pl.BlockSpec((d, LANES), lambda i, j: (0, 0)),
            pl.BlockSpec((1, LANES), lambda i, j: (0, 0)),
        ],
        out_specs=[
            pl.BlockSpec((tm, tn), lambda i, j: (i, j)),
            pl.BlockSpec((tm, LANES), lambda i, j: (i, 0)),
        ],
        out_shape=[
            jax.ShapeDtypeStruct((rows, n), F32),
            jax.ShapeDtypeStruct((rows, LANES), F32),
        ],
        scratch_shapes=[pltpu.VMEM((tm, d), BF16)],
        compiler_params=_params(("parallel", "arbitrary")),
        name="inproj",
    )(x, gain, w_main, w_f, f_bias)


def _inproj_kv_kernel(x_ref, g_ref, w_ref, wf_ref, fb_ref, pk_ref, pv_ref,
                      z_ref, lf_ref, k_hbm, v_hbm, a_scr, stage, sems,
                      *, nz, tiles_per_seq):
    i = pl.program_id(0)
    j = pl.program_id(1)
    tm = x_ref.shape[0]
    plen = pk_ref.shape[0]
    last = pl.num_programs(0) - 1

    nh = k_hbm.shape[2]

    class _TileCopy:
        def __init__(self, slot, dst, tile):
            row = (tile % tiles_per_seq) * tm + plen
            self.copies = [
                pltpu.make_async_copy(stage.at[slot, :, pl.ds(hd * HEAD_DIM, HEAD_DIM)],
                                      dst.at[tile // tiles_per_seq, pl.ds(row, tm), hd, :],
                                      sems.at[slot])
                for hd in range(nh)]

        def start(self):
            for c in self.copies:
                c.start()

        def wait(self):
            for c in self.copies:
                c.wait()

    tile_copy = _TileCopy

    def prefix_copy(slot, src, dst, tile):
        return pltpu.make_async_copy(src, dst.at[tile // tiles_per_seq, pl.ds(0, plen), :, :],
                                     sems.at[2 + slot])

    def opens_sequence(tile):
        return tile % tiles_per_seq == 0

    @pl.when(j == 0)
    def _():
        a = _rms(x_ref[...], g_ref[...]).astype(BF16)
        a_scr[...] = a
        ff = jnp.dot(a, wf_ref[...], preferred_element_type=F32) + fb_ref[...]
        lf_ref[...] = _log_sigmoid(ff)

    tile = jnp.dot(a_scr[...], w_ref[...], preferred_element_type=F32)

    @pl.when(j < nz)
    def _():
        z_ref[...] = tile

    @pl.when(j == nz)
    def _():
        @pl.when(i > 0)
        def _():
            tile_copy(1, v_hbm, i - 1).wait()

            @pl.when(opens_sequence(i - 1))
            def _():
                prefix_copy(1, pv_ref, v_hbm, i - 1).wait()
        stage[0] = tile
        tile_copy(0, k_hbm, i).start()

        @pl.when(opens_sequence(i))
        def _():
            prefix_copy(0, pk_ref, k_hbm, i).start()

    @pl.when(j == nz + 1)
    def _():
        stage[1] = tile
        tile_copy(1, v_hbm, i).start()
        tile_copy(0, k_hbm, i).wait()

        @pl.when(opens_sequence(i))
        def _():
            prefix_copy(1, pv_ref, v_hbm, i).start()
            prefix_copy(0, pk_ref, k_hbm, i).wait()

        @pl.when(i == last)
        def _():
            tile_copy(1, v_hbm, i).wait()

            @pl.when(opens_sequence(i))
            def _():
                prefix_copy(1, pv_ref, v_hbm, i).wait()


def _inproj_kv(x, gain, w_main, w_f, f_bias, prefix_k, prefix_v, *, nseq, nz):
    rows, d = x.shape
    n = w_main.shape[1]
    tn = n // (nz + 2)
    seq = rows // nseq
    plen, nh, _ = prefix_k.shape
    tm = _pick(seq, 1024)
    kern = functools.partial(_inproj_kv_kernel, nz=nz, tiles_per_seq=seq // tm)
    cache = jax.ShapeDtypeStruct((nseq, plen + seq, nh, HEAD_DIM), F32)
    return pl.pallas_call(
        kern,
        grid=(rows // tm, nz + 2),
        in_specs=[
            pl.BlockSpec((tm, d), lambda i, j: (i, 0)),
            pl.BlockSpec((1, d), lambda i, j: (0, 0)),
            pl.BlockSpec((d, tn), lambda i, j: (0, j)),
            pl.BlockSpec((d, LANES), lambda i, j: (0, 0)),
            pl.BlockSpec((1, LANES), lambda i, j: (0, 0)),
            pl.BlockSpec((plen, nh, HEAD_DIM), lambda i, j: (0, 0, 0)),
            pl.BlockSpec((plen, nh, HEAD_DIM), lambda i, j: (0, 0, 0)),
        ],
        out_specs=[
            pl.BlockSpec((tm, tn), lambda i, j: (i, jnp.minimum(j, nz - 1))),
            pl.BlockSpec((tm, LANES), lambda i, j: (i, 0)),
            pl.BlockSpec(memory_space=pl.ANY),
            pl.BlockSpec(memory_space=pl.ANY),
        ],
        out_shape=[
            jax.ShapeDtypeStruct((rows, nz * tn), F32),
            jax.ShapeDtypeStruct((rows, LANES), F32),
            cache, cache,
        ],
        scratch_shapes=[pltpu.VMEM((tm, d), BF16), pltpu.VMEM((2, tm, tn), F32),
                        pltpu.SemaphoreType.DMA((4,))],
        compiler_params=_params(("arbitrary", "arbitrary")),
        name="inproj_kv",
    )(x, gain, w_main, w_f, f_bias, prefix_k, prefix_v)


def _split3(x):
    hi = x.astype(BF16)
    r = x - hi.astype(F32)
    mid = r.astype(BF16)
    lo = (r - mid.astype(F32)).astype(BF16)
    return hi, mid, lo


def _cumsum_kernel(x_ref, init_ref, c_ref, *, chain):
    g, nblk, blk = x_ref.shape
    upper = (lax.broadcasted_iota(jnp.int32, (blk, blk), 0)
             <= lax.broadcasted_iota(jnp.int32, (blk, blk), 1)).astype(BF16)
    if chain:
        before = (lax.broadcasted_iota(jnp.int32, (nblk, nblk), 1)
                  < lax.broadcasted_iota(jnp.int32, (nblk, nblk), 0)).astype(BF16)
    for r in range(g):
        c = sum(jnp.dot(p, upper, preferred_element_type=F32) for p in _split3(x_ref[r]))
        if chain:
            tot = jnp.broadcast_to(c[:, blk - 1:blk], (nblk, blk))
            c = c + sum(jnp.dot(before, p, preferred_element_type=F32) for p in _split3(tot))
        c_ref[r] = c + init_ref[r]


def _cumsum(x, init, *, chain):
    n, nblk, blk = x.shape
    g = _pick(n, 8)
    ni = init.shape[1]
    return pl.pallas_call(
        functools.partial(_cumsum_kernel, chain=chain),
        grid=(n // g,),
        in_specs=[
            pl.BlockSpec((g, nblk, blk), lambda i: (i, 0, 0)),
            pl.BlockSpec((g, ni, blk), lambda i: (i, 0, 0)),
        ],
        out_specs=pl.BlockSpec((g, nblk, blk), lambda i: (i, 0, 0)),
        out_shape=jax.ShapeDtypeStruct((n, nblk, blk), F32),
        compiler_params=_params(("parallel",)),
        name="cumsum",
    )(x, init)


def _cumsum_time(lf, init, nh):
    nb, t = lf.shape[0], lf.shape[1]
    blk = _pick(t, 256)
    x = jnp.transpose(lf[:, :, :nh], (0, 2, 1))
    if t == blk:
        init3 = jnp.broadcast_to(init.reshape(1, nb * nh, 1), (1, nb * nh, blk))
        return _cumsum(x.reshape(1, nb * nh, blk), init3, chain=False).reshape(nb, nh, t)
    init3 = jnp.broadcast_to(init.reshape(nb * nh, 1, 1), (nb * nh, 1, blk))
    return _cumsum(x.reshape(nb * nh, t // blk, blk), init3, chain=True).reshape(nb, nh, t)


def _block_mid(pre, half):
    c = pre.shape[0]
    if half >= 8:
        x3 = pre.reshape(c // (2 * half), 2 * half, HEAD_DIM)
        return jnp.broadcast_to(x3[:, half - 1:half, :], x3.shape).reshape(c, HEAD_DIM)
    x3 = pre.reshape(c // 8, 8, HEAD_DIM)
    sub = lax.broadcasted_iota(jnp.int32, x3.shape, 1)
    mid = None
    for blk in range(8 // (2 * half)):
        src = 2 * half * blk + half - 1
        piece = jnp.broadcast_to(x3[:, src:src + 1, :], x3.shape)
        mid = piece if mid is None else jnp.where(sub >= 2 * half * blk, piece, mid)
    return mid.reshape(c, HEAD_DIM)


def _hgrn_masks(c):
    rows = lax.broadcasted_iota(jnp.int32, (c, HEAD_DIM), 0)
    ri = lax.broadcasted_iota(jnp.int32, (c, c), 0)
    ci = lax.broadcasted_iota(jnp.int32, (c, c), 1)
    levels = c.bit_length() - 1
    upper = [(rows & (1 << (lvl - 1))) != 0 for lvl in range(1, levels + 1)]
    owner = jnp.where(ri >= ci, 32 - lax.clz(ri ^ ci), -1)
    return owner, upper


def _hgrn_chunk(hq, hf, hi, lb, st, masks):
    c = hq.shape[0]
    owner, upper_masks = masks
    sg = jax.nn.sigmoid(hf)
    q = hq * jax.nn.sigmoid(hq)
    g2 = jnp.log(lb + (1.0 - lb) * sg) * LOG2E
    k = (1.0 - lb) * (1.0 - sg)

    scores = jnp.where(owner == 0, jnp.sum(q * k, axis=-1, keepdims=True), 0.0)
    pre = g2
    for lvl, upper in enumerate(upper_masks, start=1):
        mid = _block_mid(pre, 1 << (lvl - 1))
        w = jnp.exp2(jnp.where(upper, pre, mid - pre))
        x = (jnp.where(upper, q, k) * w).astype(BF16)
        s_l = lax.dot_general(x, x, (((1,), (1,)), ((), ())), preferred_element_type=F32)
        scores = jnp.where(owner == lvl, s_l, scores)
        pre = jnp.where(upper, pre + mid, pre)
    cum = pre
    v = hi.astype(BF16)
    qe = (q * jnp.exp2(cum)).astype(BF16)
    o = (lax.dot_general(qe, st.astype(BF16), (((1,), (1,)), ((), ())), preferred_element_type=F32)
         + jnp.dot(scores.astype(BF16), v, preferred_element_type=F32))
    last = cum[c - 1:c, :]
    kd = (k * jnp.exp2(last - cum)).astype(BF16)
    st_new = st * jnp.exp2(last) + lax.dot_general(
        v, kd, (((0,), (0,)), ((), ())), preferred_element_type=F32)
    return o, st_new


def _hgrn_kernel(hq_ref, hf_ref, hi_ref, hg_ref, lbl_ref, gain_ref, s0_ref,
                 o_ref, sfin_ref, st_scr, *, chunk):
    ti = pl.program_id(2)
    hps = st_scr.shape[0]

    @pl.when(ti == 0)
    def _():
        st_scr[...] = s0_ref[0]

    tt = hq_ref.shape[1]
    masks = _hgrn_masks(chunk)
    for hh in range(hps):
        cols = slice(hh * HEAD_DIM, (hh + 1) * HEAD_DIM)
        lg = lbl_ref[hh]
        e = jnp.exp(lg - jnp.max(lg, axis=0, keepdims=True))
        lb = e[0:1, :] / jnp.sum(e, axis=0, keepdims=True)
        gain = gain_ref[hh]
        st = st_scr[hh]
        for cidx in range(tt // chunk):
            sl = slice(cidx * chunk, (cidx + 1) * chunk)
            o, st = _hgrn_chunk(hq_ref[0, sl, cols], hf_ref[0, sl, cols], hi_ref[0, sl, cols],
                                lb, st, masks)
            hg = hg_ref[0, sl, cols]
            o_ref[0, sl, cols] = (_rms(o, gain) * (hg * jax.nn.sigmoid(hg))).astype(o_ref.dtype)
        st_scr[hh] = st

        @pl.when(ti == pl.num_programs(2) - 1)
        def _():
            sfin_ref[0, hh] = st


def _hgrn(z3, lb_logits, gain, s0t, *, s0_shared, chunk, tt, hps):
    nb, t, _ = z3.shape
    nh = gain.shape[0]
    nslot = lb_logits.shape[1]
    ng = nh // hps
    s0_map = (lambda b, h, i: (0, h, 0, 0)) if s0_shared else (lambda b, h, i: (b, h, 0, 0))

    def col(group_off):
        return pl.BlockSpec((1, tt, hps * HEAD_DIM), lambda b, h, i: (b, i, group_off + h))

    return pl.pallas_call(
        functools.partial(_hgrn_kernel, chunk=chunk),
        grid=(nb, ng, t // tt),
        in_specs=[
            col(0), col(ng), col(2 * ng), col(3 * ng),
            pl.BlockSpec((hps, nslot, HEAD_DIM), lambda b, h, i: (h, 0, 0)),
            pl.BlockSpec((hps, 1, HEAD_DIM), lambda b, h, i: (h, 0, 0)),
            pl.BlockSpec((1, hps, HEAD_DIM, HEAD_DIM), s0_map),
        ],
        out_specs=[
            pl.BlockSpec((1, tt, hps * HEAD_DIM), lambda b, h, i: (b, i, h)),
            pl.BlockSpec((1, hps, HEAD_DIM, HEAD_DIM), lambda b, h, i: (b, h, 0, 0)),
        ],
        out_shape=[
            jax.ShapeDtypeStruct((nb, t, nh * HEAD_DIM), BF16),
            jax.ShapeDtypeStruct((nb, nh, HEAD_DIM, HEAD_DIM), F32),
        ],
        scratch_shapes=[pltpu.VMEM((hps, HEAD_DIM, HEAD_DIM), F32)],
        compiler_params=_params(("parallel", "parallel", "arbitrary")),
        name="hgrn",
    )(z3, z3, z3, z3, lb_logits, gain, s0t)


def _fox_kernel(q_ref, cq_ref, kp_ref, vp_ref, ckp_ref, ks_ref, vs_ref, cks_ref, o_ref,
                *, pchunk, scale):
    h = pl.program_id(1)
    qi = pl.program_id(2)
    tq = q_ref.shape[1]
    q = q_ref[0].astype(BF16)
    lane = lax.broadcasted_iota(jnp.int32, (tq, LANES), 1)
    cq = jnp.sum(jnp.where(lane == h, cq_ref[0], 0.0), axis=-1, keepdims=True)

    def update(carry, kb, vb, ck, mask):
        m, l, acc = carry
        s = lax.dot_general(q, kb.astype(BF16), (((1,), (1,)), ((), ())),
                            preferred_element_type=F32) * scale + (cq - ck)
        if mask is not None:
            s = jnp.where(mask, s, -jnp.inf)
        m_new = jnp.maximum(m, jnp.max(s, axis=-1, keepdims=True))
        alpha = jnp.exp(m - m_new)
        p = jnp.exp(s - m_new)
        l = alpha * l + jnp.sum(p, axis=-1, keepdims=True)
        acc = alpha * acc + jnp.dot(p.astype(BF16), vb.astype(BF16), preferred_element_type=F32)
        return m_new, l, acc

    carry = (jnp.full((tq, 1), NEG_BIG, F32), jnp.zeros((tq, 1), F32),
             jnp.zeros((tq, HEAD_DIM), F32))
    plen = kp_ref.shape[1]
    for pc in range(plen // pchunk):
        sl = slice(pc * pchunk, (pc + 1) * pchunk)
        carry = update(carry, kp_ref[0, sl, :], vp_ref[0, sl, :], ckp_ref[0, 0, :, sl], None)

    def body(j, carry):
        start = pl.multiple_of(j * tq, tq)
        return update(carry, ks_ref[0, pl.ds(start, tq), :], vs_ref[0, pl.ds(start, tq), :],
                      cks_ref[0, 0, :, pl.ds(start, tq)], None)

    if ks_ref.shape[1] == tq:
        start = 0
    else:
        carry = lax.fori_loop(0, qi, body, carry)
        start = pl.multiple_of(qi * tq, tq)
    causal = (lax.broadcasted_iota(jnp.int32, (tq, tq), 1)
              <= lax.broadcasted_iota(jnp.int32, (tq, tq), 0))
    m, l, acc = update(carry, ks_ref[0, pl.ds(start, tq), :], vs_ref[0, pl.ds(start, tq), :],
                       cks_ref[0, 0, :, pl.ds(start, tq)], causal)
    o_ref[0] = (acc / l).astype(o_ref.dtype)


def _fox(z3, b_off, qcol, kcol, vcol, cq, kp, vp, kp_col, vp_col, ckp, cks, *, p_shared, tq):
    nb, t, _ = cq.shape
    nh = cks.shape[1]
    plen = kp.shape[1]
    pchunk = _pick(plen, 512)
    pb = (lambda b: 0) if p_shared else (lambda b: b)
    kern = functools.partial(_fox_kernel, pchunk=pchunk, scale=HEAD_DIM ** -0.5)
    return pl.pallas_call(
        kern,
        grid=(nb, nh, t // tq),
        in_specs=[
            pl.BlockSpec((1, tq, HEAD_DIM), lambda b, h, i: (b + b_off, i, qcol + h)),
            pl.BlockSpec((1, tq, LANES), lambda b, h, i: (b, i, 0)),
            pl.BlockSpec((1, plen, HEAD_DIM), lambda b, h, i: (pb(b), 0, kp_col + h)),
            pl.BlockSpec((1, plen, HEAD_DIM), lambda b, h, i: (pb(b), 0, vp_col + h)),
            pl.BlockSpec((1, 1, 1, plen), lambda b, h, i: (pb(b), h, 0, 0)),
            pl.BlockSpec((1, t, HEAD_DIM), lambda b, h, i: (b + b_off, 0, kcol + h)),
            pl.BlockSpec((1, t, HEAD_DIM), lambda b, h, i: (b + b_off, 0, vcol + h)),
            pl.BlockSpec((1, 1, 1, t), lambda b, h, i: (b, h, 0, 0)),
        ],
        out_specs=pl.BlockSpec((1, tq, HEAD_DIM), lambda b, h, i: (b, i, h)),
        out_shape=jax.ShapeDtypeStruct((nb, t, nh * HEAD_DIM), BF16),
        compiler_params=_params(("parallel", "parallel", "arbitrary")),
        name="fox",
    )(z3, cq, kp, vp, ckp, z3, z3, cks)


def _bias_pieces(c):
    hi = c.astype(BF16).astype(F32)
    r = c - hi
    mid = r.astype(BF16).astype(F32)
    return hi, mid, r - mid


def _bias_rows(c, ones_first):
    hi, mid, lo = _bias_pieces(c)
    sub = lax.broadcasted_iota(jnp.int32, (HEAD_DIM, c.shape[1]), 0)
    third = jnp.where(sub >= 3, sub - 3, sub)
    pieces = jnp.where(third == 0, hi, jnp.where(third == 1, mid, lo))
    first, second = (1.0, pieces) if ones_first else (pieces, 1.0)
    return jnp.where(sub < 3, first, jnp.where(sub < 6, second, 0.0))


def _bias_cols(c):
    hi, mid, lo = _bias_pieces(c)
    lane = lax.broadcasted_iota(jnp.int32, (c.shape[0], LANES), 1)
    pieces = jnp.where(lane == 3, hi, jnp.where(lane == 4, mid, lo))
    return jnp.where(lane < 3, 1.0, jnp.where(lane < 6, pieces, 0.0))


def _fox_prompt_kernel(q_ref, cq_ref, k_hbm, v_hbm, ck_ref, ckp_ref, o_ref,
                       kaug_scr, vt_scr, sa_scr, sb_scr, acc_scr, kbuf, vbuf, sems, *, scale):
    h = pl.program_id(1)
    qi = pl.program_id(2)
    nheads = pl.num_programs(1)
    nslices = pl.num_programs(0) * nheads
    lin = pl.program_id(0) * nheads + h
    slot = lin % 2
    tq = q_ref.shape[1]
    tk = sa_scr.shape[0]
    plen = ckp_ref.shape[1]
    t = kbuf.shape[1] - plen

    def slice_copies(idx):
        seq_idx, head = idx // nheads, idx % nheads
        return (pltpu.make_async_copy(k_hbm.at[seq_idx, :, head, :], kbuf.at[idx % 2],
                                      sems.at[0, idx % 2]),
                pltpu.make_async_copy(v_hbm.at[seq_idx, :, head, :], vbuf.at[idx % 2],
                                      sems.at[1, idx % 2]))

    @pl.when(qi == 0)
    def _():
        @pl.when(lin == 0)
        def _():
            for c in slice_copies(lin):
                c.start()
        for c in slice_copies(lin):
            c.wait()

        @pl.when(lin + 1 < nslices)
        def _():
            for c in slice_copies(lin + 1):
                c.start()

        def prep(cix, carry):
            r0 = pl.multiple_of(cix * tk, tk)
            src = pl.ds(pl.multiple_of(plen + r0, 8), tk)
            ck = ck_ref[0, 0, :, pl.ds(r0, tk)] * LOG2E
            kaug_scr[pl.ds(r0, tk), HEAD_DIM:] = _bias_rows(-ck, True).T.astype(BF16)
            kaug_scr[pl.ds(r0, tk), :HEAD_DIM] = kbuf[slot, src, :].astype(BF16)
            vt_scr[:, pl.ds(r0, tk)] = vbuf[slot, src, :].T.astype(BF16)
            return carry
        lax.fori_loop(0, t // tk, prep, 0)

    qt = (q_ref[0] * (scale * LOG2E)).T.astype(BF16)
    augq = _bias_rows(cq_ref[0, 0] * LOG2E, False).astype(BF16)
    qaug = jnp.concatenate([qt, augq], axis=0)

    def row0(blk):
        return pl.multiple_of(blk * tk, tk)

    def colmax(s):
        return jnp.max(s, axis=0, keepdims=True)

    def stage(blk, s_scr):
        s = jnp.dot(kaug_scr[pl.ds(row0(blk), tk), :], qaug, preferred_element_type=F32)
        s_scr[...] = s
        return colmax(s)

    def pv_block(blk):
        return lambda p: jnp.dot(vt_scr[:, pl.ds(row0(blk), tk)], p, preferred_element_type=F32)

    def absorb(ml, blocks):
        m, l = ml
        m_new = m
        for _, smax, _ in blocks:
            m_new = jnp.maximum(m_new, smax)
        alpha = jnp.exp2(m - m_new)
        l = alpha * l
        pv = None
        for s, _, pv_fn in blocks:
            p = jnp.exp2(s - m_new)
            l = l + jnp.sum(p, axis=0, keepdims=True)
            term = pv_fn(p.astype(BF16))
            pv = term if pv is None else pv + term
        acc_scr[...] = alpha * acc_scr[...] + pv
        return m_new, l

    acc_scr[...] = jnp.zeros_like(acc_scr)
    ml = (jnp.full((1, tq), NEG_BIG, F32), jnp.zeros((1, tq), F32))

    def body(pair, state):
        ma, ml = state
        mb = stage(2 * pair + 1, sb_scr)
        ml = absorb(ml, [(sa_scr[...], ma, pv_block(2 * pair))])
        ma = stage(2 * pair + 2, sa_scr)
        ml = absorb(ml, [(sb_scr[...], mb, pv_block(2 * pair + 1))])
        return ma, ml

    _, ml = lax.fori_loop(0, qi, body, (stage(0, sa_scr), ml))

    stage(2 * qi + 1, sb_scr)
    krow = lax.broadcasted_iota(jnp.int32, (tk, tq), 0)
    qcolumn = lax.broadcasted_iota(jnp.int32, (tk, tq), 1)
    s_a = jnp.where(krow <= qcolumn, sa_scr[...], -jnp.inf)
    s_b = jnp.where(krow + tk <= qcolumn, sb_scr[...], -jnp.inf)
    lane = lax.broadcasted_iota(jnp.int32, ckp_ref.shape[1:], 1)
    ckp = jnp.sum(jnp.where(lane == h, ckp_ref[0], 0.0), axis=-1, keepdims=True) * LOG2E
    kp = jnp.concatenate([kbuf[slot, :plen, :].astype(BF16), _bias_cols(-ckp).astype(BF16)], axis=1)
    vp = vbuf[slot, :plen, :].astype(BF16)
    s_p = jnp.dot(kp, qaug, preferred_element_type=F32)
    m, l = absorb(ml, [
        (s_p, colmax(s_p), lambda p: lax.dot_general(vp, p, (((0,), (0,)), ((), ())),
                                                     preferred_element_type=F32)),
        (s_a, colmax(s_a), pv_block(2 * qi)),
        (s_b, colmax(s_b), pv_block(2 * qi + 1)),
    ])
    o_ref[0] = (acc_scr[...] * (1.0 / l)).T.astype(o_ref.dtype)


def _fox_prompt(z3, qcol, k_cache, v_cache, c_row, cp_col, *, tq):
    nb, t, _ = z3.shape
    nh = c_row.shape[1]
    plen = cp_col.shape[1]
    kern = functools.partial(_fox_prompt_kernel, scale=HEAD_DIM ** -0.5)
    return pl.pallas_call(
        kern,
        grid=(nb, nh, t // tq),
        in_specs=[
            pl.BlockSpec((1, tq, HEAD_DIM), lambda b, h, i: (b, i, qcol + h)),
            pl.BlockSpec((1, 1, 1, tq), lambda b, h, i: (b, h, 0, i)),
            pl.BlockSpec(memory_space=pl.ANY),
            pl.BlockSpec(memory_space=pl.ANY),
            pl.BlockSpec((1, 1, 1, t), lambda b, h, i: (b, h, 0, 0)),
            pl.BlockSpec((1, plen, LANES), lambda b, h, i: (0, 0, 0)),
        ],
        out_specs=pl.BlockSpec((1, tq, HEAD_DIM), lambda b, h, i: (b, i, h)),
        out_shape=jax.ShapeDtypeStruct((nb, t, nh * HEAD_DIM), BF16),
        scratch_shapes=[pltpu.VMEM((t, 2 * HEAD_DIM), BF16), pltpu.VMEM((HEAD_DIM, t), BF16),
                        pltpu.VMEM((tq // 2, tq), F32), pltpu.VMEM((tq // 2, tq), F32),
                        pltpu.VMEM((HEAD_DIM, tq), F32),
                        pltpu.VMEM((2, plen + t, HEAD_DIM), F32),
                        pltpu.VMEM((2, plen + t, HEAD_DIM), F32),
                        pltpu.SemaphoreType.DMA((2, 2))],
        compiler_params=_params(("arbitrary", "arbitrary", "arbitrary")),
        name="fox_prompt",
    )(z3, c_row, k_cache, v_cache, c_row, cp_col)


def _outproj_kernel(h_ref, a_ref, b_ref, wa_ref, wb_ref, g_ref, o_ref):
    mix = (jnp.dot(a_ref[...], wa_ref[...], preferred_element_type=F32)
           + jnp.dot(b_ref[...], wb_ref[...], preferred_element_type=F32))
    o_ref[...] = h_ref[...] + _rms(mix, g_ref[...])


def _outproj(h, mix_a, mix_b, w_a, w_b, gain):
    rows, d = h.shape
    ka, kb = mix_a.shape[1], mix_b.shape[1]
    tm = _pick(rows, 512)
    return pl.pallas_call(
        _outproj_kernel,
        grid=(rows // tm,),
        in_specs=[
            pl.BlockSpec((tm, d), lambda i: (i, 0)),
            pl.BlockSpec((tm, ka), lambda i: (i, 0)),
            pl.BlockSpec((tm, kb), lambda i: (i, 0)),
            pl.BlockSpec((ka, d), lambda i: (0, 0)),
            pl.BlockSpec((kb, d), lambda i: (0, 0)),
            pl.BlockSpec((1, d), lambda i: (0, 0)),
        ],
        out_specs=pl.BlockSpec((tm, d), lambda i: (i, 0)),
        out_shape=jax.ShapeDtypeStruct((rows, d), F32),
        compiler_params=_params(("parallel",)),
        name="outproj",
    )(h, mix_a, mix_b, w_a, w_b, gain)


def _mlp_kernel(h_ref, gpre_ref, wu_ref, wd_ref, gpost_ref, o_ref, a_scr, acc_scr):
    j = pl.program_id(1)

    @pl.when(j == 0)
    def _():
        a_scr[...] = _rms(h_ref[...], gpre_ref[...]).astype(BF16)
        acc_scr[...] = jnp.zeros_like(acc_scr)

    u = jnp.maximum(jnp.dot(a_scr[...], wu_ref[...], preferred_element_type=F32), 0.0)
    acc_scr[...] += jnp.dot((u * u).astype(BF16), wd_ref[...], preferred_element_type=F32)

    @pl.when(j == pl.num_programs(1) - 1)
    def _():
        o_ref[...] = h_ref[...] + _rms(acc_scr[...], gpost_ref[...])


def _mlp(h, g_pre, w_up, w_down, g_post):
    rows, d = h.shape
    dff = w_up.shape[1]
    tm = _pick(rows, 512)
    tf = _pick(dff, 1024)
    return pl.pallas_call(
        _mlp_kernel,
        grid=(rows // tm, dff // tf),
        in_specs=[
            pl.BlockSpec((tm, d), lambda i, j: (i, 0)),
            pl.BlockSpec((1, d), lambda i, j: (0, 0)),
            pl.BlockSpec((d, tf), lambda i, j: (0, j)),
            pl.BlockSpec((tf, d), lambda i, j: (j, 0)),
            pl.BlockSpec((1, d), lambda i, j: (0, 0)),
        ],
        out_specs=pl.BlockSpec((tm, d), lambda i, j: (i, 0)),
        out_shape=jax.ShapeDtypeStruct((rows, d), F32),
        scratch_shapes=[pltpu.VMEM((tm, d), BF16), pltpu.VMEM((tm, d), F32)],
        compiler_params=_params(("parallel", "arbitrary")),
        name="mlp",
    )(h, g_pre, w_up, w_down, g_post)


def _col_form(c_row):
    c_col = jnp.transpose(c_row, (0, 2, 1))
    return jnp.pad(c_col, ((0, 0), (0, 0), (0, LANES - c_col.shape[2])))


def kernel(x_prompt, x_sample, cache_fox_k, cache_fox_v, cache_fox_logf, state_hgrn, meta_tokens,
           g_mix_pre, w_in, hg_lb_logits, hg_norm_gain, fox_f_bias, w_out, g_mix_post, g_mlp_pre,
           w_up, w_down, g_mlp_post):
    bp, seq, d = x_prompt.shape
    bs, dseq, _ = x_sample.shape
    n_meta = meta_tokens.shape[0]
    past = cache_fox_k.shape[2]
    nh = cache_fox_k.shape[3]
    hw = nh * HEAD_DIM
    n_main = w_in.shape[2] - nh
    assert dseq == n_meta, "sample frames and meta tokens share the small-stream kernels"
    assert state_hgrn.shape[2] == nh and n_main == 7 * hw

    w_main = w_in[0, :, :n_main].astype(BF16)
    w_f = jnp.pad(w_in[0, :, n_main:], ((0, 0), (0, LANES - nh))).astype(BF16)
    f_bias = jnp.pad(fox_f_bias[0], (0, LANES - nh))[None, :]
    w_oa = w_out[0, :hw].astype(BF16)
    w_ob = w_out[0, hw:].astype(BF16)
    w_u = w_up[0].astype(BF16)
    w_d = w_down[0].astype(BF16)
    g_pre = g_mix_pre[0][None, :]
    g_post = g_mix_post[0][None, :]
    g_mpre = g_mlp_pre[0][None, :]
    g_mpost = g_mlp_post[0][None, :]
    lb_logits = jnp.transpose(hg_lb_logits.reshape(-1, nh, HEAD_DIM), (1, 0, 2))
    hg_gain = hg_norm_gain[0].reshape(nh, 1, HEAD_DIM)
    qcol, kcol, vcol = 4 * nh, 5 * nh, 6 * nh

    ns = bs + 1
    xs = jnp.concatenate([meta_tokens, x_sample.reshape(bs * dseq, d)], axis=0)
    zs, lfs = _inproj(xs, g_pre, w_main, w_f, f_bias)
    zs3 = zs.reshape(ns, dseq, n_main)
    lfs3 = lfs.reshape(ns, dseq, LANES)

    c_past = _cumsum_time(cache_fox_logf[0], jnp.zeros((bs, nh), F32), nh)
    init_s = jnp.concatenate([jnp.zeros((1, nh), F32), c_past[:, :, past - 1]], axis=0)
    c_small = _cumsum_time(lfs3, init_s, nh)

    s0_small = jnp.concatenate([jnp.zeros((1,) + state_hgrn.shape[2:], F32), state_hgrn[0]], axis=0)
    s0_small_t = jnp.swapaxes(s0_small, -1, -2)
    mix_hg_s, sfin_s_t = _hgrn(zs3, lb_logits, hg_gain, s0_small_t, s0_shared=False,
                               chunk=dseq, tt=dseq, hps=nh)

    kc = cache_fox_k[0].reshape(bs, past, hw)
    vc = cache_fox_v[0].reshape(bs, past, hw)
    mix_fox_s = _fox(zs3, 1, qcol, kcol, vcol, _col_form(c_small[1:]), kc, vc, 0, 0,
                     c_past[:, :, None, :], c_small[1:, :, None, :], p_shared=False, tq=dseq)

    xm = x_prompt.reshape(bp * seq, d)
    meta_k = zs3[0, :, kcol * HEAD_DIM:kcol * HEAD_DIM + hw].reshape(n_meta, nh, HEAD_DIM)
    meta_v = zs3[0, :, vcol * HEAD_DIM:vcol * HEAD_DIM + hw].reshape(n_meta, nh, HEAD_DIM)
    zm, lfm, k_cache, v_cache = _inproj_kv(xm, g_pre, w_main, w_f, f_bias, meta_k, meta_v,
                                           nseq=bp, nz=kcol // nh)
    zm3 = zm.reshape(bp, seq, kcol * HEAD_DIM)
    lfm3 = lfm.reshape(bp, seq, LANES)
    c_meta = c_small[0:1]
    init_m = jnp.broadcast_to(c_meta[:, :, n_meta - 1], (bp, nh))
    c_main = _cumsum_time(lfm3, init_m, nh)

    mix_hg_m, sfin_m_t = _hgrn(zm3, lb_logits, hg_gain, sfin_s_t[0:1], s0_shared=True,
                               chunk=_pick(seq, 128), tt=_pick(seq, 512), hps=1)
    mix_fox_m = _fox_prompt(zm3, qcol, k_cache, v_cache, c_main[:, :, None, :], _col_form(c_meta),
                            tq=_pick(seq, 1024))

    h1m = _outproj(xm, mix_hg_m.reshape(bp * seq, hw), mix_fox_m.reshape(bp * seq, hw),
                   w_oa, w_ob, g_post)
    y_prompt = _mlp(h1m, g_mpre, w_u, w_d, g_mpost).reshape(bp, seq, d)
    h1s = _outproj(x_sample.reshape(bs * dseq, d), mix_hg_s[1:].reshape(bs * dseq, hw),
                   mix_fox_s.reshape(bs * dseq, hw), w_oa, w_ob, g_post)
    y_sample = _mlp(h1s, g_mpre, w_u, w_d, g_mpost).reshape(bs, dseq, d)

    meta_lf = jnp.broadcast_to(lfs3[0:1, :, :nh], (bp, n_meta, nh))
    lf_p = jnp.concatenate([meta_lf, lfm3[:, :, :nh]], axis=1)[None]
    k_s = zs3[1:, :, kcol * HEAD_DIM:kcol * HEAD_DIM + hw]
    v_s = zs3[1:, :, vcol * HEAD_DIM:vcol * HEAD_DIM + hw]
    return (y_prompt, y_sample,
            k_cache[None], v_cache[None],
            lf_p,
            jnp.swapaxes(sfin_m_t, -1, -2)[None],
            k_s.reshape(1, bs, dseq, nh, HEAD_DIM),
            v_s.reshape(1, bs, dseq, nh, HEAD_DIM),
            lfs3[1:, :, :nh][None],
            jnp.swapaxes(sfin_s_t[1:], -1, -2)[None])
```

```python
import functools

import jax
import jax.numpy as jnp
from jax import lax
from jax.experimental import pallas as pl
from jax.experimental.pallas import tpu as pltpu

F32 = jnp.float32
BF16 = jnp.bfloat16

EPS = 1e-6
HEAD_DIM = 128
LANES = 128
NEG_BIG = -1e30
LOG2E = 1.4426950408889634
VMEM_LIMIT = 56 * 1024 * 1024


def _pick(n, pref):
    if n <= pref:
        return n
    t = pref
    while n % t:
        t //= 2
    return t


def _params(sem):
    return pltpu.CompilerParams(dimension_semantics=sem, vmem_limit_bytes=VMEM_LIMIT)


def _log_sigmoid(x):
    return jnp.minimum(x, 0.0) - jnp.log1p(jnp.exp(-jnp.abs(x)))


def _rms(x, gain):
    ms = jnp.mean(x * x, axis=-1, keepdims=True)
    return x * lax.rsqrt(ms + EPS) * gain


def _inproj_kernel(x_ref, g_ref, w_ref, wf_ref, fb_ref, z_ref, lf_ref, a_scr):
    @pl.when(pl.program_id(1) == 0)
    def _():
        a = _rms(x_ref[...], g_ref[...]).astype(BF16)
        a_scr[...] = a
        ff = jnp.dot(a, wf_ref[...], preferred_element_type=F32) + fb_ref[...]
        lf_ref[...] = _log_sigmoid(ff)

    z_ref[...] = jnp.dot(a_scr[...], w_ref[...], preferred_element_type=F32)


def _inproj(x, gain, w_main, w_f, f_bias):
    rows, d = x.shape
    n = w_main.shape[1]
    tm = _pick(rows, 1024)
    tn = _pick(n, 1024)
    return pl.pallas_call(
        _inproj_kernel,
        grid=(rows // tm, n // tn),
        in_specs=[
            pl.BlockSpec((tm, d), lambda i, j: (i, 0)),
            pl.BlockSpec((1, d), lambda i, j: (0, 0)),
            pl.BlockSpec((d, tn), lambda i, j: (0, j)),
            pl.BlockSpec((d, LANES), lambda i, j: (0, 0)),
            pl.BlockSpec((1, LANES), lambda i, j: (0, 0)),
        ],
        out_specs=[
            pl.BlockSpec((tm, tn), lambda i, j: (i, j)),
            pl.BlockSpec((tm, LANES), lambda i, j: (i, 0)),
        ],
        out_shape=[
            jax.ShapeDtypeStruct((rows, n), F32),
            jax.ShapeDtypeStruct((rows, LANES), F32),
        ],
        scratch_shapes=[pltpu.VMEM((tm, d), BF16)],
        compiler_params=_params(("parallel", "arbitrary")),
        name="inproj",
    )(x, gain, w_main, w_f, f_bias)


def _inproj_kv_kernel(x_ref, g_ref, w_ref, wf_ref, fb_ref, pk_ref, pv_ref,
                      z_ref, lf_ref, k_hbm, v_hbm, a_scr, stage, sems,
                      *, nz, tiles_per_seq):
    i = pl.program_id(0)
    j = pl.program_id(1)
    tm = x_ref.shape[0]
    plen = pk_ref.shape[0]
    last = pl.num_programs(0) - 1

    nh = k_hbm.shape[2]

    class _TileCopy:
        def __init__(self, slot, dst, tile):
            row = (tile % tiles_per_seq) * tm + plen
            self.copies = [
                pltpu.make_async_copy(stage.at[slot, :, pl.ds(hd * HEAD_DIM, HEAD_DIM)],
                                      dst.at[tile // tiles_per_seq, pl.ds(row, tm), hd, :],
                                      sems.at[slot])
                for hd in range(nh)]

        def start(self):
            for c in self.copies:
                c.start()

        def wait(self):
            for c in self.copies:
                c.wait()

    tile_copy = _TileCopy

    def prefix_copy(slot, src, dst, tile):
        return pltpu.make_async_copy(src, dst.at[tile // tiles_per_seq, pl.ds(0, plen), :, :],
                                     sems.at[2 + slot])

    def opens_sequence(tile):
        return tile % tiles_per_seq == 0

    @pl.when(j == 0)
    def _():
        a = _rms(x_ref[...], g_ref[...]).astype(BF16)
        a_scr[...] = a
        ff = jnp.dot(a, wf_ref[...], preferred_element_type=F32) + fb_ref[...]
        lf_ref[...] = _log_sigmoid(ff)

    tile = jnp.dot(a_scr[...], w_ref[...], preferred_element_type=F32)

    @pl.when(j < nz)
    def _():
        z_ref[...] = tile

    @pl.when(j == nz)
    def _():
        @pl.when(i > 0)
        def _():
            tile_copy(1, v_hbm, i - 1).wait()

            @pl.when(opens_sequence(i - 1))
            def _():
                prefix_copy(1, pv_ref, v_hbm, i - 1).wait()
        stage[0] = tile
        tile_copy(0, k_hbm, i).start()

        @pl.when(opens_sequence(i))
        def _():
            prefix_copy(0, pk_ref, k_hbm, i).start()

    @pl.when(j == nz + 1)
    def _():
        stage[1] = tile
        tile_copy(1, v_hbm, i).start()
        tile_copy(0, k_hbm, i).wait()

        @pl.when(opens_sequence(i))
        def _():
            prefix_copy(1, pv_ref, v_hbm, i).start()
            prefix_copy(0, pk_ref, k_hbm, i).wait()

        @pl.when(i == last)
        def _():
            tile_copy(1, v_hbm, i).wait()

            @pl.when(opens_sequence(i))
            def _():
                prefix_copy(1, pv_ref, v_hbm, i).wait()


def _inproj_kv(x, gain, w_main, w_f, f_bias, prefix_k, prefix_v, *, nseq, nz):
    rows, d = x.shape
    n = w_main.shape[1]
    tn = n // (nz + 2)
    seq = rows // nseq
    plen, nh, _ = prefix_k.shape
    tm = _pick(seq, 1024)
    kern = functools.partial(_inproj_kv_kernel, nz=nz, tiles_per_seq=seq // tm)
    cache = jax.ShapeDtypeStruct((nseq, plen + seq, nh, HEAD_DIM), F32)
    return pl.pallas_call(
        kern,
        grid=(rows // tm, nz + 2),
        in_specs=[
            pl.BlockSpec((tm, d), lambda i, j: (i, 0)),
            pl.BlockSpec((1, d), lambda i, j: (0, 0)),
            pl.BlockSpec((d, tn), lambda i, j: (0, j)),
            pl.BlockSpec((d, LANES), lambda i, j: (0, 0)),
            pl.BlockSpec((1, LANES), lambda i, j: (0, 0)),
            pl.BlockSpec((plen, nh, HEAD_DIM), lambda i, j: (0, 0, 0)),
            pl.BlockSpec((plen, nh, HEAD_DIM), lambda i, j: (0, 0, 0)),
        ],
        out_specs=[
            pl.BlockSpec((tm, tn), lambda i, j: (i, jnp.minimum(j, nz - 1))),
            pl.BlockSpec((tm, LANES), lambda i, j: (i, 0)),
            pl.BlockSpec(memory_space=pl.ANY),
            pl.BlockSpec(memory_space=pl.ANY),
        ],
        out_shape=[
            jax.ShapeDtypeStruct((rows, nz * tn), F32),
            jax.ShapeDtypeStruct((rows, LANES), F32),
            cache, cache,
        ],
        scratch_shapes=[pltpu.VMEM((tm, d), BF16), pltpu.VMEM((2, tm, tn), F32),
                        pltpu.SemaphoreType.DMA((4,))],
        compiler_params=_params(("arbitrary", "arbitrary")),
        name="inproj_kv",
    )(x, gain, w_main, w_f, f_bias, prefix_k, prefix_v)


def _split3(x):
    hi = x.astype(BF16)
    r = x - hi.astype(F32)
    mid = r.astype(BF16)
    lo = (r - mid.astype(F32)).astype(BF16)
    return hi, mid, lo


def _cumsum_kernel(x_ref, init_ref, c_ref, *, chain):
    g, nblk, blk = x_ref.shape
    upper = (lax.broadcasted_iota(jnp.int32, (blk, blk), 0)
             <= lax.broadcasted_iota(jnp.int32, (blk, blk), 1)).astype(BF16)
    if chain:
        before = (lax.broadcasted_iota(jnp.int32, (nblk, nblk), 1)
                  < lax.broadcasted_iota(jnp.int32, (nblk, nblk), 0)).astype(BF16)
    for r in range(g):
        c = sum(jnp.dot(p, upper, preferred_element_type=F32) for p in _split3(x_ref[r]))
        if chain:
            tot = jnp.broadcast_to(c[:, blk - 1:blk], (nblk, blk))
            c = c + sum(jnp.dot(before, p, preferred_element_type=F32) for p in _split3(tot))
        c_ref[r] = c + init_ref[r]


def _cumsum(x, init, *, chain):
    n, nblk, blk = x.shape
    g = _pick(n, 8)
    ni = init.shape[1]
    return pl.pallas_call(
        functools.partial(_cumsum_kernel, chain=chain),
        grid=(n // g,),
        in_specs=[
            pl.BlockSpec((g, nblk, blk), lambda i: (i, 0, 0)),
            pl.BlockSpec((g, ni, blk), lambda i: (i, 0, 0)),
        ],
        out_specs=pl.BlockSpec((g, nblk, blk), lambda i: (i, 0, 0)),
        out_shape=jax.ShapeDtypeStruct((n, nblk, blk), F32),
        compiler_params=_params(("parallel",)),
        name="cumsum",
    )(x, init)


def _cumsum_time(lf, init, nh):
    nb, t = lf.shape[0], lf.shape[1]
    blk = _pick(t, 256)
    x = jnp.transpose(lf[:, :, :nh], (0, 2, 1))
    if t == blk:
        init3 = jnp.broadcast_to(init.reshape(1, nb * nh, 1), (1, nb * nh, blk))
        return _cumsum(x.reshape(1, nb * nh, blk), init3, chain=False).reshape(nb, nh, t)
    init3 = jnp.broadcast_to(init.reshape(nb * nh, 1, 1), (nb * nh, 1, blk))
    return _cumsum(x.reshape(nb * nh, t // blk, blk), init3, chain=True).reshape(nb, nh, t)


def _block_mid(pre, half):
    c = pre.shape[0]
    if half >= 8:
        x3 = pre.reshape(c // (2 * half), 2 * half, HEAD_DIM)
        return jnp.broadcast_to(x3[:, half - 1:half, :], x3.shape).reshape(c, HEAD_DIM)
    x3 = pre.reshape(c // 8, 8, HEAD_DIM)
    sub = lax.broadcasted_iota(jnp.int32, x3.shape, 1)
    mid = None
    for blk in range(8 // (2 * half)):
        src = 2 * half * blk + half - 1
        piece = jnp.broadcast_to(x3[:, src:src + 1, :], x3.shape)
        mid = piece if mid is None else jnp.where(sub >= 2 * half * blk, piece, mid)
    return mid.reshape(c, HEAD_DIM)


def _hgrn_masks(c):
    rows = lax.broadcasted_iota(jnp.int32, (c, HEAD_DIM), 0)
    ri = lax.broadcasted_iota(jnp.int32, (c, c), 0)
    ci = lax.broadcasted_iota(jnp.int32, (c, c), 1)
    levels = c.bit_length() - 1
    upper = [(rows & (1 << (lvl - 1))) != 0 for lvl in range(1, levels + 1)]
    owner = jnp.where(ri >= ci, 32 - lax.clz(ri ^ ci), -1)
    return owner, upper


def _hgrn_chunk(hq, hf, hi, lb, st, masks):
    c = hq.shape[0]
    owner, upper_masks = masks
    sg = jax.nn.sigmoid(hf)
    q = hq * jax.nn.sigmoid(hq)
    g2 = jnp.log(lb + (1.0 - lb) * sg) * LOG2E
    k = (1.0 - lb) * (1.0 - sg)

    scores = jnp.where(owner == 0, jnp.sum(q * k, axis=-1, keepdims=True), 0.0)
    pre = g2
    for lvl, upper in enumerate(upper_masks, start=1):
        mid = _block_mid(pre, 1 << (lvl - 1))
        w = jnp.exp2(jnp.where(upper, pre, mid - pre))
        x = (jnp.where(upper, q, k) * w).astype(BF16)
        s_l = lax.dot_general(x, x, (((1,), (1,)), ((), ())), preferred_element_type=F32)
        scores = jnp.where(owner == lvl, s_l, scores)
        pre = jnp.where(upper, pre + mid, pre)
    cum = pre
    v = hi.astype(BF16)
    qe = (q * jnp.exp2(cum)).astype(BF16)
    o = (lax.dot_general(qe, st.astype(BF16), (((1,), (1,)), ((), ())), preferred_element_type=F32)
         + jnp.dot(scores.astype(BF16), v, preferred_element_type=F32))
    last = cum[c - 1:c, :]
    kd = (k * jnp.exp2(last - cum)).astype(BF16)
    st_new = st * jnp.exp2(last) + lax.dot_general(
        v, kd, (((0,), (0,)), ((), ())), preferred_element_type=F32)
    return o, st_new


def _hgrn_kernel(hq_ref, hf_ref, hi_ref, hg_ref, lbl_ref, gain_ref, s0_ref,
                 o_ref, sfin_ref, st_scr, *, chunk):
    ti = pl.program_id(2)
    hps = st_scr.shape[0]

    @pl.when(ti == 0)
    def _():
        st_scr[...] = s0_ref[0]

    tt = hq_ref.shape[1]
    masks = _hgrn_masks(chunk)
    for hh in range(hps):
        cols = slice(hh * HEAD_DIM, (hh + 1) * HEAD_DIM)
        lg = lbl_ref[hh]
        e = jnp.exp(lg - jnp.max(lg, axis=0, keepdims=True))
        lb = e[0:1, :] / jnp.sum(e, axis=0, keepdims=True)
        gain = gain_ref[hh]
        st = st_scr[hh]
        for cidx in range(tt // chunk):
            sl = slice(cidx * chunk, (cidx + 1) * chunk)
            o, st = _hgrn_chunk(hq_ref[0, sl, cols], hf_ref[0, sl, cols], hi_ref[0, sl, cols],
                                lb, st, masks)
            hg = hg_ref[0, sl, cols]
            o_ref[0, sl, cols] = (_rms(o, gain) * (hg * jax.nn.sigmoid(hg))).astype(o_ref.dtype)
        st_scr[hh] = st

        @pl.when(ti == pl.num_programs(2) - 1)
        def _():
            sfin_ref[0, hh] = st


def _hgrn(z3, lb_logits, gain, s0t, *, s0_shared, chunk, tt, hps):
    nb, t, _ = z3.shape
    nh = gain.shape[0]
    nslot = lb_logits.shape[1]
    ng = nh // hps
    s0_map = (lambda b, h, i: (0, h, 0, 0)) if s0_shared else (lambda b, h, i: (b, h, 0, 0))

    def col(group_off):
        return pl.BlockSpec((1, tt, hps * HEAD_DIM), lambda b, h, i: (b, i, group_off + h))

    return pl.pallas_call(
        functools.partial(_hgrn_kernel, chunk=chunk),
        grid=(nb, ng, t // tt),
        in_specs=[
            col(0), col(ng), col(2 * ng), col(3 * ng),
            pl.BlockSpec((hps, nslot, HEAD_DIM), lambda b, h, i: (h, 0, 0)),
            pl.BlockSpec((hps, 1, HEAD_DIM), lambda b, h, i: (h, 0, 0)),
            pl.BlockSpec((1, hps, HEAD_DIM, HEAD_DIM), s0_map),
        ],
        out_specs=[
            pl.BlockSpec((1, tt, hps * HEAD_DIM), lambda b, h, i: (b, i, h)),
            pl.BlockSpec((1, hps, HEAD_DIM, HEAD_DIM), lambda b, h, i: (b, h, 0, 0)),
        ],
        out_shape=[
            jax.ShapeDtypeStruct((nb, t, nh * HEAD_DIM), BF16),
            jax.ShapeDtypeStruct((nb, nh, HEAD_DIM, HEAD_DIM), F32),
        ],
        scratch_shapes=[pltpu.VMEM((hps, HEAD_DIM, HEAD_DIM), F32)],
        compiler_params=_params(("parallel", "parallel", "arbitrary")),
        name="hgrn",
    )(z3, z3, z3, z3, lb_logits, gain, s0t)


def _slice_prefetch(k_hbm, v_hbm, kbuf, vbuf, sems, lin, nslices, nheads):
    def copies(idx):
        seq_idx, head = idx // nheads, idx % nheads
        return (pltpu.make_async_copy(k_hbm.at[seq_idx, :, head, :], kbuf.at[idx % 2],
                                      sems.at[0, idx % 2]),
                pltpu.make_async_copy(v_hbm.at[seq_idx, :, head, :], vbuf.at[idx % 2],
                                      sems.at[1, idx % 2]))

    @pl.when(lin == 0)
    def _():
        for c in copies(lin):
            c.start()
    for c in copies(lin):
        c.wait()

    @pl.when(lin + 1 < nslices)
    def _():
        for c in copies(lin + 1):
            c.start()


def _fox_kernel(q_ref, cq_ref, kp_hbm, vp_hbm, ckp_ref, ks_ref, vs_ref, cks_ref, o_ref,
                kbuf, vbuf, sems, *, pchunk, scale):
    h = pl.program_id(1)
    nheads = pl.num_programs(1)
    lin = pl.program_id(0) * nheads + h
    slot = lin % 2
    _slice_prefetch(kp_hbm, vp_hbm, kbuf, vbuf, sems, lin, pl.num_programs(0) * nheads, nheads)
    tq = q_ref.shape[1]
    q = q_ref[0].astype(BF16)
    lane = lax.broadcasted_iota(jnp.int32, (tq, LANES), 1)
    cq = jnp.sum(jnp.where(lane == h, cq_ref[0], 0.0), axis=-1, keepdims=True)

    def logits(kb, ck):
        return lax.dot_general(q, kb.astype(BF16), (((1,), (1,)), ((), ())),
                               preferred_element_type=F32) * scale + (cq - ck)

    plen = kbuf.shape[1]
    chunks = [slice(pc * pchunk, (pc + 1) * pchunk) for pc in range(plen // pchunk)]
    causal = (lax.broadcasted_iota(jnp.int32, (tq, tq), 1)
              <= lax.broadcasted_iota(jnp.int32, (tq, tq), 0))
    s_list = [logits(kbuf[slot, sl, :], ckp_ref[0, 0, :, sl]) for sl in chunks]
    s_list.append(jnp.where(causal, logits(ks_ref[0], cks_ref[0, 0]), -jnp.inf))
    v_list = [vbuf[slot, sl, :] for sl in chunks] + [vs_ref[0]]
    m = functools.reduce(jnp.maximum, [jnp.max(s, axis=-1, keepdims=True) for s in s_list])
    l = jnp.zeros((tq, 1), F32)
    acc = jnp.zeros((tq, HEAD_DIM), F32)
    for s, vb in zip(s_list, v_list):
        p = jnp.exp(s - m)
        l = l + jnp.sum(p, axis=-1, keepdims=True)
        acc = acc + jnp.dot(p.astype(BF16), vb.astype(BF16), preferred_element_type=F32)
    o_ref[0] = (acc / l).astype(o_ref.dtype)


def _fox(z3, b_off, qcol, kcol, vcol, cq, kp, vp, ckp, cks):
    nb, t, _ = cq.shape
    nh = cks.shape[1]
    plen = kp.shape[1]
    kern = functools.partial(_fox_kernel, pchunk=_pick(plen, 512), scale=HEAD_DIM ** -0.5)
    return pl.pallas_call(
        kern,
        grid=(nb, nh),
        in_specs=[
            pl.BlockSpec((1, t, HEAD_DIM), lambda b, h: (b + b_off, 0, qcol + h)),
            pl.BlockSpec((1, t, LANES), lambda b, h: (b, 0, 0)),
            pl.BlockSpec(memory_space=pl.ANY),
            pl.BlockSpec(memory_space=pl.ANY),
            pl.BlockSpec((1, 1, 1, plen), lambda b, h: (b, h, 0, 0)),
            pl.BlockSpec((1, t, HEAD_DIM), lambda b, h: (b + b_off, 0, kcol + h)),
            pl.BlockSpec((1, t, HEAD_DIM), lambda b, h: (b + b_off, 0, vcol + h)),
            pl.BlockSpec((1, 1, 1, t), lambda b, h: (b, h, 0, 0)),
        ],
        out_specs=pl.BlockSpec((1, t, HEAD_DIM), lambda b, h: (b, 0, h)),
        out_shape=jax.ShapeDtypeStruct((nb, t, nh * HEAD_DIM), BF16),
        scratch_shapes=[pltpu.VMEM((2, plen, HEAD_DIM), F32), pltpu.VMEM((2, plen, HEAD_DIM), F32),
                        pltpu.SemaphoreType.DMA((2, 2))],
        compiler_params=_params(("arbitrary", "arbitrary")),
        name="fox",
    )(z3, cq, kp, vp, ckp, z3, z3, cks)


def _bias_pieces(c):
    hi = c.astype(BF16).astype(F32)
    r = c - hi
    mid = r.astype(BF16).astype(F32)
    return hi, mid, r - mid


def _bias_rows(c, ones_first):
    hi, mid, lo = _bias_pieces(c)
    sub = lax.broadcasted_iota(jnp.int32, (HEAD_DIM, c.shape[1]), 0)
    third = jnp.where(sub >= 3, sub - 3, sub)
    pieces = jnp.where(third == 0, hi, jnp.where(third == 1, mid, lo))
    first, second = (1.0, pieces) if ones_first else (pieces, 1.0)
    return jnp.where(sub < 3, first, jnp.where(sub < 6, second, 0.0))


def _bias_cols(c):
    hi, mid, lo = _bias_pieces(c)
    lane = lax.broadcasted_iota(jnp.int32, (c.shape[0], LANES), 1)
    pieces = jnp.where(lane == 3, hi, jnp.where(lane == 4, mid, lo))
    return jnp.where(lane < 3, 1.0, jnp.where(lane < 6, pieces, 0.0))


def _fox_prompt_kernel(q_ref, cq_ref, k_hbm, v_hbm, ck_ref, ckp_ref, o_ref,
                       kaug_scr, vt_scr, sa_scr, sb_scr, acc_scr, kbuf, vbuf, sems, *, scale):
    h = pl.program_id(1)
    qi = pl.program_id(2)
    nheads = pl.num_programs(1)
    nslices = pl.num_programs(0) * nheads
    lin = pl.program_id(0) * nheads + h
    slot = lin % 2
    tq = q_ref.shape[1]
    tk = sa_scr.shape[0]
    plen = ckp_ref.shape[1]
    t = kbuf.shape[1] - plen

    @pl.when(qi == 0)
    def _():
        _slice_prefetch(k_hbm, v_hbm, kbuf, vbuf, sems, lin, nslices, nheads)

        def prep(cix, carry):
            r0 = pl.multiple_of(cix * tk, tk)
            src = pl.ds(pl.multiple_of(plen + r0, 8), tk)
            ck = ck_ref[0, 0, :, pl.ds(r0, tk)] * LOG2E
            kaug_scr[pl.ds(r0, tk), HEAD_DIM:] = _bias_rows(-ck, True).T.astype(BF16)
            kaug_scr[pl.ds(r0, tk), :HEAD_DIM] = kbuf[slot, src, :].astype(BF16)
            vt_scr[:, pl.ds(r0, tk)] = vbuf[slot, src, :].T.astype(BF16)
            return carry
        lax.fori_loop(0, t // tk, prep, 0)

    qt = (q_ref[0] * (scale * LOG2E)).T.astype(BF16)
    augq = _bias_rows(cq_ref[0, 0] * LOG2E, False).astype(BF16)
    qaug = jnp.concatenate([qt, augq], axis=0)

    def row0(blk):
        return pl.multiple_of(blk * tk, tk)

    def colmax(s):
        return jnp.max(s, axis=0, keepdims=True)

    def stage(blk, s_scr):
        s = jnp.dot(kaug_scr[pl.ds(row0(blk), tk), :], qaug, preferred_element_type=F32)
        s_scr[...] = s
        return colmax(s)

    def pv_block(blk):
        return lambda p: jnp.dot(vt_scr[:, pl.ds(row0(blk), tk)], p, preferred_element_type=F32)

    def absorb(ml, blocks):
        m, l = ml
        m_new = m
        for _, smax, _ in blocks:
            m_new = jnp.maximum(m_new, smax)
        alpha = jnp.exp2(m - m_new)
        l = alpha * l
        pv = None
        for s, _, pv_fn in blocks:
            p = jnp.exp2(s - m_new)
            l = l + jnp.sum(p, axis=0, keepdims=True)
            term = pv_fn(p.astype(BF16))
            pv = term if pv is None else pv + term
        acc_scr[...] = alpha * acc_scr[...] + pv
        return m_new, l

    acc_scr[...] = jnp.zeros_like(acc_scr)
    ml = (jnp.full((1, tq), NEG_BIG, F32), jnp.zeros((1, tq), F32))

    def body(pair, state):
        ma, ml = state
        mb = stage(2 * pair + 1, sb_scr)
        ml = absorb(ml, [(sa_scr[...], ma, pv_block(2 * pair))])
        ma = stage(2 * pair + 2, sa_scr)
        ml = absorb(ml, [(sb_scr[...], mb, pv_block(2 * pair + 1))])
        return ma, ml

    _, ml = lax.fori_loop(0, qi, body, (stage(0, sa_scr), ml))

    stage(2 * qi + 1, sb_scr)
    krow = lax.broadcasted_iota(jnp.int32, (tk, tq), 0)
    qcolumn = lax.broadcasted_iota(jnp.int32, (tk, tq), 1)
    s_a = jnp.where(krow <= qcolumn, sa_scr[...], -jnp.inf)
    s_b = jnp.where(krow + tk <= qcolumn, sb_scr[...], -jnp.inf)
    lane = lax.broadcasted_iota(jnp.int32, ckp_ref.shape[1:], 1)
    ckp = jnp.sum(jnp.where(lane == h, ckp_ref[0], 0.0), axis=-1, keepdims=True) * LOG2E
    kp = jnp.concatenate([kbuf[slot, :plen, :].astype(BF16), _bias_cols(-ckp).astype(BF16)], axis=1)
    vp = vbuf[slot, :plen, :].astype(BF16)
    s_p = jnp.dot(kp, qaug, preferred_element_type=F32)
    m, l = absorb(ml, [
        (s_p, colmax(s_p), lambda p: lax.dot_general(vp, p, (((0,), (0,)), ((), ())),
                                                     preferred_element_type=F32)),
        (s_a, colmax(s_a), pv_block(2 * qi)),
        (s_b, colmax(s_b), pv_block(2 * qi + 1)),
    ])
    o_ref[0] = (acc_scr[...] * (1.0 / l)).T.astype(o_ref.dtype)


def _fox_prompt(z3, qcol, k_cache, v_cache, c_row, cp_col, *, tq):
    nb, t, _ = z3.shape
    nh = c_row.shape[1]
    plen = cp_col.shape[1]
    kern = functools.partial(_fox_prompt_kernel, scale=HEAD_DIM ** -0.5)
    return pl.pallas_call(
        kern,
        grid=(nb, nh, t // tq),
        in_specs=[
            pl.BlockSpec((1, tq, HEAD_DIM), lambda b, h, i: (b, i, qcol + h)),
            pl.BlockSpec((1, 1, 1, tq), lambda b, h, i: (b, h, 0, i)),
            pl.BlockSpec(memory_space=pl.ANY),
            pl.BlockSpec(memory_space=pl.ANY),
            pl.BlockSpec((1, 1, 1, t), lambda b, h, i: (b, h, 0, 0)),
            pl.BlockSpec((1, plen, LANES), lambda b, h, i: (0, 0, 0)),
        ],
        out_specs=pl.BlockSpec((1, tq, HEAD_DIM), lambda b, h, i: (b, i, h)),
        out_shape=jax.ShapeDtypeStruct((nb, t, nh * HEAD_DIM), BF16),
        scratch_shapes=[pltpu.VMEM((t, 2 * HEAD_DIM), BF16), pltpu.VMEM((HEAD_DIM, t), BF16),
                        pltpu.VMEM((tq // 2, tq), F32), pltpu.VMEM((tq // 2, tq), F32),
                        pltpu.VMEM((HEAD_DIM, tq), F32),
                        pltpu.VMEM((2, plen + t, HEAD_DIM), F32),
                        pltpu.VMEM((2, plen + t, HEAD_DIM), F32),
                        pltpu.SemaphoreType.DMA((2, 2))],
        compiler_params=_params(("arbitrary", "arbitrary", "arbitrary")),
        name="fox_prompt",
    )(z3, c_row, k_cache, v_cache, c_row, cp_col)


def _outproj_kernel(h_ref, a_ref, b_ref, wa_ref, wb_ref, g_ref, o_ref):
    mix = (jnp.dot(a_ref[...], wa_ref[...], preferred_element_type=F32)
           + jnp.dot(b_ref[...], wb_ref[...], preferred_element_type=F32))
    o_ref[...] = h_ref[...] + _rms(mix, g_ref[...])


def _outproj(h, mix_a, mix_b, w_a, w_b, gain):
    rows, d = h.shape
    ka, kb = mix_a.shape[1], mix_b.shape[1]
    tm = _pick(rows, 512)
    return pl.pallas_call(
        _outproj_kernel,
        grid=(rows // tm,),
        in_specs=[
            pl.BlockSpec((tm, d), lambda i: (i, 0)),
            pl.BlockSpec((tm, ka), lambda i: (i, 0)),
            pl.BlockSpec((tm, kb), lambda i: (i, 0)),
            pl.BlockSpec((ka, d), lambda i: (0, 0)),
            pl.BlockSpec((kb, d), lambda i: (0, 0)),
            pl.BlockSpec((1, d), lambda i: (0, 0)),
        ],
        out_specs=pl.BlockSpec((tm, d), lambda i: (i, 0)),
        out_shape=jax.ShapeDtypeStruct((rows, d), F32),
        compiler_params=_params(("parallel",)),
        name="outproj",
    )(h, mix_a, mix_b, w_a, w_b, gain)


def _mlp_kernel(h_ref, gpre_ref, wu_ref, wd_ref, gpost_ref, o_ref, a_scr, acc_scr):
    j = pl.program_id(1)

    @pl.when(j == 0)
    def _():
        a_scr[...] = _rms(h_ref[...], gpre_ref[...]).astype(BF16)
        acc_scr[...] = jnp.zeros_like(acc_scr)

    u = jnp.maximum(jnp.dot(a_scr[...], wu_ref[...], preferred_element_type=F32), 0.0)
    acc_scr[...] += jnp.dot((u * u).astype(BF16), wd_ref[...], preferred_element_type=F32)

    @pl.when(j == pl.num_programs(1) - 1)
    def _():
        o_ref[...] = h_ref[...] + _rms(acc_scr[...], gpost_ref[...])


def _mlp(h, g_pre, w_up, w_down, g_post):
    rows, d = h.shape
    dff = w_up.shape[1]
    tm = _pick(rows, 512)
    tf = _pick(dff, 1024)
    return pl.pallas_call(
        _mlp_kernel,
        grid=(rows // tm, dff // tf),
        in_specs=[
            pl.BlockSpec((tm, d), lambda i, j: (i, 0)),
            pl.BlockSpec((1, d), lambda i, j: (0, 0)),
            pl.BlockSpec((d, tf), lambda i, j: (0, j)),
            pl.BlockSpec((tf, d), lambda i, j: (j, 0)),
            pl.BlockSpec((1, d), lambda i, j: (0, 0)),
        ],
        out_specs=pl.BlockSpec((tm, d), lambda i, j: (i, 0)),
        out_shape=jax.ShapeDtypeStruct((rows, d), F32),
        scratch_shapes=[pltpu.VMEM((tm, d), BF16), pltpu.VMEM((tm, d), F32)],
        compiler_params=_params(("parallel", "arbitrary")),
        name="mlp",
    )(h, g_pre, w_up, w_down, g_post)


def _col_form(c_row):
    c_col = jnp.transpose(c_row, (0, 2, 1))
    return jnp.pad(c_col, ((0, 0), (0, 0), (0, LANES - c_col.shape[2])))


def kernel(x_prompt, x_sample, cache_fox_k, cache_fox_v, cache_fox_logf, state_hgrn, meta_tokens,
           g_mix_pre, w_in, hg_lb_logits, hg_norm_gain, fox_f_bias, w_out, g_mix_post, g_mlp_pre,
           w_up, w_down, g_mlp_post):
    bp, seq, d = x_prompt.shape
    bs, dseq, _ = x_sample.shape
    n_meta = meta_tokens.shape[0]
    past = cache_fox_k.shape[2]
    nh = cache_fox_k.shape[3]
    hw = nh * HEAD_DIM
    n_main = w_in.shape[2] - nh
    assert dseq == n_meta, "sample frames and meta tokens share the small-stream kernels"
    assert state_hgrn.shape[2] == nh and n_main == 7 * hw

    w_main = w_in[0, :, :n_main].astype(BF16)
    w_f = jnp.pad(w_in[0, :, n_main:], ((0, 0), (0, LANES - nh))).astype(BF16)
    f_bias = jnp.pad(fox_f_bias[0], (0, LANES - nh))[None, :]
    w_oa = w_out[0, :hw].astype(BF16)
    w_ob = w_out[0, hw:].astype(BF16)
    w_u = w_up[0].astype(BF16)
    w_d = w_down[0].astype(BF16)
    g_pre = g_mix_pre[0][None, :]
    g_post = g_mix_post[0][None, :]
    g_mpre = g_mlp_pre[0][None, :]
    g_mpost = g_mlp_post[0][None, :]
    lb_logits = jnp.transpose(hg_lb_logits.reshape(-1, nh, HEAD_DIM), (1, 0, 2))
    hg_gain = hg_norm_gain[0].reshape(nh, 1, HEAD_DIM)
    qcol, kcol, vcol = 4 * nh, 5 * nh, 6 * nh

    ns = bs + 1
    xs = jnp.concatenate([meta_tokens, x_sample.reshape(bs * dseq, d)], axis=0)
    zs, lfs = _inproj(xs, g_pre, w_main, w_f, f_bias)
    zs3 = zs.reshape(ns, dseq, n_main)
    lfs3 = lfs.reshape(ns, dseq, LANES)

    c_past = _cumsum_time(cache_fox_logf[0], jnp.zeros((bs, nh), F32), nh)
    init_s = jnp.concatenate([jnp.zeros((1, nh), F32), c_past[:, :, past - 1]], axis=0)
    c_small = _cumsum_time(lfs3, init_s, nh)

    s0_small = jnp.concatenate([jnp.zeros((1,) + state_hgrn.shape[2:], F32), state_hgrn[0]], axis=0)
    s0_small_t = jnp.swapaxes(s0_small, -1, -2)
    mix_hg_s, sfin_s_t = _hgrn(zs3, lb_logits, hg_gain, s0_small_t, s0_shared=False,
                               chunk=dseq, tt=dseq, hps=nh)

    mix_fox_s = _fox(zs3, 1, qcol, kcol, vcol, _col_form(c_small[1:]), cache_fox_k[0], cache_fox_v[0],
                     c_past[:, :, None, :], c_small[1:, :, None, :])

    xm = x_prompt.reshape(bp * seq, d)
    meta_k = zs3[0, :, kcol * HEAD_DIM:kcol * HEAD_DIM + hw].reshape(n_meta, nh, HEAD_DIM)
    meta_v = zs3[0, :, vcol * HEAD_DIM:vcol * HEAD_DIM + hw].reshape(n_meta, nh, HEAD_DIM)
    zm, lfm, k_cache, v_cache = _inproj_kv(xm, g_pre, w_main, w_f, f_bias, meta_k, meta_v,
                                           nseq=bp, nz=kcol // nh)
    zm3 = zm.reshape(bp, seq, kcol * HEAD_DIM)
    lfm3 = lfm.reshape(bp, seq, LANES)
    c_meta = c_small[0:1]
    init_m = jnp.broadcast_to(c_meta[:, :, n_meta - 1], (bp, nh))
    c_main = _cumsum_time(lfm3, init_m, nh)

    mix_hg_m, sfin_m_t = _hgrn(zm3, lb_logits, hg_gain, sfin_s_t[0:1], s0_shared=True,
                               chunk=_pick(seq, 128), tt=_pick(seq, 512), hps=1)
    mix_fox_m = _fox_prompt(zm3, qcol, k_cache, v_cache, c_main[:, :, None, :], _col_form(c_meta),
                            tq=_pick(seq, 1024))

    h1m = _outproj(xm, mix_hg_m.reshape(bp * seq, hw), mix_fox_m.reshape(bp * seq, hw),
                   w_oa, w_ob, g_post)
    y_prompt = _mlp(h1m, g_mpre, w_u, w_d, g_mpost).reshape(bp, seq, d)
    h1s = _outproj(x_sample.reshape(bs * dseq, d), mix_hg_s[1:].reshape(bs * dseq, hw),
                   mix_fox_s.reshape(bs * dseq, hw), w_oa, w_ob, g_post)
    y_sample = _mlp(h1s, g_mpre, w_u, w_d, g_mpost).reshape(bs, dseq, d)

    meta_lf = jnp.broadcast_to(lfs3[0:1, :, :nh], (bp, n_meta, nh))
    lf_p = jnp.concatenate([meta_lf, lfm3[:, :, :nh]], axis=1)[None]
    k_s = zs3[1:, :, kcol * HEAD_DIM:kcol * HEAD_DIM + hw]
    v_s = zs3[1:, :, vcol * HEAD_DIM:vcol * HEAD_DIM + hw]
    return (y_prompt, y_sample,
            k_cache[None], v_cache[None],
            lf_p,
            jnp.swapaxes(sfin_m_t, -1, -2)[None],
            k_s.reshape(1, bs, dseq, nh, HEAD_DIM),
            v_s.reshape(1, bs, dseq, nh, HEAD_DIM),
            lfs3[1:, :, :nh][None],
            jnp.swapaxes(sfin_s_t[1:], -1, -2)[None])
```

```python
import functools

import jax
import jax.numpy as jnp
from jax import lax
from jax.experimental import pallas as pl
from jax.experimental.pallas import tpu as pltpu

F32 = jnp.float32
BF16 = jnp.bfloat16

EPS = 1e-6
HEAD_DIM = 128
LANES = 128
NEG_BIG = -1e30
LOG2E = 1.4426950408889634
VMEM_LIMIT = 56 * 1024 * 1024


def _pick(n, pref):
    if n <= pref:
        return n
    t = pref
    while n % t:
        t //= 2
    return t


def _params(sem):
    return pltpu.CompilerParams(dimension_semantics=sem, vmem_limit_bytes=VMEM_LIMIT)


def _log_sigmoid(x):
    return jnp.minimum(x, 0.0) - jnp.log1p(jnp.exp(-jnp.abs(x)))


def _rms(x, gain):
    ms = jnp.mean(x * x, axis=-1, keepdims=True)
    return x * lax.rsqrt(ms + EPS) * gain


def _inproj_kernel(x_ref, g_ref, w_ref, wf_ref, fb_ref, z_ref, lf_ref, a_scr):
    @pl.when(pl.program_id(1) == 0)
    def _():
        a = _rms(x_ref[...], g_ref[...]).astype(BF16)
        a_scr[...] = a
        ff = jnp.dot(a, wf_ref[...], preferred_element_type=F32) + fb_ref[...]
        lf_ref[...] = _log_sigmoid(ff)

    z_ref[...] = jnp.dot(a_scr[...], w_ref[...], preferred_element_type=F32)


def _inproj(x, gain, w_main, w_f, f_bias):
    rows, d = x.shape
    n = w_main.shape[1]
    tm = _pick(rows, 1024)
    tn = _pick(n, 1024)
    return pl.pallas_call(
        _inproj_kernel,
        grid=(rows // tm, n // tn),
        in_specs=[
            pl.BlockSpec((tm, d), lambda i, j: (i, 0)),
            pl.BlockSpec((1, d), lambda i, j: (0, 0)),
            pl.BlockSpec((d, tn), lambda i, j: (0, j)),
            pl.BlockSpec((d, LANES), lambda i, j: (0, 0)),
            pl.BlockSpec((1, LANES), lambda i, j: (0, 0)),
        ],
        out_specs=[
            pl.BlockSpec((tm, tn), lambda i, j: (i, j)),
            pl.BlockSpec((tm, LANES), lambda i, j: (i, 0)),
        ],
        out_shape=[
            jax.ShapeDtypeStruct((rows, n), F32),
            jax.ShapeDtypeStruct((rows, LANES), F32),
        ],
        scratch_shapes=[pltpu.VMEM((tm, d), BF16)],
        compiler_params=_params(("parallel", "arbitrary")),
        name="inproj",
    )(x, gain, w_main, w_f, f_bias)


def _inproj_kv_kernel(x_ref, g_ref, w_ref, wf_ref, fb_ref, pk_ref, pv_ref,
                      z_ref, lf_ref, k_hbm, v_hbm, a_scr, stage, sems,
                      *, nz, tiles_per_seq):
    i = pl.program_id(0)
    j = pl.program_id(1)
    tm = x_ref.shape[0]
    plen = pk_ref.shape[0]
    last = pl.num_programs(0) - 1

    nh = k_hbm.shape[2]

    class _TileCopy:
        def __init__(self, slot, dst, tile):
            row = (tile % tiles_per_seq) * tm + plen
            self.copies = [
                pltpu.make_async_copy(stage.at[slot, :, pl.ds(hd * HEAD_DIM, HEAD_DIM)],
                                      dst.at[tile // tiles_per_seq, pl.ds(row, tm), hd, :],
                                      sems.at[slot])
                for hd in range(nh)]

        def start(self):
            for c in self.copies:
                c.start()

        def wait(self):
            for c in self.copies:
                c.wait()

    tile_copy = _TileCopy

    def prefix_copy(slot, src, dst, tile):
        return pltpu.make_async_copy(src, dst.at[tile // tiles_per_seq, pl.ds(0, plen), :, :],
                                     sems.at[2 + slot])

    def opens_sequence(tile):
        return tile % tiles_per_seq == 0

    @pl.when(j == 0)
    def _():
        a = _rms(x_ref[...], g_ref[...]).astype(BF16)
        a_scr[...] = a
        ff = jnp.dot(a, wf_ref[...], preferred_element_type=F32) + fb_ref[...]
        lf_ref[...] = _log_sigmoid(ff)

    def project():
        return jnp.dot(a_scr[...], w_ref[...], preferred_element_type=F32)

    @pl.when(j < nz)
    def _():
        z_ref[...] = project()

    @pl.when(j == nz)
    def _():
        @pl.when(i > 0)
        def _():
            tile_copy(1, v_hbm, i - 1).wait()

            @pl.when(opens_sequence(i - 1))
            def _():
                prefix_copy(1, pv_ref, v_hbm, i - 1).wait()
        stage[0] = project()
        tile_copy(0, k_hbm, i).start()

        @pl.when(opens_sequence(i))
        def _():
            prefix_copy(0, pk_ref, k_hbm, i).start()

    @pl.when(j == nz + 1)
    def _():
        stage[1] = project()
        tile_copy(1, v_hbm, i).start()
        tile_copy(0, k_hbm, i).wait()

        @pl.when(opens_sequence(i))
        def _():
            prefix_copy(1, pv_ref, v_hbm, i).start()
            prefix_copy(0, pk_ref, k_hbm, i).wait()

        @pl.when(i == last)
        def _():
            tile_copy(1, v_hbm, i).wait()

            @pl.when(opens_sequence(i))
            def _():
                prefix_copy(1, pv_ref, v_hbm, i).wait()


def _inproj_kv(x, gain, w_main, w_f, f_bias, prefix_k, prefix_v, *, nseq, nz):
    rows, d = x.shape
    n = w_main.shape[1]
    tn = n // (nz + 2)
    seq = rows // nseq
    plen, nh, _ = prefix_k.shape
    tm = _pick(seq, 1024)
    kern = functools.partial(_inproj_kv_kernel, nz=nz, tiles_per_seq=seq // tm)
    cache = jax.ShapeDtypeStruct((nseq, plen + seq, nh, HEAD_DIM), F32)
    return pl.pallas_call(
        kern,
        grid=(rows // tm, nz + 2),
        in_specs=[
            pl.BlockSpec((tm, d), lambda i, j: (i, 0)),
            pl.BlockSpec((1, d), lambda i, j: (0, 0)),
            pl.BlockSpec((d, tn), lambda i, j: (0, j)),
            pl.BlockSpec((d, LANES), lambda i, j: (0, 0)),
            pl.BlockSpec((1, LANES), lambda i, j: (0, 0)),
            pl.BlockSpec((plen, nh, HEAD_DIM), lambda i, j: (0, 0, 0)),
            pl.BlockSpec((plen, nh, HEAD_DIM), lambda i, j: (0, 0, 0)),
        ],
        out_specs=[
            pl.BlockSpec((tm, tn), lambda i, j: (i, jnp.minimum(j, nz - 1))),
            pl.BlockSpec((tm, LANES), lambda i, j: (i, 0)),
            pl.BlockSpec(memory_space=pl.ANY),
            pl.BlockSpec(memory_space=pl.ANY),
        ],
        out_shape=[
            jax.ShapeDtypeStruct((rows, nz * tn), F32),
            jax.ShapeDtypeStruct((rows, LANES), F32),
            cache, cache,
        ],
        scratch_shapes=[pltpu.VMEM((tm, d), BF16), pltpu.VMEM((2, tm, tn), F32),
                        pltpu.SemaphoreType.DMA((4,))],
        compiler_params=_params(("arbitrary", "arbitrary")),
        name="inproj_kv",
    )(x, gain, w_main, w_f, f_bias, prefix_k, prefix_v)


def _split3(x):
    hi = x.astype(BF16)
    r = x - hi.astype(F32)
    mid = r.astype(BF16)
    lo = (r - mid.astype(F32)).astype(BF16)
    return hi, mid, lo


def _cumsum_kernel(x_ref, init_ref, c_ref, *, chain):
    g, nblk, blk = x_ref.shape
    upper = (lax.broadcasted_iota(jnp.int32, (blk, blk), 0)
             <= lax.broadcasted_iota(jnp.int32, (blk, blk), 1)).astype(BF16)
    if chain:
        before = (lax.broadcasted_iota(jnp.int32, (nblk, nblk), 1)
                  < lax.broadcasted_iota(jnp.int32, (nblk, nblk), 0)).astype(BF16)
    for r in range(g):
        c = sum(jnp.dot(p, upper, preferred_element_type=F32) for p in _split3(x_ref[r]))
        if chain:
            tot = jnp.broadcast_to(c[:, blk - 1:blk], (nblk, blk))
            c = c + sum(jnp.dot(before, p, preferred_element_type=F32) for p in _split3(tot))
        c_ref[r] = c + init_ref[r]


def _cumsum(x, init, *, chain):
    n, nblk, blk = x.shape
    g = _pick(n, 8)
    ni = init.shape[1]
    return pl.pallas_call(
        functools.partial(_cumsum_kernel, chain=chain),
        grid=(n // g,),
        in_specs=[
            pl.BlockSpec((g, nblk, blk), lambda i: (i, 0, 0)),
            pl.BlockSpec((g, ni, blk), lambda i: (i, 0, 0)),
        ],
        out_specs=pl.BlockSpec((g, nblk, blk), lambda i: (i, 0, 0)),
        out_shape=jax.ShapeDtypeStruct((n, nblk, blk), F32),
        compiler_params=_params(("parallel",)),
        name="cumsum",
    )(x, init)


def _cumsum_time(lf, init, nh):
    nb, t = lf.shape[0], lf.shape[1]
    blk = _pick(t, 256)
    x = jnp.transpose(lf[:, :, :nh], (0, 2, 1))
    if t == blk:
        init3 = jnp.broadcast_to(init.reshape(1, nb * nh, 1), (1, nb * nh, blk))
        return _cumsum(x.reshape(1, nb * nh, blk), init3, chain=False).reshape(nb, nh, t)
    init3 = jnp.broadcast_to(init.reshape(nb * nh, 1, 1), (nb * nh, 1, blk))
    return _cumsum(x.reshape(nb * nh, t // blk, blk), init3, chain=True).reshape(nb, nh, t)


def _block_mid(pre, half):
    c = pre.shape[0]
    if half >= 8:
        x3 = pre.reshape(c // (2 * half), 2 * half, HEAD_DIM)
        return jnp.broadcast_to(x3[:, half - 1:half, :], x3.shape).reshape(c, HEAD_DIM)
    x3 = pre.reshape(c // 8, 8, HEAD_DIM)
    sub = lax.broadcasted_iota(jnp.int32, x3.shape, 1)
    mid = None
    for blk in range(8 // (2 * half)):
        src = 2 * half * blk + half - 1
        piece = jnp.broadcast_to(x3[:, src:src + 1, :], x3.shape)
        mid = piece if mid is None else jnp.where(sub >= 2 * half * blk, piece, mid)
    return mid.reshape(c, HEAD_DIM)


def _hgrn_masks(c):
    rows = lax.broadcasted_iota(jnp.int32, (c, HEAD_DIM), 0)
    ri = lax.broadcasted_iota(jnp.int32, (c, c), 0)
    ci = lax.broadcasted_iota(jnp.int32, (c, c), 1)
    levels = c.bit_length() - 1
    upper = [(rows & (1 << (lvl - 1))) != 0 for lvl in range(1, levels + 1)]
    owner = jnp.where(ri >= ci, 32 - lax.clz(ri ^ ci), -1)
    return owner, upper


def _hgrn_chunk(hq, hf, hi, lb, st, masks):
    c = hq.shape[0]
    owner, upper_masks = masks
    sg = jax.nn.sigmoid(hf)
    q = hq * jax.nn.sigmoid(hq)
    g2 = jnp.log(lb + (1.0 - lb) * sg) * LOG2E
    k = (1.0 - lb) * (1.0 - sg)

    scores = jnp.where(owner == 0, jnp.sum(q * k, axis=-1, keepdims=True), 0.0)
    pre = g2
    for lvl, upper in enumerate(upper_masks, start=1):
        mid = _block_mid(pre, 1 << (lvl - 1))
        w = jnp.exp2(jnp.where(upper, pre, mid - pre))
        x = (jnp.where(upper, q, k) * w).astype(BF16)
        s_l = lax.dot_general(x, x, (((1,), (1,)), ((), ())), preferred_element_type=F32)
        scores = jnp.where(owner == lvl, s_l, scores)
        pre = jnp.where(upper, pre + mid, pre)
    cum = pre
    v = hi.astype(BF16)
    qe = (q * jnp.exp2(cum)).astype(BF16)
    o = (lax.dot_general(qe, st.astype(BF16), (((1,), (1,)), ((), ())), preferred_element_type=F32)
         + jnp.dot(scores.astype(BF16), v, preferred_element_type=F32))
    last = cum[c - 1:c, :]
    kd = (k * jnp.exp2(last - cum)).astype(BF16)
    st_new = st * jnp.exp2(last) + lax.dot_general(
        v, kd, (((0,), (0,)), ((), ())), preferred_element_type=F32)
    return o, st_new


def _hgrn_kernel(hq_ref, hf_ref, hi_ref, hg_ref, lbl_ref, gain_ref, s0_ref,
                 o_ref, sfin_ref, st_scr, *, chunk):
    ti = pl.program_id(2)
    hps = st_scr.shape[0]

    @pl.when(ti == 0)
    def _():
        st_scr[...] = s0_ref[0]

    tt = hq_ref.shape[1]
    masks = _hgrn_masks(chunk)
    for hh in range(hps):
        cols = slice(hh * HEAD_DIM, (hh + 1) * HEAD_DIM)
        lg = lbl_ref[hh]
        e = jnp.exp(lg - jnp.max(lg, axis=0, keepdims=True))
        lb = e[0:1, :] / jnp.sum(e, axis=0, keepdims=True)
        gain = gain_ref[hh]
        st = st_scr[hh]
        for cidx in range(tt // chunk):
            sl = slice(cidx * chunk, (cidx + 1) * chunk)
            o, st = _hgrn_chunk(hq_ref[0, sl, cols], hf_ref[0, sl, cols], hi_ref[0, sl, cols],
                                lb, st, masks)
            hg = hg_ref[0, sl, cols]
            o_ref[0, sl, cols] = (_rms(o, gain) * (hg * jax.nn.sigmoid(hg))).astype(o_ref.dtype)
        st_scr[hh] = st

        @pl.when(ti == pl.num_programs(2) - 1)
        def _():
            sfin_ref[0, hh] = st


def _hgrn(z3, lb_logits, gain, s0t, *, s0_shared, chunk, tt, hps):
    nb, t, _ = z3.shape
    nh = gain.shape[0]
    nslot = lb_logits.shape[1]
    ng = nh // hps
    s0_map = (lambda b, h, i: (0, h, 0, 0)) if s0_shared else (lambda b, h, i: (b, h, 0, 0))

    def col(group_off):
        return pl.BlockSpec((1, tt, hps * HEAD_DIM), lambda b, h, i: (b, i, group_off + h))

    return pl.pallas_call(
        functools.partial(_hgrn_kernel, chunk=chunk),
        grid=(nb, ng, t // tt),
        in_specs=[
            col(0), col(ng), col(2 * ng), col(3 * ng),
            pl.BlockSpec((hps, nslot, HEAD_DIM), lambda b, h, i: (h, 0, 0)),
            pl.BlockSpec((hps, 1, HEAD_DIM), lambda b, h, i: (h, 0, 0)),
            pl.BlockSpec((1, hps, HEAD_DIM, HEAD_DIM), s0_map),
        ],
        out_specs=[
            pl.BlockSpec((1, tt, hps * HEAD_DIM), lambda b, h, i: (b, i, h)),
            pl.BlockSpec((1, hps, HEAD_DIM, HEAD_DIM), lambda b, h, i: (b, h, 0, 0)),
        ],
        out_shape=[
            jax.ShapeDtypeStruct((nb, t, nh * HEAD_DIM), BF16),
            jax.ShapeDtypeStruct((nb, nh, HEAD_DIM, HEAD_DIM), F32),
        ],
        scratch_shapes=[pltpu.VMEM((hps, HEAD_DIM, HEAD_DIM), F32)],
        compiler_params=_params(("parallel", "parallel", "arbitrary")),
        name="hgrn",
    )(z3, z3, z3, z3, lb_logits, gain, s0t)


def _slice_prefetch(k_hbm, v_hbm, kbuf, vbuf, sems, lin, nslices, nheads):
    def copies(idx):
        seq_idx, head = idx // nheads, idx % nheads
        return (pltpu.make_async_copy(k_hbm.at[seq_idx, :, head, :], kbuf.at[idx % 2],
                                      sems.at[0, idx % 2]),
                pltpu.make_async_copy(v_hbm.at[seq_idx, :, head, :], vbuf.at[idx % 2],
                                      sems.at[1, idx % 2]))

    @pl.when(lin == 0)
    def _():
        for c in copies(lin):
            c.start()
    for c in copies(lin):
        c.wait()

    @pl.when(lin + 1 < nslices)
    def _():
        for c in copies(lin + 1):
            c.start()


def _fox_kernel(q_ref, cq_ref, kp_ref, vp_ref, ckp_ref, ks_ref, vs_ref, cks_ref, o_ref,
                *, pchunk, scale):
    tq = q_ref.shape[1]
    nheads = ckp_ref.shape[1]
    plen = ckp_ref.shape[3]
    lane = lax.broadcasted_iota(jnp.int32, (tq, LANES), 1)
    causal = (lax.broadcasted_iota(jnp.int32, (tq, tq), 1)
              <= lax.broadcasted_iota(jnp.int32, (tq, tq), 0))
    starts = [pc * pchunk for pc in range(plen // pchunk)]
    for hd in range(nheads):
        cols = slice(hd * HEAD_DIM, (hd + 1) * HEAD_DIM)
        q = q_ref[0, :, cols].astype(BF16)
        cq = jnp.sum(jnp.where(lane == hd, cq_ref[0], 0.0), axis=-1, keepdims=True)

        def head_rows(ref, start):
            return ref.at[0][pl.ds(start * nheads + hd, pchunk, stride=nheads), :]

        def logits(kb, ck):
            return lax.dot_general(q, kb.astype(BF16), (((1,), (1,)), ((), ())),
                                   preferred_element_type=F32) * scale + (cq - ck)

        s_list = [logits(head_rows(kp_ref, st), ckp_ref[0, hd, :, st:st + pchunk]) for st in starts]
        s_list.append(jnp.where(causal, logits(ks_ref[0, :, cols], cks_ref[0, hd]), -jnp.inf))
        v_list = [head_rows(vp_ref, st) for st in starts] + [vs_ref[0, :, cols]]
        m = functools.reduce(jnp.maximum, [jnp.max(s, axis=-1, keepdims=True) for s in s_list])
        l = jnp.zeros((tq, 1), F32)
        acc = jnp.zeros((tq, HEAD_DIM), F32)
        for s, vb in zip(s_list, v_list):
            p = jnp.exp(s - m)
            l = l + jnp.sum(p, axis=-1, keepdims=True)
            acc = acc + jnp.dot(p.astype(BF16), vb.astype(BF16), preferred_element_type=F32)
        o_ref[0, :, cols] = (acc / l).astype(o_ref.dtype)


def _fox(z3, b_off, qcol, kcol, vcol, cq, kp, vp, ckp, cks):
    nb, t, _ = cq.shape
    nh = cks.shape[1]
    plen = kp.shape[1]
    hw = nh * HEAD_DIM
    kern = functools.partial(_fox_kernel, pchunk=_pick(plen, 512), scale=HEAD_DIM ** -0.5)
    return pl.pallas_call(
        kern,
        grid=(nb,),
        in_specs=[
            pl.BlockSpec((1, t, hw), lambda b: (b + b_off, 0, qcol)),
            pl.BlockSpec((1, t, LANES), lambda b: (b, 0, 0)),
            pl.BlockSpec((1, plen * nh, HEAD_DIM), lambda b: (b, 0, 0)),
            pl.BlockSpec((1, plen * nh, HEAD_DIM), lambda b: (b, 0, 0)),
            pl.BlockSpec((1, nh, 1, plen), lambda b: (b, 0, 0, 0)),
            pl.BlockSpec((1, t, hw), lambda b: (b + b_off, 0, kcol)),
            pl.BlockSpec((1, t, hw), lambda b: (b + b_off, 0, vcol)),
            pl.BlockSpec((1, nh, 1, t), lambda b: (b, 0, 0, 0)),
        ],
        out_specs=pl.BlockSpec((1, t, hw), lambda b: (b, 0, 0)),
        out_shape=jax.ShapeDtypeStruct((nb, t, hw), BF16),
        compiler_params=_params(("parallel",)),
        name="fox",
    )(z3, cq, kp.reshape(nb, plen * nh, HEAD_DIM), vp.reshape(nb, plen * nh, HEAD_DIM),
      ckp, z3, z3, cks)


def _bias_pieces(c):
    hi = c.astype(BF16).astype(F32)
    r = c - hi
    mid = r.astype(BF16).astype(F32)
    return hi, mid, r - mid


def _bias_rows(c, ones_first):
    hi, mid, lo = _bias_pieces(c)
    sub = lax.broadcasted_iota(jnp.int32, (HEAD_DIM, c.shape[1]), 0)
    third = jnp.where(sub >= 3, sub - 3, sub)
    pieces = jnp.where(third == 0, hi, jnp.where(third == 1, mid, lo))
    first, second = (1.0, pieces) if ones_first else (pieces, 1.0)
    return jnp.where(sub < 3, first, jnp.where(sub < 6, second, 0.0))


def _bias_cols(c):
    hi, mid, lo = _bias_pieces(c)
    lane = lax.broadcasted_iota(jnp.int32, (c.shape[0], LANES), 1)
    pieces = jnp.where(lane == 3, hi, jnp.where(lane == 4, mid, lo))
    return jnp.where(lane < 3, 1.0, jnp.where(lane < 6, pieces, 0.0))


def _fox_prompt_kernel(q_ref, cq_ref, k_hbm, v_hbm, ck_ref, ckp_ref, o_ref,
                       kaug_scr, vt_scr, sa_scr, sb_scr, acc_scr, kbuf, vbuf, sems, *, scale):
    h = pl.program_id(1)
    qi = pl.program_id(2)
    nheads = pl.num_programs(1)
    nslices = pl.num_programs(0) * nheads
    lin = pl.program_id(0) * nheads + h
    slot = lin % 2
    tq = q_ref.shape[1]
    tk = sa_scr.shape[0]
    plen = ckp_ref.shape[1]
    t = kbuf.shape[1] - plen

    @pl.when(qi == 0)
    def _():
        _slice_prefetch(k_hbm, v_hbm, kbuf, vbuf, sems, lin, nslices, nheads)

        def prep(cix, carry):
            r0 = pl.multiple_of(cix * tk, tk)
            src = pl.ds(pl.multiple_of(plen + r0, 8), tk)
            ck = ck_ref[0, 0, :, pl.ds(r0, tk)] * LOG2E
            kaug_scr[pl.ds(r0, tk), HEAD_DIM:] = _bias_rows(-ck, True).T.astype(BF16)
            kaug_scr[pl.ds(r0, tk), :HEAD_DIM] = kbuf[slot, src, :].astype(BF16)
            vt_scr[:, pl.ds(r0, tk)] = vbuf[slot, src, :].T.astype(BF16)
            return carry
        lax.fori_loop(0, t // tk, prep, 0)

    qt = (q_ref[0] * (scale * LOG2E)).T.astype(BF16)
    augq = _bias_rows(cq_ref[0, 0] * LOG2E, False).astype(BF16)
    qaug = jnp.concatenate([qt, augq], axis=0)

    def row0(blk):
        return pl.multiple_of(blk * tk, tk)

    def colmax(s):
        return jnp.max(s, axis=0, keepdims=True)

    def stage(blk, s_scr):
        s = jnp.dot(kaug_scr[pl.ds(row0(blk), tk), :], qaug, preferred_element_type=F32)
        s_scr[...] = s
        return colmax(s)

    def pv_block(blk):
        return lambda p: jnp.dot(vt_scr[:, pl.ds(row0(blk), tk)], p, preferred_element_type=F32)

    def absorb(ml, blocks):
        m, l = ml
        m_new = m
        for _, smax, _ in blocks:
            m_new = jnp.maximum(m_new, smax)
        alpha = jnp.exp2(m - m_new)
        l = alpha * l
        pv = None
        for s, _, pv_fn in blocks:
            p = jnp.exp2(s - m_new)
            l = l + jnp.sum(p, axis=0, keepdims=True)
            term = pv_fn(p.astype(BF16))
            pv = term if pv is None else pv + term
        acc_scr[...] = alpha * acc_scr[...] + pv
        return m_new, l

    acc_scr[...] = jnp.zeros_like(acc_scr)
    ml = (jnp.full((1, tq), NEG_BIG, F32), jnp.zeros((1, tq), F32))

    def body(pair, state):
        ma, ml = state
        mb = stage(2 * pair + 1, sb_scr)
        ml = absorb(ml, [(sa_scr[...], ma, pv_block(2 * pair))])
        ma = stage(2 * pair + 2, sa_scr)
        ml = absorb(ml, [(sb_scr[...], mb, pv_block(2 * pair + 1))])
        return ma, ml

    _, ml = lax.fori_loop(0, qi, body, (stage(0, sa_scr), ml))

    stage(2 * qi + 1, sb_scr)
    krow = lax.broadcasted_iota(jnp.int32, (tk, tq), 0)
    qcolumn = lax.broadcasted_iota(jnp.int32, (tk, tq), 1)
    s_a = jnp.where(krow <= qcolumn, sa_scr[...], -jnp.inf)
    s_b = jnp.where(krow + tk <= qcolumn, sb_scr[...], -jnp.inf)
    lane = lax.broadcasted_iota(jnp.int32, ckp_ref.shape[1:], 1)
    ckp = jnp.sum(jnp.where(lane == h, ckp_ref[0], 0.0), axis=-1, keepdims=True) * LOG2E
    kp = jnp.concatenate([kbuf[slot, :plen, :].astype(BF16), _bias_cols(-ckp).astype(BF16)], axis=1)
    vp = vbuf[slot, :plen, :].astype(BF16)
    s_p = jnp.dot(kp, qaug, preferred_element_type=F32)
    m, l = absorb(ml, [
        (s_p, colmax(s_p), lambda p: lax.dot_general(vp, p, (((0,), (0,)), ((), ())),
                                                     preferred_element_type=F32)),
        (s_a, colmax(s_a), pv_block(2 * qi)),
        (s_b, colmax(s_b), pv_block(2 * qi + 1)),
    ])
    o_ref[0] = (acc_scr[...] * (1.0 / l)).T.astype(o_ref.dtype)


def _fox_prompt(z3, qcol, k_cache, v_cache, c_row, cp_col, *, tq):
    nb, t, _ = z3.shape
    nh = c_row.shape[1]
    plen = cp_col.shape[1]
    kern = functools.partial(_fox_prompt_kernel, scale=HEAD_DIM ** -0.5)
    return pl.pallas_call(
        kern,
        grid=(nb, nh, t // tq),
        in_specs=[
            pl.BlockSpec((1, tq, HEAD_DIM), lambda b, h, i: (b, i, qcol + h)),
            pl.BlockSpec((1, 1, 1, tq), lambda b, h, i: (b, h, 0, i)),
            pl.BlockSpec(memory_space=pl.ANY),
            pl.BlockSpec(memory_space=pl.ANY),
            pl.BlockSpec((1, 1, 1, t), lambda b, h, i: (b, h, 0, 0)),
            pl.BlockSpec((1, plen, LANES), lambda b, h, i: (0, 0, 0)),
        ],
        out_specs=pl.BlockSpec((1, tq, HEAD_DIM), lambda b, h, i: (b, i, h)),
        out_shape=jax.ShapeDtypeStruct((nb, t, nh * HEAD_DIM), BF16),
        scratch_shapes=[pltpu.VMEM((t, 2 * HEAD_DIM), BF16), pltpu.VMEM((HEAD_DIM, t), BF16),
                        pltpu.VMEM((tq // 2, tq), F32), pltpu.VMEM((tq // 2, tq), F32),
                        pltpu.VMEM((HEAD_DIM, tq), F32),
                        pltpu.VMEM((2, plen + t, HEAD_DIM), F32),
                        pltpu.VMEM((2, plen + t, HEAD_DIM), F32),
                        pltpu.SemaphoreType.DMA((2, 2))],
        compiler_params=_params(("arbitrary", "arbitrary", "arbitrary")),
        name="fox_prompt",
    )(z3, c_row, k_cache, v_cache, c_row, cp_col)


def _outproj_kernel(h_ref, a_ref, b_ref, wa_ref, wb_ref, g_ref, o_ref):
    mix = (jnp.dot(a_ref[...], wa_ref[...], preferred_element_type=F32)
           + jnp.dot(b_ref[...], wb_ref[...], preferred_element_type=F32))
    o_ref[...] = h_ref[...] + _rms(mix, g_ref[...])


def _outproj(h, mix_a, mix_b, w_a, w_b, gain):
    rows, d = h.shape
    ka, kb = mix_a.shape[1], mix_b.shape[1]
    tm = _pick(rows, 512)
    return pl.pallas_call(
        _outproj_kernel,
        grid=(rows // tm,),
        in_specs=[
            pl.BlockSpec((tm, d), lambda i: (i, 0)),
            pl.BlockSpec((tm, ka), lambda i: (i, 0)),
            pl.BlockSpec((tm, kb), lambda i: (i, 0)),
            pl.BlockSpec((ka, d), lambda i: (0, 0)),
            pl.BlockSpec((kb, d), lambda i: (0, 0)),
            pl.BlockSpec((1, d), lambda i: (0, 0)),
        ],
        out_specs=pl.BlockSpec((tm, d), lambda i: (i, 0)),
        out_shape=jax.ShapeDtypeStruct((rows, d), F32),
        compiler_params=_params(("parallel",)),
        name="outproj",
    )(h, mix_a, mix_b, w_a, w_b, gain)


def _mlp_kernel(h_ref, gpre_ref, wu_ref, wd_ref, gpost_ref, o_ref, a_scr, acc_scr):
    j = pl.program_id(1)

    @pl.when(j == 0)
    def _():
        a_scr[...] = _rms(h_ref[...], gpre_ref[...]).astype(BF16)
        acc_scr[...] = jnp.zeros_like(acc_scr)

    u = jnp.maximum(jnp.dot(a_scr[...], wu_ref[...], preferred_element_type=F32), 0.0)
    acc_scr[...] += jnp.dot((u * u).astype(BF16), wd_ref[...], preferred_element_type=F32)

    @pl.when(j == pl.num_programs(1) - 1)
    def _():
        o_ref[...] = h_ref[...] + _rms(acc_scr[...], gpost_ref[...])


def _mlp(h, g_pre, w_up, w_down, g_post):
    rows, d = h.shape
    dff = w_up.shape[1]
    tm = _pick(rows, 512)
    tf = _pick(dff, 1024)
    return pl.pallas_call(
        _mlp_kernel,
        grid=(rows // tm, dff // tf),
        in_specs=[
            pl.BlockSpec((tm, d), lambda i, j: (i, 0)),
            pl.BlockSpec((1, d), lambda i, j: (0, 0)),
            pl.BlockSpec((d, tf), lambda i, j: (0, j)),
            pl.BlockSpec((tf, d), lambda i, j: (j, 0)),
            pl.BlockSpec((1, d), lambda i, j: (0, 0)),
        ],
        out_specs=pl.BlockSpec((tm, d), lambda i, j: (i, 0)),
        out_shape=jax.ShapeDtypeStruct((rows, d), F32),
        scratch_shapes=[pltpu.VMEM((tm, d), BF16), pltpu.VMEM((tm, d), F32)],
        compiler_params=_params(("parallel", "arbitrary")),
        name="mlp",
    )(h, g_pre, w_up, w_down, g_post)


def _col_form(c_row):
    c_col = jnp.transpose(c_row, (0, 2, 1))
    return jnp.pad(c_col, ((0, 0), (0, 0), (0, LANES - c_col.shape[2])))


def kernel(x_prompt, x_sample, cache_fox_k, cache_fox_v, cache_fox_logf, state_hgrn, meta_tokens,
           g_mix_pre, w_in, hg_lb_logits, hg_norm_gain, fox_f_bias, w_out, g_mix_post, g_mlp_pre,
           w_up, w_down, g_mlp_post):
    bp, seq, d = x_prompt.shape
    bs, dseq, _ = x_sample.shape
    n_meta = meta_tokens.shape[0]
    past = cache_fox_k.shape[2]
    nh = cache_fox_k.shape[3]
    hw = nh * HEAD_DIM
    n_main = w_in.shape[2] - nh
    assert dseq == n_meta, "sample frames and meta tokens share the small-stream kernels"
    assert state_hgrn.shape[2] == nh and n_main == 7 * hw

    w_main = w_in[0, :, :n_main].astype(BF16)
    w_f = jnp.pad(w_in[0, :, n_main:], ((0, 0), (0, LANES - nh))).astype(BF16)
    f_bias = jnp.pad(fox_f_bias[0], (0, LANES - nh))[None, :]
    w_oa = w_out[0, :hw].astype(BF16)
    w_ob = w_out[0, hw:].astype(BF16)
    w_u = w_up[0].astype(BF16)
    w_d = w_down[0].astype(BF16)
    g_pre = g_mix_pre[0][None, :]
    g_post = g_mix_post[0][None, :]
    g_mpre = g_mlp_pre[0][None, :]
    g_mpost = g_mlp_post[0][None, :]
    lb_logits = jnp.transpose(hg_lb_logits.reshape(-1, nh, HEAD_DIM), (1, 0, 2))
    hg_gain = hg_norm_gain[0].reshape(nh, 1, HEAD_DIM)
    qcol, kcol, vcol = 4 * nh, 5 * nh, 6 * nh

    ns = bs + 1
    xs = jnp.concatenate([meta_tokens, x_sample.reshape(bs * dseq, d)], axis=0)
    zs, lfs = _inproj(xs, g_pre, w_main, w_f, f_bias)
    zs3 = zs.reshape(ns, dseq, n_main)
    lfs3 = lfs.reshape(ns, dseq, LANES)

    c_past = _cumsum_time(cache_fox_logf[0], jnp.zeros((bs, nh), F32), nh)
    init_s = jnp.concatenate([jnp.zeros((1, nh), F32), c_past[:, :, past - 1]], axis=0)
    c_small = _cumsum_time(lfs3, init_s, nh)

    s0_small = jnp.concatenate([jnp.zeros((1,) + state_hgrn.shape[2:], F32), state_hgrn[0]], axis=0)
    s0_small_t = jnp.swapaxes(s0_small, -1, -2)
    mix_hg_s, sfin_s_t = _hgrn(zs3, lb_logits, hg_gain, s0_small_t, s0_shared=False,
                               chunk=dseq, tt=dseq, hps=nh)

    mix_fox_s = _fox(zs3, 1, qcol // nh, kcol // nh, vcol // nh, _col_form(c_small[1:]),
                     cache_fox_k[0], cache_fox_v[0], c_past[:, :, None, :], c_small[1:, :, None, :])

    xm = x_prompt.reshape(bp * seq, d)
    meta_k = zs3[0, :, kcol * HEAD_DIM:kcol * HEAD_DIM + hw].reshape(n_meta, nh, HEAD_DIM)
    meta_v = zs3[0, :, vcol * HEAD_DIM:vcol * HEAD_DIM + hw].reshape(n_meta, nh, HEAD_DIM)
    zm, lfm, k_cache, v_cache = _inproj_kv(xm, g_pre, w_main, w_f, f_bias, meta_k, meta_v,
                                           nseq=bp, nz=kcol // nh)
    zm3 = zm.reshape(bp, seq, kcol * HEAD_DIM)
    lfm3 = lfm.reshape(bp, seq, LANES)
    c_meta = c_small[0:1]
    init_m = jnp.broadcast_to(c_meta[:, :, n_meta - 1], (bp, nh))
    c_main = _cumsum_time(lfm3, init_m, nh)

    mix_hg_m, sfin_m_t = _hgrn(zm3, lb_logits, hg_gain, sfin_s_t[0:1], s0_shared=True,
                               chunk=_pick(seq, 128), tt=_pick(seq, 2048), hps=1)
    mix_fox_m = _fox_prompt(zm3, qcol, k_cache, v_cache, c_main[:, :, None, :], _col_form(c_meta),
                            tq=_pick(seq, 1024))

    h1m = _outproj(xm, mix_hg_m.reshape(bp * seq, hw), mix_fox_m.reshape(bp * seq, hw),
                   w_oa, w_ob, g_post)
    y_prompt = _mlp(h1m, g_mpre, w_u, w_d, g_mpost).reshape(bp, seq, d)
    h1s = _outproj(x_sample.reshape(bs * dseq, d), mix_hg_s[1:].reshape(bs * dseq, hw),
                   mix_fox_s.reshape(bs * dseq, hw), w_oa, w_ob, g_post)
    y_sample = _mlp(h1s, g_mpre, w_u, w_d, g_mpost).reshape(bs, dseq, d)

    meta_lf = jnp.broadcast_to(lfs3[0:1, :, :nh], (bp, n_meta, nh))
    lf_p = jnp.concatenate([meta_lf, lfm3[:, :, :nh]], axis=1)[None]
    k_s = zs3[1:, :, kcol * HEAD_DIM:kcol * HEAD_DIM + hw]
    v_s = zs3[1:, :, vcol * HEAD_DIM:vcol * HEAD_DIM + hw]
    return (y_prompt, y_sample,
            k_cache[None], v_cache[None],
            lf_p,
            jnp.swapaxes(sfin_m_t, -1, -2)[None],
            k_s.reshape(1, bs, dseq, nh, HEAD_DIM),
            v_s.reshape(1, bs, dseq, nh, HEAD_DIM),
            lfs3[1:, :, :nh][None],
            jnp.swapaxes(sfin_s_t[1:], -1, -2)[None])
```

```python
import functools

import jax
import jax.numpy as jnp
from jax import lax
from jax.experimental import pallas as pl
from jax.experimental.pallas import tpu as pltpu

F32 = jnp.float32
BF16 = jnp.bfloat16

EPS = 1e-6
HEAD_DIM = 128
LANES = 128
NEG_BIG = -1e30
LOG2E = 1.4426950408889634
VMEM_LIMIT = 56 * 1024 * 1024


def _pick(n, pref):
    if n <= pref:
        return n
    t = pref
    while n % t:
        t //= 2
    return t


def _params(sem):
    return pltpu.CompilerParams(dimension_semantics=sem, vmem_limit_bytes=VMEM_LIMIT)


def _log_sigmoid(x):
    return jnp.minimum(x, 0.0) - jnp.log1p(jnp.exp(-jnp.abs(x)))


def _rms(x, gain):
    ms = jnp.mean(x * x, axis=-1, keepdims=True)
    return x * lax.rsqrt(ms + EPS) * gain


def _silu(t):
    return t * jax.nn.sigmoid(t)


def _log2_forget(t, lbl_ref):
    lg = lbl_ref[...]
    e = jnp.exp(lg - jnp.max(lg, axis=0, keepdims=True))
    lb = e[0:1, :] / jnp.sum(e, axis=0, keepdims=True)
    return jnp.log(lb + (1.0 - lb) * jax.nn.sigmoid(t)) * LOG2E


def _store_heads(z_ref, tile):
    for hd in range(z_ref.shape[0]):
        z_ref[hd] = tile[:, hd * HEAD_DIM:(hd + 1) * HEAD_DIM]


def _store_z_tiles(j, project, lbl_ref, z_ref):
    @pl.when((j == 0) | (j == 3))
    def _():
        _store_heads(z_ref, _silu(project()))

    @pl.when(j == 1)
    def _():
        _store_heads(z_ref, _log2_forget(project(), lbl_ref))

    @pl.when((j == 2) | (j == 4))
    def _():
        _store_heads(z_ref, project() * jnp.where(j == 4, HEAD_DIM ** -0.5 * LOG2E, 1.0))


def _inproj_kernel(x_ref, g_ref, w_ref, wf_ref, fb_ref, lbl_ref, z_ref, lf_ref, a_scr):
    j = pl.program_id(1)

    @pl.when(j == 0)
    def _():
        a = _rms(x_ref[...], g_ref[...]).astype(BF16)
        a_scr[...] = a
        ff = jnp.dot(a, wf_ref[...], preferred_element_type=F32) + fb_ref[...]
        lf_ref[...] = _log_sigmoid(ff)

    def project():
        return jnp.dot(a_scr[...], w_ref[...], preferred_element_type=F32)

    _store_z_tiles(j, project, lbl_ref, z_ref)

    @pl.when(j >= 5)
    def _():
        _store_heads(z_ref, project())


def _inproj(x, gain, w_main, w_f, f_bias, lb_logits):
    rows, d = x.shape
    n = w_main.shape[1]
    tm = _pick(rows, 1024)
    tn = lb_logits.shape[1]
    nh = tn // HEAD_DIM
    return pl.pallas_call(
        _inproj_kernel,
        grid=(rows // tm, n // tn),
        in_specs=[
            pl.BlockSpec((tm, d), lambda i, j: (i, 0)),
            pl.BlockSpec((1, d), lambda i, j: (0, 0)),
            pl.BlockSpec((d, tn), lambda i, j: (0, j)),
            pl.BlockSpec((d, LANES), lambda i, j: (0, 0)),
            pl.BlockSpec((1, LANES), lambda i, j: (0, 0)),
            pl.BlockSpec(lb_logits.shape, lambda i, j: (0, 0)),
        ],
        out_specs=[
            pl.BlockSpec((nh, tm, HEAD_DIM), lambda i, j: (j, i, 0)),
            pl.BlockSpec((tm, LANES), lambda i, j: (i, 0)),
        ],
        out_shape=[
            jax.ShapeDtypeStruct((n // HEAD_DIM, rows, HEAD_DIM), F32),
            jax.ShapeDtypeStruct((rows, LANES), F32),
        ],
        scratch_shapes=[pltpu.VMEM((tm, d), BF16)],
        compiler_params=_params(("parallel", "arbitrary")),
        name="inproj",
    )(x, gain, w_main, w_f, f_bias, lb_logits)


def _inproj_kv_kernel(x_ref, g_ref, w_ref, wf_ref, fb_ref, lbl_ref, pk_ref, pv_ref,
                      z_ref, lf_ref, k_hbm, v_hbm, a_scr, stage, sems,
                      *, nz, tiles_per_seq):
    i = pl.program_id(0)
    j = pl.program_id(1)
    tm = x_ref.shape[0]
    plen = pk_ref.shape[0]
    last = pl.num_programs(0) - 1

    nh = k_hbm.shape[2]

    class _TileCopy:
        def __init__(self, slot, dst, tile):
            row = (tile % tiles_per_seq) * tm + plen
            self.copies = [
                pltpu.make_async_copy(stage.at[slot, :, pl.ds(hd * HEAD_DIM, HEAD_DIM)],
                                      dst.at[tile // tiles_per_seq, pl.ds(row, tm), hd, :],
                                      sems.at[slot])
                for hd in range(nh)]

        def start(self):
            for c in self.copies:
                c.start()

        def wait(self):
            for c in self.copies:
                c.wait()

    tile_copy = _TileCopy

    def prefix_copy(slot, src, dst, tile):
        return pltpu.make_async_copy(src, dst.at[tile // tiles_per_seq, pl.ds(0, plen), :, :],
                                     sems.at[2 + slot])

    def opens_sequence(tile):
        return tile % tiles_per_seq == 0

    @pl.when(j == 0)
    def _():
        a = _rms(x_ref[...], g_ref[...]).astype(BF16)
        a_scr[...] = a
        ff = jnp.dot(a, wf_ref[...], preferred_element_type=F32) + fb_ref[...]
        lf_ref[...] = _log_sigmoid(ff)

    def project():
        return jnp.dot(a_scr[...], w_ref[...], preferred_element_type=F32)

    _store_z_tiles(j, project, lbl_ref, z_ref)

    @pl.when(j == nz)
    def _():
        @pl.when(i > 0)
        def _():
            tile_copy(1, v_hbm, i - 1).wait()

            @pl.when(opens_sequence(i - 1))
            def _():
                prefix_copy(1, pv_ref, v_hbm, i - 1).wait()
        stage[0] = project()
        tile_copy(0, k_hbm, i).start()

        @pl.when(opens_sequence(i))
        def _():
            prefix_copy(0, pk_ref, k_hbm, i).start()

    @pl.when(j == nz + 1)
    def _():
        stage[1] = project()
        tile_copy(1, v_hbm, i).start()
        tile_copy(0, k_hbm, i).wait()

        @pl.when(opens_sequence(i))
        def _():
            prefix_copy(1, pv_ref, v_hbm, i).start()
            prefix_copy(0, pk_ref, k_hbm, i).wait()

        @pl.when(i == last)
        def _():
            tile_copy(1, v_hbm, i).wait()

            @pl.when(opens_sequence(i))
            def _():
                prefix_copy(1, pv_ref, v_hbm, i).wait()


def _inproj_kv(x, gain, w_main, w_f, f_bias, lb_logits, prefix_k, prefix_v, *, nseq, nz):
    rows, d = x.shape
    n = w_main.shape[1]
    tn = n // (nz + 2)
    seq = rows // nseq
    plen, nh, _ = prefix_k.shape
    tm = _pick(seq, 1024)
    kern = functools.partial(_inproj_kv_kernel, nz=nz, tiles_per_seq=seq // tm)
    cache = jax.ShapeDtypeStruct((nseq, plen + seq, nh, HEAD_DIM), F32)
    return pl.pallas_call(
        kern,
        grid=(rows // tm, nz + 2),
        in_specs=[
            pl.BlockSpec((tm, d), lambda i, j: (i, 0)),
            pl.BlockSpec((1, d), lambda i, j: (0, 0)),
            pl.BlockSpec((d, tn), lambda i, j: (0, j)),
            pl.BlockSpec((d, LANES), lambda i, j: (0, 0)),
            pl.BlockSpec((1, LANES), lambda i, j: (0, 0)),
            pl.BlockSpec(lb_logits.shape, lambda i, j: (0, 0)),
            pl.BlockSpec((plen, nh, HEAD_DIM), lambda i, j: (0, 0, 0)),
            pl.BlockSpec((plen, nh, HEAD_DIM), lambda i, j: (0, 0, 0)),
        ],
        out_specs=[
            pl.BlockSpec((nh, tm, HEAD_DIM), lambda i, j: (jnp.minimum(j, nz - 1), i, 0)),
            pl.BlockSpec((tm, LANES), lambda i, j: (i, 0)),
            pl.BlockSpec(memory_space=pl.ANY),
            pl.BlockSpec(memory_space=pl.ANY),
        ],
        out_shape=[
            jax.ShapeDtypeStruct((nz * nh, rows, HEAD_DIM), F32),
            jax.ShapeDtypeStruct((rows, LANES), F32),
            cache, cache,
        ],
        scratch_shapes=[pltpu.VMEM((tm, d), BF16), pltpu.VMEM((2, tm, tn), F32),
                        pltpu.SemaphoreType.DMA((4,))],
        compiler_params=_params(("arbitrary", "arbitrary")),
        name="inproj_kv",
    )(x, gain, w_main, w_f, f_bias, lb_logits, prefix_k, prefix_v)


def _split3(x):
    hi = x.astype(BF16)
    r = x - hi.astype(F32)
    mid = r.astype(BF16)
    lo = (r - mid.astype(F32)).astype(BF16)
    return hi, mid, lo


def _cumsum_kernel(x_ref, init_ref, c_ref, *, chain):
    g, nblk, blk = x_ref.shape
    upper = (lax.broadcasted_iota(jnp.int32, (blk, blk), 0)
             <= lax.broadcasted_iota(jnp.int32, (blk, blk), 1)).astype(BF16)
    if chain:
        before = (lax.broadcasted_iota(jnp.int32, (nblk, nblk), 1)
                  < lax.broadcasted_iota(jnp.int32, (nblk, nblk), 0)).astype(BF16)
    for r in range(g):
        c = sum(jnp.dot(p, upper, preferred_element_type=F32) for p in _split3(x_ref[r]))
        if chain:
            tot = jnp.broadcast_to(c[:, blk - 1:blk], (nblk, blk))
            c = c + sum(jnp.dot(before, p, preferred_element_type=F32) for p in _split3(tot))
        c_ref[r] = c + init_ref[r]


def _cumsum(x, init, *, chain):
    n, nblk, blk = x.shape
    g = _pick(n, 32)
    ni = init.shape[1]
    return pl.pallas_call(
        functools.partial(_cumsum_kernel, chain=chain),
        grid=(n // g,),
        in_specs=[
            pl.BlockSpec((g, nblk, blk), lambda i: (i, 0, 0)),
            pl.BlockSpec((g, ni, blk), lambda i: (i, 0, 0)),
        ],
        out_specs=pl.BlockSpec((g, nblk, blk), lambda i: (i, 0, 0)),
        out_shape=jax.ShapeDtypeStruct((n, nblk, blk), F32),
        compiler_params=_params(("parallel",)),
        name="cumsum",
    )(x, init)


def _cumsum_time(lf, init, nh):
    nb, t = lf.shape[0], lf.shape[1]
    blk = _pick(t, 256)
    x = jnp.transpose(lf[:, :, :nh], (0, 2, 1))
    if t == blk:
        init3 = jnp.broadcast_to(init.reshape(1, nb * nh, 1), (1, nb * nh, blk))
        return _cumsum(x.reshape(1, nb * nh, blk), init3, chain=False).reshape(nb, nh, t)
    init3 = jnp.broadcast_to(init.reshape(nb * nh, 1, 1), (nb * nh, 1, blk))
    return _cumsum(x.reshape(nb * nh, t // blk, blk), init3, chain=True).reshape(nb, nh, t)


def _block_mid(pre, half):
    c = pre.shape[0]
    if half >= 8:
        x3 = pre.reshape(c // (2 * half), 2 * half, HEAD_DIM)
        return jnp.broadcast_to(x3[:, half - 1:half, :], x3.shape).reshape(c, HEAD_DIM)
    x3 = pre.reshape(c // 8, 8, HEAD_DIM)
    sub = lax.broadcasted_iota(jnp.int32, x3.shape, 1)
    mid = None
    for blk in range(8 // (2 * half)):
        src = 2 * half * blk + half - 1
        piece = jnp.broadcast_to(x3[:, src:src + 1, :], x3.shape)
        mid = piece if mid is None else jnp.where(sub >= 2 * half * blk, piece, mid)
    return mid.reshape(c, HEAD_DIM)


def _hgrn_masks(c):
    rows = lax.broadcasted_iota(jnp.int32, (c, HEAD_DIM), 0)
    ri = lax.broadcasted_iota(jnp.int32, (c, c), 0)
    ci = lax.broadcasted_iota(jnp.int32, (c, c), 1)
    levels = c.bit_length() - 1
    upper = [(rows & (1 << (lvl - 1))) != 0 for lvl in range(1, levels + 1)]
    owner = jnp.where(ri >= ci, 32 - lax.clz(ri ^ ci), -1)
    return owner, upper


def _hgrn_chunk(q, g2, hi, st, masks):
    c = q.shape[0]
    owner, upper_masks = masks
    k = 1.0 - jnp.exp2(g2)

    scores = jnp.where(owner == 0, jnp.sum(q * k, axis=-1, keepdims=True), 0.0)
    pre = g2
    for lvl, upper in enumerate(upper_masks, start=1):
        mid = _block_mid(pre, 1 << (lvl - 1))
        w = jnp.exp2(jnp.where(upper, pre, mid - pre))
        x = (jnp.where(upper, q, k) * w).astype(BF16)
        s_l = lax.dot_general(x, x, (((1,), (1,)), ((), ())), preferred_element_type=F32)
        scores = jnp.where(owner == lvl, s_l, scores)
        pre = jnp.where(upper, pre + mid, pre)
    cum = pre
    v = hi.astype(BF16)
    qe = (q * jnp.exp2(cum)).astype(BF16)
    o = (lax.dot_general(qe, st.astype(BF16), (((1,), (1,)), ((), ())), preferred_element_type=F32)
         + jnp.dot(scores.astype(BF16), v, preferred_element_type=F32))
    last = cum[c - 1:c, :]
    kd = (k * jnp.exp2(last - cum)).astype(BF16)
    st_new = st * jnp.exp2(last) + lax.dot_general(
        v, kd, (((0,), (0,)), ((), ())), preferred_element_type=F32)
    return o, st_new


def _hgrn_kernel(hq_ref, hf_ref, hi_ref, hg_ref, gain_ref, s0_ref,
                 o_ref, sfin_ref, st_scr, *, chunk):
    ti = pl.program_id(2)
    hps = st_scr.shape[0]

    @pl.when(ti == 0)
    def _():
        st_scr[...] = s0_ref[0]

    tt = hq_ref.shape[1]
    masks = _hgrn_masks(chunk)
    for hh in range(hps):
        gain = gain_ref[hh]
        st = st_scr[hh]
        for cidx in range(tt // chunk):
            sl = slice(cidx * chunk, (cidx + 1) * chunk)
            o, st = _hgrn_chunk(hq_ref[hh, sl, :], hf_ref[hh, sl, :], hi_ref[hh, sl, :], st, masks)
            o_ref[hh, sl, :] = (_rms(o, gain) * hg_ref[hh, sl, :]).astype(o_ref.dtype)
        st_scr[hh] = st

        @pl.when(ti == pl.num_programs(2) - 1)
        def _():
            sfin_ref[0, hh] = st


def _hgrn(z, nb, gain, s0t, *, s0_shared, chunk, tt, hps):
    rows = z.shape[1]
    t = rows // nb
    nh = gain.shape[0]
    ng = nh // hps
    nt = t // tt
    s0_map = (lambda b, h, i: (0, h, 0, 0)) if s0_shared else (lambda b, h, i: (b, h, 0, 0))

    def col(tile):
        return pl.BlockSpec((hps, tt, HEAD_DIM), lambda b, h, i: (tile * ng + h, b * nt + i, 0))

    return pl.pallas_call(
        functools.partial(_hgrn_kernel, chunk=chunk),
        grid=(nb, ng, nt),
        in_specs=[
            col(0), col(1), col(2), col(3),
            pl.BlockSpec((hps, 1, HEAD_DIM), lambda b, h, i: (h, 0, 0)),
            pl.BlockSpec((1, hps, HEAD_DIM, HEAD_DIM), s0_map),
        ],
        out_specs=[
            pl.BlockSpec((hps, tt, HEAD_DIM), lambda b, h, i: (h, b * nt + i, 0)),
            pl.BlockSpec((1, hps, HEAD_DIM, HEAD_DIM), lambda b, h, i: (b, h, 0, 0)),
        ],
        out_shape=[
            jax.ShapeDtypeStruct((nh, rows, HEAD_DIM), BF16),
            jax.ShapeDtypeStruct((nb, nh, HEAD_DIM, HEAD_DIM), F32),
        ],
        scratch_shapes=[pltpu.VMEM((hps, HEAD_DIM, HEAD_DIM), F32)],
        compiler_params=_params(("parallel", "parallel", "arbitrary")),
        name="hgrn",
    )(z, z, z, z, gain, s0t)


def _slice_prefetch(k_hbm, v_hbm, kbuf, vbuf, sems, lin, nslices, nheads):
    def copies(idx):
        seq_idx, head = idx // nheads, idx % nheads
        return (pltpu.make_async_copy(k_hbm.at[seq_idx, :, head, :], kbuf.at[idx % 2],
                                      sems.at[0, idx % 2]),
                pltpu.make_async_copy(v_hbm.at[seq_idx, :, head, :], vbuf.at[idx % 2],
                                      sems.at[1, idx % 2]))

    @pl.when(lin == 0)
    def _():
        for c in copies(lin):
            c.start()
    for c in copies(lin):
        c.wait()

    @pl.when(lin + 1 < nslices)
    def _():
        for c in copies(lin + 1):
            c.start()


def _fox_kernel(q_ref, cq_ref, kp_ref, vp_ref, ckp_ref, ks_ref, vs_ref, cks_ref, o_ref,
                *, pchunk):
    tq = q_ref.shape[1]
    nheads = ckp_ref.shape[1]
    plen = ckp_ref.shape[3]
    lane = lax.broadcasted_iota(jnp.int32, (tq, LANES), 1)
    causal = (lax.broadcasted_iota(jnp.int32, (tq, tq), 1)
              <= lax.broadcasted_iota(jnp.int32, (tq, tq), 0))
    starts = [pc * pchunk for pc in range(plen // pchunk)]
    for hd in range(nheads):
        q = q_ref[hd].astype(BF16)
        cq = jnp.sum(jnp.where(lane == hd, cq_ref[0], 0.0), axis=-1, keepdims=True)

        def head_rows(ref, start):
            return ref.at[0][pl.ds(start * nheads + hd, pchunk, stride=nheads), :]

        def logits(kb, ck):
            return lax.dot_general(q, kb.astype(BF16), (((1,), (1,)), ((), ())),
                                   preferred_element_type=F32) + (cq - ck) * LOG2E

        s_list = [logits(head_rows(kp_ref, st), ckp_ref[0, hd, :, st:st + pchunk]) for st in starts]
        s_list.append(jnp.where(causal, logits(ks_ref[hd], cks_ref[0, hd]), -jnp.inf))
        v_list = [head_rows(vp_ref, st) for st in starts] + [vs_ref[hd]]
        m = functools.reduce(jnp.maximum, [jnp.max(s, axis=-1, keepdims=True) for s in s_list])
        l = jnp.zeros((tq, 1), F32)
        acc = jnp.zeros((tq, HEAD_DIM), F32)
        for s, vb in zip(s_list, v_list):
            p = jnp.exp2(s - m)
            l = l + jnp.sum(p, axis=-1, keepdims=True)
            acc = acc + jnp.dot(p.astype(BF16), vb.astype(BF16), preferred_element_type=F32)
        o_ref[hd] = (acc / l).astype(o_ref.dtype)


def _fox(z, b_off, qtile, ktile, vtile, cq, kp, vp, ckp, cks):
    nb, t, _ = cq.shape
    nh = cks.shape[1]
    plen = kp.shape[1]
    kern = functools.partial(_fox_kernel, pchunk=_pick(plen, 512))
    return pl.pallas_call(
        kern,
        grid=(nb,),
        in_specs=[
            pl.BlockSpec((nh, t, HEAD_DIM), lambda b: (qtile, b + b_off, 0)),
            pl.BlockSpec((1, t, LANES), lambda b: (b, 0, 0)),
            pl.BlockSpec((1, plen * nh, HEAD_DIM), lambda b: (b, 0, 0)),
            pl.BlockSpec((1, plen * nh, HEAD_DIM), lambda b: (b, 0, 0)),
            pl.BlockSpec((1, nh, 1, plen), lambda b: (b, 0, 0, 0)),
            pl.BlockSpec((nh, t, HEAD_DIM), lambda b: (ktile, b + b_off, 0)),
            pl.BlockSpec((nh, t, HEAD_DIM), lambda b: (vtile, b + b_off, 0)),
            pl.BlockSpec((1, nh, 1, t), lambda b: (b, 0, 0, 0)),
        ],
        out_specs=pl.BlockSpec((nh, t, HEAD_DIM), lambda b: (0, b, 0)),
        out_shape=jax.ShapeDtypeStruct((nh, nb * t, HEAD_DIM), BF16),
        compiler_params=_params(("parallel",)),
        name="fox",
    )(z, cq, kp.reshape(nb, plen * nh, HEAD_DIM), vp.reshape(nb, plen * nh, HEAD_DIM),
      ckp, z, z, cks)


def _bias_pieces(c):
    hi = c.astype(BF16).astype(F32)
    r = c - hi
    mid = r.astype(BF16).astype(F32)
    return hi, mid, r - mid


def _bias_rows(c, ones_first):
    hi, mid, lo = _bias_pieces(c)
    sub = lax.broadcasted_iota(jnp.int32, (HEAD_DIM, c.shape[1]), 0)
    third = jnp.where(sub >= 3, sub - 3, sub)
    pieces = jnp.where(third == 0, hi, jnp.where(third == 1, mid, lo))
    first, second = (1.0, pieces) if ones_first else (pieces, 1.0)
    return jnp.where(sub < 3, first, jnp.where(sub < 6, second, 0.0))


def _bias_cols(c):
    hi, mid, lo = _bias_pieces(c)
    lane = lax.broadcasted_iota(jnp.int32, (c.shape[0], LANES), 1)
    pieces = jnp.where(lane == 3, hi, jnp.where(lane == 4, mid, lo))
    return jnp.where(lane < 3, 1.0, jnp.where(lane < 6, pieces, 0.0))


def _fox_prompt_kernel(q_ref, cq_ref, k_hbm, v_hbm, ck_ref, ckp_ref, o_ref,
                       kaug_scr, vt_scr, sa_scr, sb_scr, acc_scr, kbuf, vbuf, sems):
    h = pl.program_id(1)
    qi = pl.program_id(2)
    nheads = pl.num_programs(1)
    nslices = pl.num_programs(0) * nheads
    lin = pl.program_id(0) * nheads + h
    slot = lin % 2
    tq = q_ref.shape[1]
    tk = sa_scr.shape[0]
    plen = ckp_ref.shape[1]
    t = kbuf.shape[1] - plen

    @pl.when(qi == 0)
    def _():
        _slice_prefetch(k_hbm, v_hbm, kbuf, vbuf, sems, lin, nslices, nheads)

        def prep(cix, carry):
            r0 = pl.multiple_of(cix * tk, tk)
            src = pl.ds(pl.multiple_of(plen + r0, 8), tk)
            ck = ck_ref[0, 0, :, pl.ds(r0, tk)] * LOG2E
            kaug_scr[pl.ds(r0, tk), HEAD_DIM:] = _bias_rows(-ck, True).T.astype(BF16)
            kaug_scr[pl.ds(r0, tk), :HEAD_DIM] = kbuf[slot, src, :].astype(BF16)
            vt_scr[:, pl.ds(r0, tk)] = vbuf[slot, src, :].T.astype(BF16)
            return carry
        lax.fori_loop(0, t // tk, prep, 0)

    qt = q_ref[0].T.astype(BF16)
    augq = _bias_rows(cq_ref[0, 0] * LOG2E, False).astype(BF16)
    qaug = jnp.concatenate([qt, augq], axis=0)

    def row0(blk):
        return pl.multiple_of(blk * tk, tk)

    def colmax(s):
        return jnp.max(s, axis=0, keepdims=True)

    def stage(blk, s_scr):
        s = jnp.dot(kaug_scr[pl.ds(row0(blk), tk), :], qaug, preferred_element_type=F32)
        s_scr[...] = s
        return colmax(s)

    def pv_block(blk):
        return lambda p: jnp.dot(vt_scr[:, pl.ds(row0(blk), tk)], p, preferred_element_type=F32)

    def absorb(ml, blocks):
        m, l = ml
        m_new = m
        for _, smax, _ in blocks:
            m_new = jnp.maximum(m_new, smax)
        alpha = jnp.exp2(m - m_new)
        l = alpha * l
        pv = None
        for s, _, pv_fn in blocks:
            p = jnp.exp2(s - m_new)
            l = l + jnp.sum(p, axis=0, keepdims=True)
            term = pv_fn(p.astype(BF16))
            pv = term if pv is None else pv + term
        acc_scr[...] = alpha * acc_scr[...] + pv
        return m_new, l

    acc_scr[...] = jnp.zeros_like(acc_scr)
    ml = (jnp.full((1, tq), NEG_BIG, F32), jnp.zeros((1, tq), F32))

    def body(pair, state):
        ma, ml = state
        mb = stage(2 * pair + 1, sb_scr)
        ml = absorb(ml, [(sa_scr[...], ma, pv_block(2 * pair))])
        ma = stage(2 * pair + 2, sa_scr)
        ml = absorb(ml, [(sb_scr[...], mb, pv_block(2 * pair + 1))])
        return ma, ml

    _, ml = lax.fori_loop(0, qi, body, (stage(0, sa_scr), ml))

    stage(2 * qi + 1, sb_scr)
    krow = lax.broadcasted_iota(jnp.int32, (tk, tq), 0)
    qcolumn = lax.broadcasted_iota(jnp.int32, (tk, tq), 1)
    s_a = jnp.where(krow <= qcolumn, sa_scr[...], -jnp.inf)
    s_b = jnp.where(krow + tk <= qcolumn, sb_scr[...], -jnp.inf)
    lane = lax.broadcasted_iota(jnp.int32, ckp_ref.shape[1:], 1)
    ckp = jnp.sum(jnp.where(lane == h, ckp_ref[0], 0.0), axis=-1, keepdims=True) * LOG2E
    kp = jnp.concatenate([kbuf[slot, :plen, :].astype(BF16), _bias_cols(-ckp).astype(BF16)], axis=1)
    vp = vbuf[slot, :plen, :].astype(BF16)
    s_p = jnp.dot(kp, qaug, preferred_element_type=F32)
    m, l = absorb(ml, [
        (s_p, colmax(s_p), lambda p: lax.dot_general(vp, p, (((0,), (0,)), ((), ())),
                                                     preferred_element_type=F32)),
        (s_a, colmax(s_a), pv_block(2 * qi)),
        (s_b, colmax(s_b), pv_block(2 * qi + 1)),
    ])
    o_ref[0] = (acc_scr[...] * (1.0 / l)).T.astype(o_ref.dtype)


def _fox_prompt(z, qtile, k_cache, v_cache, c_row, cp_col, *, tq):
    nb, nh, _, t = c_row.shape
    nq = t // tq
    plen = cp_col.shape[1]
    return pl.pallas_call(
        _fox_prompt_kernel,
        grid=(nb, nh, nq),
        in_specs=[
            pl.BlockSpec((1, tq, HEAD_DIM), lambda b, h, i: (qtile * nh + h, b * nq + i, 0)),
            pl.BlockSpec((1, 1, 1, tq), lambda b, h, i: (b, h, 0, i)),
            pl.BlockSpec(memory_space=pl.ANY),
            pl.BlockSpec(memory_space=pl.ANY),
            pl.BlockSpec((1, 1, 1, t), lambda b, h, i: (b, h, 0, 0)),
            pl.BlockSpec((1, plen, LANES), lambda b, h, i: (0, 0, 0)),
        ],
        out_specs=pl.BlockSpec((1, tq, HEAD_DIM), lambda b, h, i: (h, b * nq + i, 0)),
        out_shape=jax.ShapeDtypeStruct((nh, nb * t, HEAD_DIM), BF16),
        scratch_shapes=[pltpu.VMEM((t, 2 * HEAD_DIM), BF16), pltpu.VMEM((HEAD_DIM, t), BF16),
                        pltpu.VMEM((tq // 2, tq), F32), pltpu.VMEM((tq // 2, tq), F32),
                        pltpu.VMEM((HEAD_DIM, tq), F32),
                        pltpu.VMEM((2, plen + t, HEAD_DIM), F32),
                        pltpu.VMEM((2, plen + t, HEAD_DIM), F32),
                        pltpu.SemaphoreType.DMA((2, 2))],
        compiler_params=_params(("arbitrary", "arbitrary", "arbitrary")),
        name="fox_prompt",
    )(z, c_row, k_cache, v_cache, c_row, cp_col)


def _outproj_kernel(h_ref, a_ref, b_ref, wa_ref, wb_ref, g_ref, o_ref):
    def heads_on_lanes(ref):
        return jnp.concatenate([ref[hd] for hd in range(ref.shape[0])], axis=1)

    mix = (jnp.dot(heads_on_lanes(a_ref), wa_ref[...], preferred_element_type=F32)
           + jnp.dot(heads_on_lanes(b_ref), wb_ref[...], preferred_element_type=F32))
    o_ref[...] = h_ref[...] + _rms(mix, g_ref[...])


def _outproj(h, mix_a, mix_b, w_a, w_b, gain):
    rows, d = h.shape
    nh = mix_a.shape[0]
    ka, kb = w_a.shape[0], w_b.shape[0]
    tm = _pick(rows, 512)
    return pl.pallas_call(
        _outproj_kernel,
        grid=(rows // tm,),
        in_specs=[
            pl.BlockSpec((tm, d), lambda i: (i, 0)),
            pl.BlockSpec((nh, tm, HEAD_DIM), lambda i: (0, i, 0)),
            pl.BlockSpec((nh, tm, HEAD_DIM), lambda i: (0, i, 0)),
            pl.BlockSpec((ka, d), lambda i: (0, 0)),
            pl.BlockSpec((kb, d), lambda i: (0, 0)),
            pl.BlockSpec((1, d), lambda i: (0, 0)),
        ],
        out_specs=pl.BlockSpec((tm, d), lambda i: (i, 0)),
        out_shape=jax.ShapeDtypeStruct((rows, d), F32),
        compiler_params=_params(("parallel",)),
        name="outproj",
    )(h, mix_a, mix_b, w_a, w_b, gain)


def _mlp_kernel(h_ref, gpre_ref, wu_ref, wd_ref, gpost_ref, o_ref, a_scr, acc_scr):
    j = pl.program_id(1)

    @pl.when(j == 0)
    def _():
        a_scr[...] = _rms(h_ref[...], gpre_ref[...]).astype(BF16)
        acc_scr[...] = jnp.zeros_like(acc_scr)

    u = jnp.maximum(jnp.dot(a_scr[...], wu_ref[...], preferred_element_type=F32), 0.0)
    acc_scr[...] += jnp.dot((u * u).astype(BF16), wd_ref[...], preferred_element_type=F32)

    @pl.when(j == pl.num_programs(1) - 1)
    def _():
        o_ref[...] = h_ref[...] + _rms(acc_scr[...], gpost_ref[...])


def _mlp(h, g_pre, w_up, w_down, g_post):
    rows, d = h.shape
    dff = w_up.shape[1]
    tm = _pick(rows, 512)
    tf = _pick(dff, 1024)
    return pl.pallas_call(
        _mlp_kernel,
        grid=(rows // tm, dff // tf),
        in_specs=[
            pl.BlockSpec((tm, d), lambda i, j: (i, 0)),
            pl.BlockSpec((1, d), lambda i, j: (0, 0)),
            pl.BlockSpec((d, tf), lambda i, j: (0, j)),
            pl.BlockSpec((tf, d), lambda i, j: (j, 0)),
            pl.BlockSpec((1, d), lambda i, j: (0, 0)),
        ],
        out_specs=pl.BlockSpec((tm, d), lambda i, j: (i, 0)),
        out_shape=jax.ShapeDtypeStruct((rows, d), F32),
        scratch_shapes=[pltpu.VMEM((tm, d), BF16), pltpu.VMEM((tm, d), F32)],
        compiler_params=_params(("parallel", "arbitrary")),
        name="mlp",
    )(h, g_pre, w_up, w_down, g_post)


def _col_form(c_row):
    c_col = jnp.transpose(c_row, (0, 2, 1))
    return jnp.pad(c_col, ((0, 0), (0, 0), (0, LANES - c_col.shape[2])))


def kernel(x_prompt, x_sample, cache_fox_k, cache_fox_v, cache_fox_logf, state_hgrn, meta_tokens,
           g_mix_pre, w_in, hg_lb_logits, hg_norm_gain, fox_f_bias, w_out, g_mix_post, g_mlp_pre,
           w_up, w_down, g_mlp_post):
    bp, seq, d = x_prompt.shape
    bs, dseq, _ = x_sample.shape
    n_meta = meta_tokens.shape[0]
    past = cache_fox_k.shape[2]
    nh = cache_fox_k.shape[3]
    hw = nh * HEAD_DIM
    n_main = w_in.shape[2] - nh
    assert dseq == n_meta, "sample frames and meta tokens share the small-stream kernels"
    assert state_hgrn.shape[2] == nh and n_main == 7 * hw

    w_main = w_in[0, :, :n_main].astype(BF16)
    w_f = jnp.pad(w_in[0, :, n_main:], ((0, 0), (0, LANES - nh))).astype(BF16)
    f_bias = jnp.pad(fox_f_bias[0], (0, LANES - nh))[None, :]
    w_oa = w_out[0, :hw].astype(BF16)
    w_ob = w_out[0, hw:].astype(BF16)
    w_u = w_up[0].astype(BF16)
    w_d = w_down[0].astype(BF16)
    g_pre = g_mix_pre[0][None, :]
    g_post = g_mix_post[0][None, :]
    g_mpre = g_mlp_pre[0][None, :]
    g_mpost = g_mlp_post[0][None, :]
    hg_gain = hg_norm_gain[0].reshape(nh, 1, HEAD_DIM)
    qtile, ktile, vtile = 4, 5, 6

    ns = bs + 1
    xs = jnp.concatenate([meta_tokens, x_sample.reshape(bs * dseq, d)], axis=0)
    zs, lfs = _inproj(xs, g_pre, w_main, w_f, f_bias, hg_lb_logits)
    lfs3 = lfs.reshape(ns, dseq, LANES)

    def small_kv(tile):
        rows_first = jnp.transpose(zs[tile * nh:(tile + 1) * nh], (1, 0, 2))
        return rows_first.reshape(ns, dseq, nh, HEAD_DIM)

    k_small, v_small = small_kv(ktile), small_kv(vtile)

    c_past = _cumsum_time(cache_fox_logf[0], jnp.zeros((bs, nh), F32), nh)
    init_s = jnp.concatenate([jnp.zeros((1, nh), F32), c_past[:, :, past - 1]], axis=0)
    c_small = _cumsum_time(lfs3, init_s, nh)

    s0_small = jnp.concatenate([jnp.zeros((1,) + state_hgrn.shape[2:], F32), state_hgrn[0]], axis=0)
    s0_small_t = jnp.swapaxes(s0_small, -1, -2)
    mix_hg_s, sfin_s_t = _hgrn(zs, ns, hg_gain, s0_small_t, s0_shared=False,
                               chunk=dseq, tt=dseq, hps=nh)

    mix_fox_s = _fox(zs, 1, qtile, ktile, vtile, _col_form(c_small[1:]),
                     cache_fox_k[0], cache_fox_v[0], c_past[:, :, None, :], c_small[1:, :, None, :])

    xm = x_prompt.reshape(bp * seq, d)
    zm, lfm, k_cache, v_cache = _inproj_kv(xm, g_pre, w_main, w_f, f_bias, hg_lb_logits,
                                           k_small[0], v_small[0], nseq=bp, nz=ktile)
    lfm3 = lfm.reshape(bp, seq, LANES)
    c_meta = c_small[0:1]
    init_m = jnp.broadcast_to(c_meta[:, :, n_meta - 1], (bp, nh))
    c_main = _cumsum_time(lfm3, init_m, nh)

    mix_hg_m, sfin_m_t = _hgrn(zm, bp, hg_gain, sfin_s_t[0:1], s0_shared=True,
                               chunk=_pick(seq, 128), tt=_pick(seq, 2048), hps=1)
    mix_fox_m = _fox_prompt(zm, qtile, k_cache, v_cache, c_main[:, :, None, :], _col_form(c_meta),
                            tq=_pick(seq, 1024))

    h1m = _outproj(xm, mix_hg_m, mix_fox_m, w_oa, w_ob, g_post)
    y_prompt = _mlp(h1m, g_mpre, w_u, w_d, g_mpost).reshape(bp, seq, d)
    h1s = _outproj(x_sample.reshape(bs * dseq, d), mix_hg_s[:, n_meta:, :], mix_fox_s,
                   w_oa, w_ob, g_post)
    y_sample = _mlp(h1s, g_mpre, w_u, w_d, g_mpost).reshape(bs, dseq, d)

    meta_lf = jnp.broadcast_to(lfs3[0:1, :, :nh], (bp, n_meta, nh))
    lf_p = jnp.concatenate([meta_lf, lfm3[:, :, :nh]], axis=1)[None]
    return (y_prompt, y_sample,
            k_cache[None], v_cache[None],
            lf_p,
            jnp.swapaxes(sfin_m_t, -1, -2)[None],
            k_small[1:][None], v_small[1:][None],
            lfs3[1:, :, :nh][None],
            jnp.swapaxes(sfin_s_t[1:], -1, -2)[None])
```

```python
import functools

import jax
import jax.numpy as jnp
from jax import lax
from jax.experimental import pallas as pl
from jax.experimental.pallas import tpu as pltpu

F32 = jnp.float32
BF16 = jnp.bfloat16

EPS = 1e-6
HEAD_DIM = 128
LANES = 128
NEG_BIG = -1e30
LOG2E = 1.4426950408889634
VMEM_LIMIT = 56 * 1024 * 1024


def _pick(n, pref):
    if n <= pref:
        return n
    t = pref
    while n % t:
        t //= 2
    return t


def _params(sem):
    return pltpu.CompilerParams(dimension_semantics=sem, vmem_limit_bytes=VMEM_LIMIT)


def _log_sigmoid(x):
    return jnp.minimum(x, 0.0) - jnp.log1p(jnp.exp(-jnp.abs(x)))


def _rms(x, gain):
    ms = jnp.mean(x * x, axis=-1, keepdims=True)
    return x * lax.rsqrt(ms + EPS) * gain


def _silu(t):
    return t * jax.nn.sigmoid(t)


def _log2_forget(t, lbl_ref):
    lg = lbl_ref[...]
    e = jnp.exp(lg - jnp.max(lg, axis=0, keepdims=True))
    lb = e[0:1, :] / jnp.sum(e, axis=0, keepdims=True)
    return jnp.log(lb + (1.0 - lb) * jax.nn.sigmoid(t)) * LOG2E


def _store_heads(z_ref, tile):
    for hd in range(z_ref.shape[0]):
        z_ref[hd] = tile[:, hd * HEAD_DIM:(hd + 1) * HEAD_DIM]


def _store_z_tiles(j, project, lbl_ref, z_ref):
    @pl.when((j == 0) | (j == 3))
    def _():
        _store_heads(z_ref, _silu(project()))

    @pl.when(j == 1)
    def _():
        _store_heads(z_ref, _log2_forget(project(), lbl_ref))

    @pl.when((j == 2) | (j == 4))
    def _():
        _store_heads(z_ref, project() * jnp.where(j == 4, HEAD_DIM ** -0.5 * LOG2E, 1.0))


def _inproj_kernel(x_ref, g_ref, w_ref, wf_ref, fb_ref, lbl_ref, z_ref, lf_ref, a_scr):
    j = pl.program_id(1)

    @pl.when(j == 0)
    def _():
        a = _rms(x_ref[...], g_ref[...]).astype(BF16)
        a_scr[...] = a
        ff = jnp.dot(a, wf_ref[...], preferred_element_type=F32) + fb_ref[...]
        lf_ref[...] = _log_sigmoid(ff)

    def project():
        return jnp.dot(a_scr[...], w_ref[...], preferred_element_type=F32)

    _store_z_tiles(j, project, lbl_ref, z_ref)

    @pl.when(j >= 5)
    def _():
        _store_heads(z_ref, project())


def _inproj(x, gain, w_main, w_f, f_bias, lb_logits):
    rows, d = x.shape
    n = w_main.shape[1]
    tm = _pick(rows, 1024)
    tn = lb_logits.shape[1]
    nh = tn // HEAD_DIM
    return pl.pallas_call(
        _inproj_kernel,
        grid=(rows // tm, n // tn),
        in_specs=[
            pl.BlockSpec((tm, d), lambda i, j: (i, 0)),
            pl.BlockSpec((1, d), lambda i, j: (0, 0)),
            pl.BlockSpec((d, tn), lambda i, j: (0, j)),
            pl.BlockSpec((d, LANES), lambda i, j: (0, 0)),
            pl.BlockSpec((1, LANES), lambda i, j: (0, 0)),
            pl.BlockSpec(lb_logits.shape, lambda i, j: (0, 0)),
        ],
        out_specs=[
            pl.BlockSpec((nh, tm, HEAD_DIM), lambda i, j: (j, i, 0)),
            pl.BlockSpec((tm, LANES), lambda i, j: (i, 0)),
        ],
        out_shape=[
            jax.ShapeDtypeStruct((n // HEAD_DIM, rows, HEAD_DIM), F32),
            jax.ShapeDtypeStruct((rows, LANES), F32),
        ],
        scratch_shapes=[pltpu.VMEM((tm, d), BF16)],
        compiler_params=_params(("parallel", "arbitrary")),
        name="inproj",
    )(x, gain, w_main, w_f, f_bias, lb_logits)


def _inproj_kv_kernel(x_ref, g_ref, w_ref, wf_ref, fb_ref, lbl_ref, pk_ref, pv_ref,
                      z_ref, lf_ref, k_hbm, v_hbm, a_scr, stage, sems,
                      *, nz, tiles_per_seq):
    i = pl.program_id(0)
    j = pl.program_id(1)
    tm = x_ref.shape[0]
    plen = pk_ref.shape[0]
    last = pl.num_programs(0) - 1

    nh = k_hbm.shape[2]

    class _TileCopy:
        def __init__(self, slot, dst, tile):
            row = (tile % tiles_per_seq) * tm + plen
            self.copies = [
                pltpu.make_async_copy(stage.at[slot, :, pl.ds(hd * HEAD_DIM, HEAD_DIM)],
                                      dst.at[tile // tiles_per_seq, pl.ds(row, tm), hd, :],
                                      sems.at[slot])
                for hd in range(nh)]

        def start(self):
            for c in self.copies:
                c.start()

        def wait(self):
            for c in self.copies:
                c.wait()

    tile_copy = _TileCopy

    def prefix_copy(slot, src, dst, tile):
        return pltpu.make_async_copy(src, dst.at[tile // tiles_per_seq, pl.ds(0, plen), :, :],
                                     sems.at[2 + slot])

    def opens_sequence(tile):
        return tile % tiles_per_seq == 0

    @pl.when(j == 0)
    def _():
        a = _rms(x_ref[...], g_ref[...]).astype(BF16)
        a_scr[...] = a
        ff = jnp.dot(a, wf_ref[...], preferred_element_type=F32) + fb_ref[...]
        lf_ref[...] = _log_sigmoid(ff)

    def project():
        return jnp.dot(a_scr[...], w_ref[...], preferred_element_type=F32)

    _store_z_tiles(j, project, lbl_ref, z_ref)

    @pl.when(j == nz)
    def _():
        @pl.when(i > 0)
        def _():
            tile_copy(1, v_hbm, i - 1).wait()

            @pl.when(opens_sequence(i - 1))
            def _():
                prefix_copy(1, pv_ref, v_hbm, i - 1).wait()
        stage[0] = project()
        tile_copy(0, k_hbm, i).start()

        @pl.when(opens_sequence(i))
        def _():
            prefix_copy(0, pk_ref, k_hbm, i).start()

    @pl.when(j == nz + 1)
    def _():
        stage[1] = project()
        tile_copy(1, v_hbm, i).start()
        tile_copy(0, k_hbm, i).wait()

        @pl.when(opens_sequence(i))
        def _():
            prefix_copy(1, pv_ref, v_hbm, i).start()
            prefix_copy(0, pk_ref, k_hbm, i).wait()

        @pl.when(i == last)
        def _():
            tile_copy(1, v_hbm, i).wait()

            @pl.when(opens_sequence(i))
            def _():
                prefix_copy(1, pv_ref, v_hbm, i).wait()


def _inproj_kv(x, gain, w_main, w_f, f_bias, lb_logits, prefix_k, prefix_v, *, nseq, nz):
    rows, d = x.shape
    n = w_main.shape[1]
    tn = n // (nz + 2)
    seq = rows // nseq
    plen, nh, _ = prefix_k.shape
    tm = _pick(seq, 1024)
    kern = functools.partial(_inproj_kv_kernel, nz=nz, tiles_per_seq=seq // tm)
    cache = jax.ShapeDtypeStruct((nseq, plen + seq, nh, HEAD_DIM), F32)
    return pl.pallas_call(
        kern,
        grid=(rows // tm, nz + 2),
        in_specs=[
            pl.BlockSpec((tm, d), lambda i, j: (i, 0)),
            pl.BlockSpec((1, d), lambda i, j: (0, 0)),
            pl.BlockSpec((d, tn), lambda i, j: (0, j)),
            pl.BlockSpec((d, LANES), lambda i, j: (0, 0)),
            pl.BlockSpec((1, LANES), lambda i, j: (0, 0)),
            pl.BlockSpec(lb_logits.shape, lambda i, j: (0, 0)),
            pl.BlockSpec((plen, nh, HEAD_DIM), lambda i, j: (0, 0, 0)),
            pl.BlockSpec((plen, nh, HEAD_DIM), lambda i, j: (0, 0, 0)),
        ],
        out_specs=[
            pl.BlockSpec((nh, tm, HEAD_DIM), lambda i, j: (jnp.minimum(j, nz - 1), i, 0)),
            pl.BlockSpec((tm, LANES), lambda i, j: (i, 0)),
            pl.BlockSpec(memory_space=pl.ANY),
            pl.BlockSpec(memory_space=pl.ANY),
        ],
        out_shape=[
            jax.ShapeDtypeStruct((nz * nh, rows, HEAD_DIM), F32),
            jax.ShapeDtypeStruct((rows, LANES), F32),
            cache, cache,
        ],
        scratch_shapes=[pltpu.VMEM((tm, d), BF16), pltpu.VMEM((2, tm, tn), F32),
                        pltpu.SemaphoreType.DMA((4,))],
        compiler_params=_params(("arbitrary", "arbitrary")),
        name="inproj_kv",
    )(x, gain, w_main, w_f, f_bias, lb_logits, prefix_k, prefix_v)


def _split3(x):
    hi = x.astype(BF16)
    r = x - hi.astype(F32)
    mid = r.astype(BF16)
    lo = (r - mid.astype(F32)).astype(BF16)
    return hi, mid, lo


def _cumsum_kernel(x_ref, init_ref, c_ref, *, chain):
    g, nblk, blk = x_ref.shape
    nrow = g * nblk
    upper = (lax.broadcasted_iota(jnp.int32, (blk, blk), 0)
             <= lax.broadcasted_iota(jnp.int32, (blk, blk), 1)).astype(BF16)
    c = sum(jnp.dot(p, upper, preferred_element_type=F32)
            for p in _split3(x_ref[...].reshape(nrow, blk)))
    if chain:
        shift = nblk.bit_length() - 1
        ri = lax.broadcasted_iota(jnp.int32, (nrow, nrow), 0)
        ci = lax.broadcasted_iota(jnp.int32, (nrow, nrow), 1)
        earlier = jnp.where(ci < ri, 1.0, 0.0)
        before = jnp.where((ri >> shift) == (ci >> shift), earlier, 0.0).astype(BF16)
        tot = jnp.broadcast_to(c[:, blk - 1:blk], (nrow, blk))
        c = c + sum(jnp.dot(before, p, preferred_element_type=F32) for p in _split3(tot))
    init = jnp.broadcast_to(init_ref[...], (g, nblk, blk)).reshape(nrow, blk)
    c_ref[...] = (c + init).reshape(g, nblk, blk)


def _cumsum(x, init, *, chain):
    n, nblk, blk = x.shape
    assert not chain or nblk & (nblk - 1) == 0, "chained blocks per sequence must be a power of two"
    g = _pick(n, 32)
    ni = init.shape[1]
    return pl.pallas_call(
        functools.partial(_cumsum_kernel, chain=chain),
        grid=(n // g,),
        in_specs=[
            pl.BlockSpec((g, nblk, blk), lambda i: (i, 0, 0)),
            pl.BlockSpec((g, ni, blk), lambda i: (i, 0, 0)),
        ],
        out_specs=pl.BlockSpec((g, nblk, blk), lambda i: (i, 0, 0)),
        out_shape=jax.ShapeDtypeStruct((n, nblk, blk), F32),
        compiler_params=_params(("parallel",)),
        name="cumsum",
    )(x, init)


def _cumsum_time(lf, init, nh):
    nb, t = lf.shape[0], lf.shape[1]
    blk = _pick(t, 256)
    x = jnp.transpose(lf[:, :, :nh], (0, 2, 1))
    if t == blk:
        init3 = jnp.broadcast_to(init.reshape(1, nb * nh, 1), (1, nb * nh, blk))
        return _cumsum(x.reshape(1, nb * nh, blk), init3, chain=False).reshape(nb, nh, t)
    init3 = jnp.broadcast_to(init.reshape(nb * nh, 1, 1), (nb * nh, 1, blk))
    return _cumsum(x.reshape(nb * nh, t // blk, blk), init3, chain=True).reshape(nb, nh, t)


def _block_mid(pre, half):
    c = pre.shape[0]
    if half >= 8:
        x3 = pre.reshape(c // (2 * half), 2 * half, HEAD_DIM)
        return jnp.broadcast_to(x3[:, half - 1:half, :], x3.shape).reshape(c, HEAD_DIM)
    x3 = pre.reshape(c // 8, 8, HEAD_DIM)
    sub = lax.broadcasted_iota(jnp.int32, x3.shape, 1)
    mid = None
    for blk in range(8 // (2 * half)):
        src = 2 * half * blk + half - 1
        piece = jnp.broadcast_to(x3[:, src:src + 1, :], x3.shape)
        mid = piece if mid is None else jnp.where(sub >= 2 * half * blk, piece, mid)
    return mid.reshape(c, HEAD_DIM)


def _hgrn_masks(c):
    rows = lax.broadcasted_iota(jnp.int32, (c, HEAD_DIM), 0)
    ri = lax.broadcasted_iota(jnp.int32, (c, c), 0)
    ci = lax.broadcasted_iota(jnp.int32, (c, c), 1)
    levels = c.bit_length() - 1
    upper = [(rows & (1 << (lvl - 1))) != 0 for lvl in range(1, levels + 1)]
    owner = jnp.where(ri >= ci, 32 - lax.clz(ri ^ ci), -1)
    return owner, upper


def _hgrn_chunk(q, g2, hi, st, masks):
    c = q.shape[0]
    owner, upper_masks = masks
    k = 1.0 - jnp.exp2(g2)

    scores = jnp.where(owner == 0, jnp.sum(q * k, axis=-1, keepdims=True), 0.0)
    pre = g2
    for lvl, upper in enumerate(upper_masks, start=1):
        mid = _block_mid(pre, 1 << (lvl - 1))
        w = jnp.exp2(jnp.where(upper, pre, mid - pre))
        x = (jnp.where(upper, q, k) * w).astype(BF16)
        s_l = lax.dot_general(x, x, (((1,), (1,)), ((), ())), preferred_element_type=F32)
        scores = jnp.where(owner == lvl, s_l, scores)
        pre = jnp.where(upper, pre + mid, pre)
    cum = pre
    v = hi.astype(BF16)
    qe = (q * jnp.exp2(cum)).astype(BF16)
    o = (lax.dot_general(qe, st.astype(BF16), (((1,), (1,)), ((), ())), preferred_element_type=F32)
         + jnp.dot(scores.astype(BF16), v, preferred_element_type=F32))
    last = cum[c - 1:c, :]
    kd = (k * jnp.exp2(last - cum)).astype(BF16)
    st_new = st * jnp.exp2(last) + lax.dot_general(
        v, kd, (((0,), (0,)), ((), ())), preferred_element_type=F32)
    return o, st_new


def _hgrn_kernel(hq_ref, hf_ref, hi_ref, hg_ref, gain_ref, s0_ref,
                 o_ref, sfin_ref, st_scr, *, chunk):
    ti = pl.program_id(2)
    hps = st_scr.shape[0]

    @pl.when(ti == 0)
    def _():
        st_scr[...] = s0_ref[0]

    tt = hq_ref.shape[1]
    masks = _hgrn_masks(chunk)
    for hh in range(hps):
        gain = gain_ref[hh]
        st = st_scr[hh]
        for cidx in range(tt // chunk):
            sl = slice(cidx * chunk, (cidx + 1) * chunk)
            o, st = _hgrn_chunk(hq_ref[hh, sl, :], hf_ref[hh, sl, :], hi_ref[hh, sl, :], st, masks)
            o_ref[hh, sl, :] = (_rms(o, gain) * hg_ref[hh, sl, :]).astype(o_ref.dtype)
        st_scr[hh] = st

        @pl.when(ti == pl.num_programs(2) - 1)
        def _():
            sfin_ref[0, hh] = st


def _hgrn(z, nb, gain, s0t, *, s0_shared, chunk, tt, hps):
    rows = z.shape[1]
    t = rows // nb
    nh = gain.shape[0]
    ng = nh // hps
    nt = t // tt
    s0_map = (lambda b, h, i: (0, h, 0, 0)) if s0_shared else (lambda b, h, i: (b, h, 0, 0))

    def col(tile):
        return pl.BlockSpec((hps, tt, HEAD_DIM), lambda b, h, i: (tile * ng + h, b * nt + i, 0))

    return pl.pallas_call(
        functools.partial(_hgrn_kernel, chunk=chunk),
        grid=(nb, ng, nt),
        in_specs=[
            col(0), col(1), col(2), col(3),
            pl.BlockSpec((hps, 1, HEAD_DIM), lambda b, h, i: (h, 0, 0)),
            pl.BlockSpec((1, hps, HEAD_DIM, HEAD_DIM), s0_map),
        ],
        out_specs=[
            pl.BlockSpec((hps, tt, HEAD_DIM), lambda b, h, i: (h, b * nt + i, 0)),
            pl.BlockSpec((1, hps, HEAD_DIM, HEAD_DIM), lambda b, h, i: (b, h, 0, 0)),
        ],
        out_shape=[
            jax.ShapeDtypeStruct((nh, rows, HEAD_DIM), BF16),
            jax.ShapeDtypeStruct((nb, nh, HEAD_DIM, HEAD_DIM), F32),
        ],
        scratch_shapes=[pltpu.VMEM((hps, HEAD_DIM, HEAD_DIM), F32)],
        compiler_params=_params(("parallel", "parallel", "arbitrary")),
        name="hgrn",
    )(z, z, z, z, gain, s0t)


def _slice_prefetch(k_hbm, v_hbm, kbuf, vbuf, sems, lin, nslices, nheads):
    def copies(idx):
        seq_idx, head = idx // nheads, idx % nheads
        return (pltpu.make_async_copy(k_hbm.at[seq_idx, :, head, :], kbuf.at[idx % 2],
                                      sems.at[0, idx % 2]),
                pltpu.make_async_copy(v_hbm.at[seq_idx, :, head, :], vbuf.at[idx % 2],
                                      sems.at[1, idx % 2]))

    @pl.when(lin == 0)
    def _():
        for c in copies(lin):
            c.start()
    for c in copies(lin):
        c.wait()

    @pl.when(lin + 1 < nslices)
    def _():
        for c in copies(lin + 1):
            c.start()


def _fox_kernel(q_ref, cq_ref, kp_ref, vp_ref, ckp_ref, ks_ref, vs_ref, cks_ref, o_ref,
                *, pchunk):
    tq = q_ref.shape[1]
    nheads = ckp_ref.shape[1]
    plen = ckp_ref.shape[3]
    lane = lax.broadcasted_iota(jnp.int32, (tq, LANES), 1)
    causal = (lax.broadcasted_iota(jnp.int32, (tq, tq), 1)
              <= lax.broadcasted_iota(jnp.int32, (tq, tq), 0))
    starts = [pc * pchunk for pc in range(plen // pchunk)]
    for hd in range(nheads):
        q = q_ref[hd].astype(BF16)
        cq = jnp.sum(jnp.where(lane == hd, cq_ref[0], 0.0), axis=-1, keepdims=True)

        def head_rows(ref, start):
            return ref.at[0][pl.ds(start * nheads + hd, pchunk, stride=nheads), :]

        def logits(kb, ck):
            return lax.dot_general(q, kb.astype(BF16), (((1,), (1,)), ((), ())),
                                   preferred_element_type=F32) + (cq - ck) * LOG2E

        s_list = [logits(head_rows(kp_ref, st), ckp_ref[0, hd, :, st:st + pchunk]) for st in starts]
        s_list.append(jnp.where(causal, logits(ks_ref[hd], cks_ref[0, hd]), -jnp.inf))
        v_list = [head_rows(vp_ref, st) for st in starts] + [vs_ref[hd]]
        m = functools.reduce(jnp.maximum, [jnp.max(s, axis=-1, keepdims=True) for s in s_list])
        l = jnp.zeros((tq, 1), F32)
        acc = jnp.zeros((tq, HEAD_DIM), F32)
        for s, vb in zip(s_list, v_list):
            p = jnp.exp2(s - m)
            l = l + jnp.sum(p, axis=-1, keepdims=True)
            acc = acc + jnp.dot(p.astype(BF16), vb.astype(BF16), preferred_element_type=F32)
        o_ref[hd] = (acc / l).astype(o_ref.dtype)


def _fox(z, b_off, qtile, ktile, vtile, cq, kp, vp, ckp, cks):
    nb, t, _ = cq.shape
    nh = cks.shape[1]
    plen = kp.shape[1]
    kern = functools.partial(_fox_kernel, pchunk=_pick(plen, 512))
    return pl.pallas_call(
        kern,
        grid=(nb,),
        in_specs=[
            pl.BlockSpec((nh, t, HEAD_DIM), lambda b: (qtile, b + b_off, 0)),
            pl.BlockSpec((1, t, LANES), lambda b: (b, 0, 0)),
            pl.BlockSpec((1, plen * nh, HEAD_DIM), lambda b: (b, 0, 0)),
            pl.BlockSpec((1, plen * nh, HEAD_DIM), lambda b: (b, 0, 0)),
            pl.BlockSpec((1, nh, 1, plen), lambda b: (b, 0, 0, 0)),
            pl.BlockSpec((nh, t, HEAD_DIM), lambda b: (ktile, b + b_off, 0)),
            pl.BlockSpec((nh, t, HEAD_DIM), lambda b: (vtile, b + b_off, 0)),
            pl.BlockSpec((1, nh, 1, t), lambda b: (b, 0, 0, 0)),
        ],
        out_specs=pl.BlockSpec((nh, t, HEAD_DIM), lambda b: (0, b, 0)),
        out_shape=jax.ShapeDtypeStruct((nh, nb * t, HEAD_DIM), BF16),
        compiler_params=_params(("parallel",)),
        name="fox",
    )(z, cq, kp.reshape(nb, plen * nh, HEAD_DIM), vp.reshape(nb, plen * nh, HEAD_DIM),
      ckp, z, z, cks)


def _bias_pieces(c):
    hi = c.astype(BF16).astype(F32)
    r = c - hi
    mid = r.astype(BF16).astype(F32)
    return hi, mid, r - mid


def _bias_rows(c, ones_first):
    hi, mid, lo = _bias_pieces(c)
    sub = lax.broadcasted_iota(jnp.int32, (HEAD_DIM, c.shape[1]), 0)
    third = jnp.where(sub >= 3, sub - 3, sub)
    pieces = jnp.where(third == 0, hi, jnp.where(third == 1, mid, lo))
    first, second = (1.0, pieces) if ones_first else (pieces, 1.0)
    return jnp.where(sub < 3, first, jnp.where(sub < 6, second, 0.0))


def _bias_cols(c):
    hi, mid, lo = _bias_pieces(c)
    lane = lax.broadcasted_iota(jnp.int32, (c.shape[0], LANES), 1)
    pieces = jnp.where(lane == 3, hi, jnp.where(lane == 4, mid, lo))
    return jnp.where(lane < 3, 1.0, jnp.where(lane < 6, pieces, 0.0))


def _fox_prompt_kernel(q_ref, cq_ref, k_hbm, v_hbm, ck_ref, ckp_ref, o_ref,
                       kaug_scr, vt_scr, sa_scr, sb_scr, acc_scr, kbuf, vbuf, sems):
    h = pl.program_id(1)
    qi = pl.program_id(2)
    nheads = pl.num_programs(1)
    nslices = pl.num_programs(0) * nheads
    lin = pl.program_id(0) * nheads + h
    slot = lin % 2
    tq = q_ref.shape[1]
    tk = sa_scr.shape[0]
    plen = ckp_ref.shape[1]
    t = kbuf.shape[1] - plen

    @pl.when(qi == 0)
    def _():
        _slice_prefetch(k_hbm, v_hbm, kbuf, vbuf, sems, lin, nslices, nheads)

        def prep(cix, carry):
            r0 = pl.multiple_of(cix * tk, tk)
            src = pl.ds(pl.multiple_of(plen + r0, 8), tk)
            ck = ck_ref[0, 0, :, pl.ds(r0, tk)] * LOG2E
            kaug_scr[pl.ds(r0, tk), HEAD_DIM:] = _bias_rows(-ck, True).T.astype(BF16)
            kaug_scr[pl.ds(r0, tk), :HEAD_DIM] = kbuf[slot, src, :].astype(BF16)
            vt_scr[:, pl.ds(r0, tk)] = vbuf[slot, src, :].T.astype(BF16)
            return carry
        lax.fori_loop(0, t // tk, prep, 0)

    qt = q_ref[0].T.astype(BF16)
    augq = _bias_rows(cq_ref[0, 0] * LOG2E, False).astype(BF16)
    qaug = jnp.concatenate([qt, augq], axis=0)

    def row0(blk):
        return pl.multiple_of(blk * tk, tk)

    def colmax(s):
        return jnp.max(s, axis=0, keepdims=True)

    def stage(blk, s_scr):
        s = jnp.dot(kaug_scr[pl.ds(row0(blk), tk), :], qaug, preferred_element_type=F32)
        s_scr[...] = s
        return colmax(s)

    def pv_block(blk):
        return lambda p: jnp.dot(vt_scr[:, pl.ds(row0(blk), tk)], p, preferred_element_type=F32)

    def absorb(ml, blocks):
        m, l = ml
        m_new = m
        for _, smax, _ in blocks:
            m_new = jnp.maximum(m_new, smax)
        alpha = jnp.exp2(m - m_new)
        l = alpha * l
        pv = None
        for s, _, pv_fn in blocks:
            p = jnp.exp2(s - m_new)
            l = l + jnp.sum(p, axis=0, keepdims=True)
            term = pv_fn(p.astype(BF16))
            pv = term if pv is None else pv + term
        acc_scr[...] = alpha * acc_scr[...] + pv
        return m_new, l

    acc_scr[...] = jnp.zeros_like(acc_scr)
    ml = (jnp.full((1, tq), NEG_BIG, F32), jnp.zeros((1, tq), F32))

    def body(pair, state):
        ma, ml = state
        mb = stage(2 * pair + 1, sb_scr)
        ml = absorb(ml, [(sa_scr[...], ma, pv_block(2 * pair))])
        ma = stage(2 * pair + 2, sa_scr)
        ml = absorb(ml, [(sb_scr[...], mb, pv_block(2 * pair + 1))])
        return ma, ml

    _, ml = lax.fori_loop(0, qi, body, (stage(0, sa_scr), ml))

    stage(2 * qi + 1, sb_scr)
    krow = lax.broadcasted_iota(jnp.int32, (tk, tq), 0)
    qcolumn = lax.broadcasted_iota(jnp.int32, (tk, tq), 1)
    s_a = jnp.where(krow <= qcolumn, sa_scr[...], -jnp.inf)
    s_b = jnp.where(krow + tk <= qcolumn, sb_scr[...], -jnp.inf)
    lane = lax.broadcasted_iota(jnp.int32, ckp_ref.shape[1:], 1)
    ckp = jnp.sum(jnp.where(lane == h, ckp_ref[0], 0.0), axis=-1, keepdims=True) * LOG2E
    kp = jnp.concatenate([kbuf[slot, :plen, :].astype(BF16), _bias_cols(-ckp).astype(BF16)], axis=1)
    vp = vbuf[slot, :plen, :].astype(BF16)
    s_p = jnp.dot(kp, qaug, preferred_element_type=F32)
    m, l = absorb(ml, [
        (s_p, colmax(s_p), lambda p: lax.dot_general(vp, p, (((0,), (0,)), ((), ())),
                                                     preferred_element_type=F32)),
        (s_a, colmax(s_a), pv_block(2 * qi)),
        (s_b, colmax(s_b), pv_block(2 * qi + 1)),
    ])
    o_ref[0] = (acc_scr[...] * (1.0 / l)).T.astype(o_ref.dtype)


def _fox_prompt(z, qtile, k_cache, v_cache, c_row, cp_col, *, tq):
    nb, nh, _, t = c_row.shape
    nq = t // tq
    plen = cp_col.shape[1]
    return pl.pallas_call(
        _fox_prompt_kernel,
        grid=(nb, nh, nq),
        in_specs=[
            pl.BlockSpec((1, tq, HEAD_DIM), lambda b, h, i: (qtile * nh + h, b * nq + i, 0)),
            pl.BlockSpec((1, 1, 1, tq), lambda b, h, i: (b, h, 0, i)),
            pl.BlockSpec(memory_space=pl.ANY),
            pl.BlockSpec(memory_space=pl.ANY),
            pl.BlockSpec((1, 1, 1, t), lambda b, h, i: (b, h, 0, 0)),
            pl.BlockSpec((1, plen, LANES), lambda b, h, i: (0, 0, 0)),
        ],
        out_specs=pl.BlockSpec((1, tq, HEAD_DIM), lambda b, h, i: (h, b * nq + i, 0)),
        out_shape=jax.ShapeDtypeStruct((nh, nb * t, HEAD_DIM), BF16),
        scratch_shapes=[pltpu.VMEM((t, 2 * HEAD_DIM), BF16), pltpu.VMEM((HEAD_DIM, t), BF16),
                        pltpu.VMEM((tq // 2, tq), F32), pltpu.VMEM((tq // 2, tq), F32),
                        pltpu.VMEM((HEAD_DIM, tq), F32),
                        pltpu.VMEM((2, plen + t, HEAD_DIM), F32),
                        pltpu.VMEM((2, plen + t, HEAD_DIM), F32),
                        pltpu.SemaphoreType.DMA((2, 2))],
        compiler_params=_params(("arbitrary", "arbitrary", "arbitrary")),
        name="fox_prompt",
    )(z, c_row, k_cache, v_cache, c_row, cp_col)


def _row_halves(n):
    return (slice(0, n // 2), slice(n // 2, n)) if n % 16 == 0 else (slice(0, n),)


def _outproj_kernel(h_ref, a_ref, b_ref, wa_ref, wb_ref, g_ref, o_ref):
    def heads_on_lanes(ref, rows):
        return jnp.concatenate([ref[hd, rows, :] for hd in range(ref.shape[0])], axis=1)

    for rows in _row_halves(o_ref.shape[0]):
        mix = (jnp.dot(heads_on_lanes(a_ref, rows), wa_ref[...], preferred_element_type=F32)
               + jnp.dot(heads_on_lanes(b_ref, rows), wb_ref[...], preferred_element_type=F32))
        o_ref[rows, :] = h_ref[rows, :] + _rms(mix, g_ref[...])


def _outproj(h, mix_a, mix_b, w_a, w_b, gain):
    rows, d = h.shape
    nh = mix_a.shape[0]
    ka, kb = w_a.shape[0], w_b.shape[0]
    tm = _pick(rows, 512)
    return pl.pallas_call(
        _outproj_kernel,
        grid=(rows // tm,),
        in_specs=[
            pl.BlockSpec((tm, d), lambda i: (i, 0)),
            pl.BlockSpec((nh, tm, HEAD_DIM), lambda i: (0, i, 0)),
            pl.BlockSpec((nh, tm, HEAD_DIM), lambda i: (0, i, 0)),
            pl.BlockSpec((ka, d), lambda i: (0, 0)),
            pl.BlockSpec((kb, d), lambda i: (0, 0)),
            pl.BlockSpec((1, d), lambda i: (0, 0)),
        ],
        out_specs=pl.BlockSpec((tm, d), lambda i: (i, 0)),
        out_shape=jax.ShapeDtypeStruct((rows, d), F32),
        compiler_params=_params(("parallel",)),
        name="outproj",
    )(h, mix_a, mix_b, w_a, w_b, gain)


def _mlp_kernel(h_ref, gpre_ref, wu_ref, wd_ref, gpost_ref, o_ref, a_scr, acc_scr):
    j = pl.program_id(1)
    last = pl.num_programs(1) - 1
    halves = _row_halves(o_ref.shape[0])

    def up_down(rows):
        u = jnp.maximum(jnp.dot(a_scr[rows, :], wu_ref[...], preferred_element_type=F32), 0.0)
        return jnp.dot((u * u).astype(BF16), wd_ref[...], preferred_element_type=F32)

    @pl.when(j == 0)
    def _():
        for rows in halves:
            a_scr[rows, :] = _rms(h_ref[rows, :], gpre_ref[...]).astype(BF16)
            acc_scr[rows, :] = up_down(rows)

    @pl.when((j > 0) & (j < last))
    def _():
        acc_scr[...] += up_down(slice(None))

    @pl.when(j == last)
    def _():
        for rows in halves:
            acc = acc_scr[rows, :] + up_down(rows)
            o_ref[rows, :] = h_ref[rows, :] + _rms(acc, gpost_ref[...])


def _mlp(h, g_pre, w_up, w_down, g_post):
    rows, d = h.shape
    dff = w_up.shape[1]
    tm = _pick(rows, 512)
    tf = _pick(dff, 1024)
    assert dff // tf >= 2
    return pl.pallas_call(
        _mlp_kernel,
        grid=(rows // tm, dff // tf),
        in_specs=[
            pl.BlockSpec((tm, d), lambda i, j: (i, 0)),
            pl.BlockSpec((1, d), lambda i, j: (0, 0)),
            pl.BlockSpec((d, tf), lambda i, j: (0, j)),
            pl.BlockSpec((tf, d), lambda i, j: (j, 0)),
            pl.BlockSpec((1, d), lambda i, j: (0, 0)),
        ],
        out_specs=pl.BlockSpec((tm, d), lambda i, j: (i, 0)),
        out_shape=jax.ShapeDtypeStruct((rows, d), F32),
        scratch_shapes=[pltpu.VMEM((tm, d), BF16), pltpu.VMEM((tm, d), F32)],
        compiler_params=_params(("parallel", "arbitrary")),
        name="mlp",
    )(h, g_pre, w_up, w_down, g_post)


def _col_form(c_row):
    c_col = jnp.transpose(c_row, (0, 2, 1))
    return jnp.pad(c_col, ((0, 0), (0, 0), (0, LANES - c_col.shape[2])))


def kernel(x_prompt, x_sample, cache_fox_k, cache_fox_v, cache_fox_logf, state_hgrn, meta_tokens,
           g_mix_pre, w_in, hg_lb_logits, hg_norm_gain, fox_f_bias, w_out, g_mix_post, g_mlp_pre,
           w_up, w_down, g_mlp_post):
    bp, seq, d = x_prompt.shape
    bs, dseq, _ = x_sample.shape
    n_meta = meta_tokens.shape[0]
    past = cache_fox_k.shape[2]
    nh = cache_fox_k.shape[3]
    hw = nh * HEAD_DIM
    n_main = w_in.shape[2] - nh
    assert dseq == n_meta, "sample frames and meta tokens share the small-stream kernels"
    assert state_hgrn.shape[2] == nh and n_main == 7 * hw

    w_main = w_in[0, :, :n_main].astype(BF16)
    w_f = jnp.pad(w_in[0, :, n_main:], ((0, 0), (0, LANES - nh))).astype(BF16)
    f_bias = jnp.pad(fox_f_bias[0], (0, LANES - nh))[None, :]
    w_oa = w_out[0, :hw].astype(BF16)
    w_ob = w_out[0, hw:].astype(BF16)
    w_u = w_up[0].astype(BF16)
    w_d = w_down[0].astype(BF16)
    g_pre = g_mix_pre[0][None, :]
    g_post = g_mix_post[0][None, :]
    g_mpre = g_mlp_pre[0][None, :]
    g_mpost = g_mlp_post[0][None, :]
    hg_gain = hg_norm_gain[0].reshape(nh, 1, HEAD_DIM)
    qtile, ktile, vtile = 4, 5, 6

    ns = bs + 1
    xs = jnp.concatenate([meta_tokens, x_sample.reshape(bs * dseq, d)], axis=0)
    zs, lfs = _inproj(xs, g_pre, w_main, w_f, f_bias, hg_lb_logits)
    lfs3 = lfs.reshape(ns, dseq, LANES)

    def small_kv(tile):
        rows_first = jnp.transpose(zs[tile * nh:(tile + 1) * nh], (1, 0, 2))
        return rows_first.reshape(ns, dseq, nh, HEAD_DIM)

    k_small, v_small = small_kv(ktile), small_kv(vtile)

    c_past = _cumsum_time(cache_fox_logf[0], jnp.zeros((bs, nh), F32), nh)
    init_s = jnp.concatenate([jnp.zeros((1, nh), F32), c_past[:, :, past - 1]], axis=0)
    c_small = _cumsum_time(lfs3, init_s, nh)

    s0_small = jnp.concatenate([jnp.zeros((1,) + state_hgrn.shape[2:], F32), state_hgrn[0]], axis=0)
    s0_small_t = jnp.swapaxes(s0_small, -1, -2)
    mix_hg_s, sfin_s_t = _hgrn(zs, ns, hg_gain, s0_small_t, s0_shared=False,
                               chunk=dseq, tt=dseq, hps=nh)

    mix_fox_s = _fox(zs, 1, qtile, ktile, vtile, _col_form(c_small[1:]),
                     cache_fox_k[0], cache_fox_v[0], c_past[:, :, None, :], c_small[1:, :, None, :])

    xm = x_prompt.reshape(bp * seq, d)
    zm, lfm, k_cache, v_cache = _inproj_kv(xm, g_pre, w_main, w_f, f_bias, hg_lb_logits,
                                           k_small[0], v_small[0], nseq=bp, nz=ktile)
    lfm3 = lfm.reshape(bp, seq, LANES)
    c_meta = c_small[0:1]
    init_m = jnp.broadcast_to(c_meta[:, :, n_meta - 1], (bp, nh))
    c_main = _cumsum_time(lfm3, init_m, nh)

    mix_hg_m, sfin_m_t = _hgrn(zm, bp, hg_gain, sfin_s_t[0:1], s0_shared=True,
                               chunk=_pick(seq, 128), tt=_pick(seq, 2048), hps=1)
    mix_fox_m = _fox_prompt(zm, qtile, k_cache, v_cache, c_main[:, :, None, :], _col_form(c_meta),
                            tq=_pick(seq, 1024))

    h1m = _outproj(xm, mix_hg_m, mix_fox_m, w_oa, w_ob, g_post)
    y_prompt = _mlp(h1m, g_mpre, w_u, w_d, g_mpost).reshape(bp, seq, d)
    h1s = _outproj(x_sample.reshape(bs * dseq, d), mix_hg_s[:, n_meta:, :], mix_fox_s,
                   w_oa, w_ob, g_post)
    y_sample = _mlp(h1s, g_mpre, w_u, w_d, g_mpost).reshape(bs, dseq, d)

    meta_lf = jnp.broadcast_to(lfs3[0:1, :, :nh], (bp, n_meta, nh))
    lf_p = jnp.concatenate([meta_lf, lfm3[:, :, :nh]], axis=1)[None]
    return (y_prompt, y_sample,
            k_cache[None], v_cache[None],
            lf_p,
            jnp.swapaxes(sfin_m_t, -1, -2)[None],
            k_small[1:][None], v_small[1:][None],
            lfs3[1:, :, :nh][None],
            jnp.swapaxes(sfin_s_t[1:], -1, -2)[None])
```

```python
import functools

import jax
import jax.numpy as jnp
from jax import lax
from jax.experimental import pallas as pl
from jax.experimental.pallas import tpu as pltpu

F32 = jnp.float32
BF16 = jnp.bfloat16

EPS = 1e-6
HEAD_DIM = 128
LANES = 128
NEG_BIG = -1e30
LOG2E = 1.4426950408889634
VMEM_LIMIT = 56 * 1024 * 1024


def _pick(n, pref):
    if n <= pref:
        return n
    t = pref
    while n % t:
        t //= 2
    return t


def _params(sem):
    return pltpu.CompilerParams(dimension_semantics=sem, vmem_limit_bytes=VMEM_LIMIT)


def _log_sigmoid(x):
    return jnp.minimum(x, 0.0) - jnp.log1p(jnp.exp(-jnp.abs(x)))


def _rms(x, gain):
    ms = jnp.mean(x * x, axis=-1, keepdims=True)
    return x * lax.rsqrt(ms + EPS) * gain


def _silu(t):
    return t * jax.nn.sigmoid(t)


def _log2_forget(t, lbl_ref):
    lg = lbl_ref[...]
    e = jnp.exp(lg - jnp.max(lg, axis=0, keepdims=True))
    lb = e[0:1, :] / jnp.sum(e, axis=0, keepdims=True)
    return jnp.log(lb + (1.0 - lb) * jax.nn.sigmoid(t)) * LOG2E


def _store_heads(z_ref, tile, rows=slice(None)):
    for hd in range(z_ref.shape[0]):
        z_ref[hd, rows, :] = tile[:, hd * HEAD_DIM:(hd + 1) * HEAD_DIM]


def _row_halves(n):
    return (slice(0, n // 2), slice(n // 2, n)) if n % 16 == 0 else (slice(0, n),)


def _store_z_tiles(j, project, lbl_ref, z_ref, silu_tiles=(0, 3)):
    @pl.when(functools.reduce(jnp.logical_or, [j == t for t in silu_tiles]))
    def _():
        _store_heads(z_ref, _silu(project()))

    @pl.when(j == 1)
    def _():
        _store_heads(z_ref, _log2_forget(project(), lbl_ref))

    @pl.when((j == 2) | (j == 4))
    def _():
        _store_heads(z_ref, project() * jnp.where(j == 4, HEAD_DIM ** -0.5 * LOG2E, 1.0))


def _inproj_kernel(x_ref, g_ref, w_ref, wf_ref, fb_ref, lbl_ref, z_ref, lf_ref, a_scr):
    j = pl.program_id(1)

    @pl.when(j == 0)
    def _():
        a = _rms(x_ref[...], g_ref[...]).astype(BF16)
        a_scr[...] = a
        ff = jnp.dot(a, wf_ref[...], preferred_element_type=F32) + fb_ref[...]
        lf_ref[...] = _log_sigmoid(ff)

    def project():
        return jnp.dot(a_scr[...], w_ref[...], preferred_element_type=F32)

    _store_z_tiles(j, project, lbl_ref, z_ref)

    @pl.when(j >= 5)
    def _():
        _store_heads(z_ref, project())


def _inproj(x, gain, w_main, w_f, f_bias, lb_logits):
    rows, d = x.shape
    n = w_main.shape[1]
    tm = _pick(rows, 1024)
    tn = lb_logits.shape[1]
    nh = tn // HEAD_DIM
    return pl.pallas_call(
        _inproj_kernel,
        grid=(rows // tm, n // tn),
        in_specs=[
            pl.BlockSpec((tm, d), lambda i, j: (i, 0)),
            pl.BlockSpec((1, d), lambda i, j: (0, 0)),
            pl.BlockSpec((d, tn), lambda i, j: (0, j)),
            pl.BlockSpec((d, LANES), lambda i, j: (0, 0)),
            pl.BlockSpec((1, LANES), lambda i, j: (0, 0)),
            pl.BlockSpec(lb_logits.shape, lambda i, j: (0, 0)),
        ],
        out_specs=[
            pl.BlockSpec((nh, tm, HEAD_DIM), lambda i, j: (j, i, 0)),
            pl.BlockSpec((tm, LANES), lambda i, j: (i, 0)),
        ],
        out_shape=[
            jax.ShapeDtypeStruct((n // HEAD_DIM, rows, HEAD_DIM), F32),
            jax.ShapeDtypeStruct((rows, LANES), F32),
        ],
        scratch_shapes=[pltpu.VMEM((tm, d), BF16)],
        compiler_params=_params(("parallel", "arbitrary")),
        name="inproj",
    )(x, gain, w_main, w_f, f_bias, lb_logits)


def _inproj_kv_kernel(x_ref, g_ref, w_ref, wf_ref, fb_ref, lbl_ref, pk_ref, pv_ref,
                      z_ref, lf_ref, k_hbm, v_hbm, a_scr, stage, sems,
                      *, nz, tiles_per_seq):
    i = pl.program_id(0)
    j = pl.program_id(1)
    tm = x_ref.shape[0]
    plen = pk_ref.shape[0]
    last = pl.num_programs(0) - 1

    nh = k_hbm.shape[2]

    class _TileCopy:
        def __init__(self, slot, dst, tile):
            row = (tile % tiles_per_seq) * tm + plen
            self.copies = [
                pltpu.make_async_copy(stage.at[slot, :, pl.ds(hd * HEAD_DIM, HEAD_DIM)],
                                      dst.at[tile // tiles_per_seq, pl.ds(row, tm), hd, :],
                                      sems.at[slot])
                for hd in range(nh)]

        def start(self):
            for c in self.copies:
                c.start()

        def wait(self):
            for c in self.copies:
                c.wait()

    tile_copy = _TileCopy

    def prefix_copy(slot, src, dst, tile):
        return pltpu.make_async_copy(src, dst.at[tile // tiles_per_seq, pl.ds(0, plen), :, :],
                                     sems.at[2 + slot])

    def opens_sequence(tile):
        return tile % tiles_per_seq == 0

    @pl.when(j == 0)
    def _():
        for rows in _row_halves(tm):
            a = _rms(x_ref[rows, :], g_ref[...]).astype(BF16)
            a_scr[rows, :] = a
            ff = jnp.dot(a, wf_ref[...], preferred_element_type=F32) + fb_ref[...]
            lf_ref[rows, :] = _log_sigmoid(ff)
            _store_heads(z_ref, _silu(jnp.dot(a, w_ref[...], preferred_element_type=F32)), rows)

    def project():
        return jnp.dot(a_scr[...], w_ref[...], preferred_element_type=F32)

    _store_z_tiles(j, project, lbl_ref, z_ref, silu_tiles=(3,))

    @pl.when(j == nz)
    def _():
        @pl.when(i > 0)
        def _():
            tile_copy(1, v_hbm, i - 1).wait()

            @pl.when(opens_sequence(i - 1))
            def _():
                prefix_copy(1, pv_ref, v_hbm, i - 1).wait()
        stage[0] = project()
        tile_copy(0, k_hbm, i).start()

        @pl.when(opens_sequence(i))
        def _():
            prefix_copy(0, pk_ref, k_hbm, i).start()

    @pl.when(j == nz + 1)
    def _():
        stage[1] = project()
        tile_copy(1, v_hbm, i).start()
        tile_copy(0, k_hbm, i).wait()

        @pl.when(opens_sequence(i))
        def _():
            prefix_copy(1, pv_ref, v_hbm, i).start()
            prefix_copy(0, pk_ref, k_hbm, i).wait()

        @pl.when(i == last)
        def _():
            tile_copy(1, v_hbm, i).wait()

            @pl.when(opens_sequence(i))
            def _():
                prefix_copy(1, pv_ref, v_hbm, i).wait()


def _inproj_kv(x, gain, w_main, w_f, f_bias, lb_logits, prefix_k, prefix_v, *, nseq, nz):
    rows, d = x.shape
    n = w_main.shape[1]
    tn = n // (nz + 2)
    seq = rows // nseq
    plen, nh, _ = prefix_k.shape
    tm = _pick(seq, 1024)
    kern = functools.partial(_inproj_kv_kernel, nz=nz, tiles_per_seq=seq // tm)
    cache = jax.ShapeDtypeStruct((nseq, plen + seq, nh, HEAD_DIM), F32)
    return pl.pallas_call(
        kern,
        grid=(rows // tm, nz + 2),
        in_specs=[
            pl.BlockSpec((tm, d), lambda i, j: (i, 0)),
            pl.BlockSpec((1, d), lambda i, j: (0, 0)),
            pl.BlockSpec((d, tn), lambda i, j: (0, j)),
            pl.BlockSpec((d, LANES), lambda i, j: (0, 0)),
            pl.BlockSpec((1, LANES), lambda i, j: (0, 0)),
            pl.BlockSpec(lb_logits.shape, lambda i, j: (0, 0)),
            pl.BlockSpec((plen, nh, HEAD_DIM), lambda i, j: (0, 0, 0)),
            pl.BlockSpec((plen, nh, HEAD_DIM), lambda i, j: (0, 0, 0)),
        ],
        out_specs=[
            pl.BlockSpec((nh, tm, HEAD_DIM), lambda i, j: (jnp.minimum(j, nz - 1), i, 0)),
            pl.BlockSpec((tm, LANES), lambda i, j: (i, 0)),
            pl.BlockSpec(memory_space=pl.ANY),
            pl.BlockSpec(memory_space=pl.ANY),
        ],
        out_shape=[
            jax.ShapeDtypeStruct((nz * nh, rows, HEAD_DIM), F32),
            jax.ShapeDtypeStruct((rows, LANES), F32),
            cache, cache,
        ],
        scratch_shapes=[pltpu.VMEM((tm, d), BF16), pltpu.VMEM((2, tm, tn), F32),
                        pltpu.SemaphoreType.DMA((4,))],
        compiler_params=_params(("arbitrary", "arbitrary")),
        name="inproj_kv",
    )(x, gain, w_main, w_f, f_bias, lb_logits, prefix_k, prefix_v)


def _split3(x):
    hi = x.astype(BF16)
    r = x - hi.astype(F32)
    mid = r.astype(BF16)
    lo = (r - mid.astype(F32)).astype(BF16)
    return hi, mid, lo


def _cumsum_kernel(x_ref, init_ref, c_ref, *, chain):
    g, nblk, blk = x_ref.shape
    nrow = g * nblk
    upper = (lax.broadcasted_iota(jnp.int32, (blk, blk), 0)
             <= lax.broadcasted_iota(jnp.int32, (blk, blk), 1)).astype(BF16)
    c = sum(jnp.dot(p, upper, preferred_element_type=F32)
            for p in _split3(x_ref[...].reshape(nrow, blk)))
    if chain:
        shift = nblk.bit_length() - 1
        ri = lax.broadcasted_iota(jnp.int32, (nrow, nrow), 0)
        ci = lax.broadcasted_iota(jnp.int32, (nrow, nrow), 1)
        earlier = jnp.where(ci < ri, 1.0, 0.0)
        before = jnp.where((ri >> shift) == (ci >> shift), earlier, 0.0).astype(BF16)
        tot = jnp.broadcast_to(c[:, blk - 1:blk], (nrow, blk))
        c = c + sum(jnp.dot(before, p, preferred_element_type=F32) for p in _split3(tot))
    init = jnp.broadcast_to(init_ref[...], (g, nblk, blk)).reshape(nrow, blk)
    c_ref[...] = (c + init).reshape(g, nblk, blk)


def _cumsum(x, init, *, chain):
    n, nblk, blk = x.shape
    assert not chain or nblk & (nblk - 1) == 0, "chained blocks per sequence must be a power of two"
    g = _pick(n, 32)
    ni = init.shape[1]
    return pl.pallas_call(
        functools.partial(_cumsum_kernel, chain=chain),
        grid=(n // g,),
        in_specs=[
            pl.BlockSpec((g, nblk, blk), lambda i: (i, 0, 0)),
            pl.BlockSpec((g, ni, blk), lambda i: (i, 0, 0)),
        ],
        out_specs=pl.BlockSpec((g, nblk, blk), lambda i: (i, 0, 0)),
        out_shape=jax.ShapeDtypeStruct((n, nblk, blk), F32),
        compiler_params=_params(("parallel",)),
        name="cumsum",
    )(x, init)


def _cumsum_time(lf, init, nh):
    nb, t = lf.shape[0], lf.shape[1]
    blk = _pick(t, 256)
    x = jnp.transpose(lf[:, :, :nh], (0, 2, 1))
    if t == blk:
        init3 = jnp.broadcast_to(init.reshape(1, nb * nh, 1), (1, nb * nh, blk))
        return _cumsum(x.reshape(1, nb * nh, blk), init3, chain=False).reshape(nb, nh, t)
    init3 = jnp.broadcast_to(init.reshape(nb * nh, 1, 1), (nb * nh, 1, blk))
    return _cumsum(x.reshape(nb * nh, t // blk, blk), init3, chain=True).reshape(nb, nh, t)


def _block_mid(pre, half):
    c = pre.shape[0]
    if half >= 8:
        x3 = pre.reshape(c // (2 * half), 2 * half, HEAD_DIM)
        return jnp.broadcast_to(x3[:, half - 1:half, :], x3.shape).reshape(c, HEAD_DIM)
    x3 = pre.reshape(c // 8, 8, HEAD_DIM)
    sub = lax.broadcasted_iota(jnp.int32, x3.shape, 1)
    mid = None
    for blk in range(8 // (2 * half)):
        src = 2 * half * blk + half - 1
        piece = jnp.broadcast_to(x3[:, src:src + 1, :], x3.shape)
        mid = piece if mid is None else jnp.where(sub >= 2 * half * blk, piece, mid)
    return mid.reshape(c, HEAD_DIM)


def _hgrn_masks(c):
    rows = lax.broadcasted_iota(jnp.int32, (c, HEAD_DIM), 0)
    ri = lax.broadcasted_iota(jnp.int32, (c, c), 0)
    ci = lax.broadcasted_iota(jnp.int32, (c, c), 1)
    levels = c.bit_length() - 1
    upper = [(rows & (1 << (lvl - 1))) != 0 for lvl in range(1, levels + 1)]
    owner = jnp.where(ri >= ci, 32 - lax.clz(ri ^ ci), -1)
    return owner, upper


def _hgrn_chunk(q, g2, hi, st, masks):
    c = q.shape[0]
    owner, upper_masks = masks
    k = 1.0 - jnp.exp2(g2)

    scores = jnp.where(owner == 0, jnp.sum(q * k, axis=-1, keepdims=True), 0.0)
    pre = g2
    for lvl, upper in enumerate(upper_masks, start=1):
        mid = _block_mid(pre, 1 << (lvl - 1))
        w = jnp.exp2(jnp.where(upper, pre, mid - pre))
        x = (jnp.where(upper, q, k) * w).astype(BF16)
        s_l = lax.dot_general(x, x, (((1,), (1,)), ((), ())), preferred_element_type=F32)
        scores = jnp.where(owner == lvl, s_l, scores)
        pre = jnp.where(upper, pre + mid, pre)
    cum = pre
    v = hi.astype(BF16)
    qe = (q * jnp.exp2(cum)).astype(BF16)
    o = (lax.dot_general(qe, st.astype(BF16), (((1,), (1,)), ((), ())), preferred_element_type=F32)
         + jnp.dot(scores.astype(BF16), v, preferred_element_type=F32))
    last = cum[c - 1:c, :]
    kd = (k * jnp.exp2(last - cum)).astype(BF16)
    st_new = st * jnp.exp2(last) + lax.dot_general(
        v, kd, (((0,), (0,)), ((), ())), preferred_element_type=F32)
    return o, st_new


def _hgrn_kernel(hq_ref, hf_ref, hi_ref, hg_ref, gain_ref, s0_ref,
                 o_ref, sfin_ref, st_scr, *, chunk):
    ti = pl.program_id(2)
    hps = st_scr.shape[0]

    @pl.when(ti == 0)
    def _():
        st_scr[...] = s0_ref[0]

    tt = hq_ref.shape[1]
    masks = _hgrn_masks(chunk)
    for hh in range(hps):
        gain = gain_ref[hh]
        st = st_scr[hh]
        for cidx in range(tt // chunk):
            sl = slice(cidx * chunk, (cidx + 1) * chunk)
            o, st = _hgrn_chunk(hq_ref[hh, sl, :], hf_ref[hh, sl, :], hi_ref[hh, sl, :], st, masks)
            o_ref[hh, sl, :] = (_rms(o, gain) * hg_ref[hh, sl, :]).astype(o_ref.dtype)
        st_scr[hh] = st

        @pl.when(ti == pl.num_programs(2) - 1)
        def _():
            sfin_ref[0, hh] = st


def _hgrn(z, nb, gain, s0t, *, s0_shared, chunk, tt, hps):
    rows = z.shape[1]
    t = rows // nb
    nh = gain.shape[0]
    ng = nh // hps
    nt = t // tt
    s0_map = (lambda b, h, i: (0, h, 0, 0)) if s0_shared else (lambda b, h, i: (b, h, 0, 0))

    def col(tile):
        return pl.BlockSpec((hps, tt, HEAD_DIM), lambda b, h, i: (tile * ng + h, b * nt + i, 0))

    return pl.pallas_call(
        functools.partial(_hgrn_kernel, chunk=chunk),
        grid=(nb, ng, nt),
        in_specs=[
            col(0), col(1), col(2), col(3),
            pl.BlockSpec((hps, 1, HEAD_DIM), lambda b, h, i: (h, 0, 0)),
            pl.BlockSpec((1, hps, HEAD_DIM, HEAD_DIM), s0_map),
        ],
        out_specs=[
            pl.BlockSpec((hps, tt, HEAD_DIM), lambda b, h, i: (h, b * nt + i, 0)),
            pl.BlockSpec((1, hps, HEAD_DIM, HEAD_DIM), lambda b, h, i: (b, h, 0, 0)),
        ],
        out_shape=[
            jax.ShapeDtypeStruct((nh, rows, HEAD_DIM), BF16),
            jax.ShapeDtypeStruct((nb, nh, HEAD_DIM, HEAD_DIM), F32),
        ],
        scratch_shapes=[pltpu.VMEM((hps, HEAD_DIM, HEAD_DIM), F32)],
        compiler_params=_params(("parallel", "parallel", "arbitrary")),
        name="hgrn",
    )(z, z, z, z, gain, s0t)


def _slice_prefetch(k_hbm, v_hbm, kbuf, vbuf, sems, lin, nslices, nheads):
    def copies(idx):
        seq_idx, head = idx // nheads, idx % nheads
        return (pltpu.make_async_copy(k_hbm.at[seq_idx, :, head, :], kbuf.at[idx % 2],
                                      sems.at[0, idx % 2]),
                pltpu.make_async_copy(v_hbm.at[seq_idx, :, head, :], vbuf.at[idx % 2],
                                      sems.at[1, idx % 2]))

    @pl.when(lin == 0)
    def _():
        for c in copies(lin):
            c.start()
    for c in copies(lin):
        c.wait()

    @pl.when(lin + 1 < nslices)
    def _():
        for c in copies(lin + 1):
            c.start()


def _fox_kernel(q_ref, cq_ref, kp_ref, vp_ref, ckp_ref, ks_ref, vs_ref, cks_ref, o_ref,
                *, pchunk):
    tq = q_ref.shape[1]
    nheads = ckp_ref.shape[1]
    plen = ckp_ref.shape[3]
    lane = lax.broadcasted_iota(jnp.int32, (tq, LANES), 1)
    causal = (lax.broadcasted_iota(jnp.int32, (tq, tq), 1)
              <= lax.broadcasted_iota(jnp.int32, (tq, tq), 0))
    starts = [pc * pchunk for pc in range(plen // pchunk)]
    for hd in range(nheads):
        q = q_ref[hd].astype(BF16)
        cq = jnp.sum(jnp.where(lane == hd, cq_ref[0], 0.0), axis=-1, keepdims=True)

        def head_rows(ref, start):
            return ref.at[0][pl.ds(start * nheads + hd, pchunk, stride=nheads), :]

        def logits(kb, ck):
            return lax.dot_general(q, kb.astype(BF16), (((1,), (1,)), ((), ())),
                                   preferred_element_type=F32) + (cq - ck) * LOG2E

        s_list = [logits(head_rows(kp_ref, st), ckp_ref[0, hd, :, st:st + pchunk]) for st in starts]
        s_list.append(jnp.where(causal, logits(ks_ref[hd], cks_ref[0, hd]), -jnp.inf))
        v_list = [head_rows(vp_ref, st) for st in starts] + [vs_ref[hd]]
        m = functools.reduce(jnp.maximum, [jnp.max(s, axis=-1, keepdims=True) for s in s_list])
        l = jnp.zeros((tq, 1), F32)
        acc = jnp.zeros((tq, HEAD_DIM), F32)
        for s, vb in zip(s_list, v_list):
            p = jnp.exp2(s - m)
            l = l + jnp.sum(p, axis=-1, keepdims=True)
            acc = acc + jnp.dot(p.astype(BF16), vb.astype(BF16), preferred_element_type=F32)
        o_ref[hd] = (acc / l).astype(o_ref.dtype)


def _fox(z, b_off, qtile, ktile, vtile, cq, kp, vp, ckp, cks):
    nb, t, _ = cq.shape
    nh = cks.shape[1]
    plen = kp.shape[1]
    kern = functools.partial(_fox_kernel, pchunk=_pick(plen, 512))
    return pl.pallas_call(
        kern,
        grid=(nb,),
        in_specs=[
            pl.BlockSpec((nh, t, HEAD_DIM), lambda b: (qtile, b + b_off, 0)),
            pl.BlockSpec((1, t, LANES), lambda b: (b, 0, 0)),
            pl.BlockSpec((1, plen * nh, HEAD_DIM), lambda b: (b, 0, 0)),
            pl.BlockSpec((1, plen * nh, HEAD_DIM), lambda b: (b, 0, 0)),
            pl.BlockSpec((1, nh, 1, plen), lambda b: (b, 0, 0, 0)),
            pl.BlockSpec((nh, t, HEAD_DIM), lambda b: (ktile, b + b_off, 0)),
            pl.BlockSpec((nh, t, HEAD_DIM), lambda b: (vtile, b + b_off, 0)),
            pl.BlockSpec((1, nh, 1, t), lambda b: (b, 0, 0, 0)),
        ],
        out_specs=pl.BlockSpec((nh, t, HEAD_DIM), lambda b: (0, b, 0)),
        out_shape=jax.ShapeDtypeStruct((nh, nb * t, HEAD_DIM), BF16),
        compiler_params=_params(("parallel",)),
        name="fox",
    )(z, cq, kp.reshape(nb, plen * nh, HEAD_DIM), vp.reshape(nb, plen * nh, HEAD_DIM),
      ckp, z, z, cks)


def _bias_pieces(c):
    hi = c.astype(BF16).astype(F32)
    r = c - hi
    mid = r.astype(BF16).astype(F32)
    return hi, mid, r - mid


def _bias_rows(c, ones_first):
    hi, mid, lo = _bias_pieces(c)
    sub = lax.broadcasted_iota(jnp.int32, (HEAD_DIM, c.shape[1]), 0)
    third = jnp.where(sub >= 3, sub - 3, sub)
    pieces = jnp.where(third == 0, hi, jnp.where(third == 1, mid, lo))
    first, second = (1.0, pieces) if ones_first else (pieces, 1.0)
    return jnp.where(sub < 3, first, jnp.where(sub < 6, second, 0.0))


def _bias_cols(c):
    hi, mid, lo = _bias_pieces(c)
    lane = lax.broadcasted_iota(jnp.int32, (c.shape[0], LANES), 1)
    pieces = jnp.where(lane == 3, hi, jnp.where(lane == 4, mid, lo))
    return jnp.where(lane < 3, 1.0, jnp.where(lane < 6, pieces, 0.0))


def _fox_prompt_kernel(q_ref, cq_ref, k_hbm, v_hbm, ck_ref, ckp_ref, o_ref,
                       kaug_scr, vt_scr, sa_scr, sb_scr, acc_scr, state_scr, kbuf, vbuf, sems):
    assert q_ref.shape[1] == 2 * sa_scr.shape[0]
    h = pl.program_id(1)
    qi = pl.program_id(2)
    nheads = pl.num_programs(1)
    nslices = pl.num_programs(0) * nheads
    lin = pl.program_id(0) * nheads + h
    slot = lin % 2
    tq = q_ref.shape[1]
    tk = sa_scr.shape[0]
    plen = ckp_ref.shape[1]
    t = kbuf.shape[1] - plen

    @pl.when(qi == 0)
    def _():
        _slice_prefetch(k_hbm, v_hbm, kbuf, vbuf, sems, lin, nslices, nheads)

        def prep(cix, carry):
            r0 = pl.multiple_of(cix * tk, tk)
            src = pl.ds(pl.multiple_of(plen + r0, 8), tk)
            ck = ck_ref[0, 0, :, pl.ds(r0, tk)] * LOG2E
            kaug_scr[pl.ds(r0, tk), HEAD_DIM:] = _bias_rows(-ck, True).T.astype(BF16)
            kaug_scr[pl.ds(r0, tk), :HEAD_DIM] = kbuf[slot, src, :].astype(BF16)
            vt_scr[:, pl.ds(r0, tk)] = vbuf[slot, src, :].T.astype(BF16)
            return carry
        lax.fori_loop(0, t // tk, prep, 0)

    qt = q_ref[0].T.astype(BF16)
    augq = _bias_rows(cq_ref[0, 0] * LOG2E, False).astype(BF16)
    qaug = jnp.concatenate([qt, augq], axis=0)

    def row0(blk):
        return pl.multiple_of(blk * tk, tk)

    def colmax(s):
        return jnp.max(s, axis=0, keepdims=True)

    def stage(blk, s_scr):
        s = jnp.dot(kaug_scr[pl.ds(row0(blk), tk), :], qaug, preferred_element_type=F32)
        s_scr[...] = s
        return colmax(s)

    def pv_block(blk):
        return lambda p: jnp.dot(vt_scr[:, pl.ds(row0(blk), tk)], p, preferred_element_type=F32)

    def absorb(ml, blocks, cols=None):
        m, l = ml
        m_new = m
        for _, smax, _ in blocks:
            m_new = jnp.maximum(m_new, smax)
        alpha = jnp.exp2(m - m_new)
        l = alpha * l
        pv = None
        for s, _, pv_fn in blocks:
            p = jnp.exp2(s - m_new)
            l = l + jnp.sum(p, axis=0, keepdims=True)
            term = pv_fn(p.astype(BF16))
            pv = term if pv is None else pv + term
        where = (Ellipsis,) if cols is None else (slice(None), cols)
        acc_scr[where] = alpha * acc_scr[where] + pv
        return m_new, l

    acc_scr[...] = jnp.zeros_like(acc_scr)
    ml = (jnp.full((1, tq), NEG_BIG, F32), jnp.zeros((1, tq), F32))

    def body(pair, state):
        ma, ml = state
        mb = stage(2 * pair + 1, sb_scr)
        ml = absorb(ml, [(sa_scr[...], ma, pv_block(2 * pair))])
        ma = stage(2 * pair + 2, sa_scr)
        ml = absorb(ml, [(sb_scr[...], mb, pv_block(2 * pair + 1))])
        return ma, ml

    def two_pairs(i, state):
        return body(2 * i + 1, body(2 * i, state))

    ma, (m_run, l_run) = lax.fori_loop(0, qi // 2, two_pairs, (stage(0, sa_scr), ml))
    state_scr[0:1, :], state_scr[1:2, :], state_scr[2:3, :] = ma, m_run, l_run

    @pl.when(qi % 2 == 1)
    def _():
        ma, (m_run, l_run) = body(qi - 1, (state_scr[0:1, :], (state_scr[1:2, :], state_scr[2:3, :])))
        state_scr[1:2, :], state_scr[2:3, :] = m_run, l_run

    ml = (state_scr[1:2, :], state_scr[2:3, :])

    lane = lax.broadcasted_iota(jnp.int32, ckp_ref.shape[1:], 1)
    ckp = jnp.sum(jnp.where(lane == h, ckp_ref[0], 0.0), axis=-1, keepdims=True) * LOG2E
    kp = jnp.concatenate([kbuf[slot, :plen, :].astype(BF16), _bias_cols(-ckp).astype(BF16)], axis=1)
    vp = vbuf[slot, :plen, :].astype(BF16)

    def pv_prefix(p):
        return lax.dot_general(vp, p, (((0,), (0,)), ((), ())), preferred_element_type=F32)

    first, second = slice(0, tk), slice(tk, tq)
    tri = (lax.broadcasted_iota(jnp.int32, (tk, tk), 0)
           <= lax.broadcasted_iota(jnp.int32, (tk, tk), 1))
    s_p = jnp.dot(kp, qaug, preferred_element_type=F32)
    s_a = sa_scr[...]
    stage(2 * qi + 1, sb_scr)
    s_b = sb_scr[:, second]
    m_run, l_run = ml
    l_halves = []
    for cols, logit_blocks in (
            (first, [(s_p[:, first], pv_prefix),
                     (jnp.where(tri, s_a[:, first], -jnp.inf), pv_block(2 * qi))]),
            (second, [(s_p[:, second], pv_prefix),
                      (s_a[:, second], pv_block(2 * qi)),
                      (jnp.where(tri, s_b, -jnp.inf), pv_block(2 * qi + 1))])):
        blocks = [(s, colmax(s), pv_fn) for s, pv_fn in logit_blocks]
        _, l_half = absorb((m_run[:, cols], l_run[:, cols]), blocks, cols)
        l_halves.append(l_half)
    l = jnp.concatenate(l_halves, axis=1)
    o_ref[0] = (acc_scr[...] * (1.0 / l)).T.astype(o_ref.dtype)


def _fox_prompt(z, qtile, k_cache, v_cache, c_row, cp_col, *, tq):
    nb, nh, _, t = c_row.shape
    nq = t // tq
    plen = cp_col.shape[1]
    return pl.pallas_call(
        _fox_prompt_kernel,
        grid=(nb, nh, nq),
        in_specs=[
            pl.BlockSpec((1, tq, HEAD_DIM), lambda b, h, i: (qtile * nh + h, b * nq + i, 0)),
            pl.BlockSpec((1, 1, 1, tq), lambda b, h, i: (b, h, 0, i)),
            pl.BlockSpec(memory_space=pl.ANY),
            pl.BlockSpec(memory_space=pl.ANY),
            pl.BlockSpec((1, 1, 1, t), lambda b, h, i: (b, h, 0, 0)),
            pl.BlockSpec((1, plen, LANES), lambda b, h, i: (0, 0, 0)),
        ],
        out_specs=pl.BlockSpec((1, tq, HEAD_DIM), lambda b, h, i: (h, b * nq + i, 0)),
        out_shape=jax.ShapeDtypeStruct((nh, nb * t, HEAD_DIM), BF16),
        scratch_shapes=[pltpu.VMEM((t, 2 * HEAD_DIM), BF16), pltpu.VMEM((HEAD_DIM, t), BF16),
                        pltpu.VMEM((tq // 2, tq), F32), pltpu.VMEM((tq // 2, tq), F32),
                        pltpu.VMEM((HEAD_DIM, tq), F32), pltpu.VMEM((8, tq), F32),
                        pltpu.VMEM((2, plen + t, HEAD_DIM), F32),
                        pltpu.VMEM((2, plen + t, HEAD_DIM), F32),
                        pltpu.SemaphoreType.DMA((2, 2))],
        compiler_params=_params(("arbitrary", "arbitrary", "arbitrary")),
        name="fox_prompt",
    )(z, c_row, k_cache, v_cache, c_row, cp_col)


def _outproj_kernel(h_ref, a_ref, b_ref, wa_ref, wb_ref, g_ref, o_ref):
    def heads_on_lanes(ref, rows):
        return jnp.concatenate([ref[hd, rows, :] for hd in range(ref.shape[0])], axis=1)

    for rows in _row_halves(o_ref.shape[0]):
        mix = (jnp.dot(heads_on_lanes(a_ref, rows), wa_ref[...], preferred_element_type=F32)
               + jnp.dot(heads_on_lanes(b_ref, rows), wb_ref[...], preferred_element_type=F32))
        o_ref[rows, :] = h_ref[rows, :] + _rms(mix, g_ref[...])


def _outproj(h, mix_a, mix_b, w_a, w_b, gain):
    rows, d = h.shape
    nh = mix_a.shape[0]
    ka, kb = w_a.shape[0], w_b.shape[0]
    tm = _pick(rows, 512)
    return pl.pallas_call(
        _outproj_kernel,
        grid=(rows // tm,),
        in_specs=[
            pl.BlockSpec((tm, d), lambda i: (i, 0)),
            pl.BlockSpec((nh, tm, HEAD_DIM), lambda i: (0, i, 0)),
            pl.BlockSpec((nh, tm, HEAD_DIM), lambda i: (0, i, 0)),
            pl.BlockSpec((ka, d), lambda i: (0, 0)),
            pl.BlockSpec((kb, d), lambda i: (0, 0)),
            pl.BlockSpec((1, d), lambda i: (0, 0)),
        ],
        out_specs=pl.BlockSpec((tm, d), lambda i: (i, 0)),
        out_shape=jax.ShapeDtypeStruct((rows, d), F32),
        compiler_params=_params(("parallel",)),
        name="outproj",
    )(h, mix_a, mix_b, w_a, w_b, gain)


def _mlp_kernel(h_ref, gpre_ref, wu_ref, wd_ref, gpost_ref, o_ref, a_scr, acc_scr):
    j = pl.program_id(1)
    last = pl.num_programs(1) - 1
    halves = _row_halves(o_ref.shape[0])

    def up_down(rows):
        u = jnp.maximum(jnp.dot(a_scr[rows, :], wu_ref[...], preferred_element_type=F32), 0.0)
        return jnp.dot((u * u).astype(BF16), wd_ref[...], preferred_element_type=F32)

    @pl.when(j == 0)
    def _():
        for rows in halves:
            a_scr[rows, :] = _rms(h_ref[rows, :], gpre_ref[...]).astype(BF16)
            acc_scr[rows, :] = up_down(rows)

    @pl.when((j > 0) & (j < last))
    def _():
        acc_scr[...] += up_down(slice(None))

    @pl.when(j == last)
    def _():
        for rows in halves:
            acc = acc_scr[rows, :] + up_down(rows)
            o_ref[rows, :] = h_ref[rows, :] + _rms(acc, gpost_ref[...])


def _mlp(h, g_pre, w_up, w_down, g_post):
    rows, d = h.shape
    dff = w_up.shape[1]
    tm = _pick(rows, 512)
    tf = _pick(dff, 1024)
    assert dff // tf >= 2
    return pl.pallas_call(
        _mlp_kernel,
        grid=(rows // tm, dff // tf),
        in_specs=[
            pl.BlockSpec((tm, d), lambda i, j: (i, 0)),
            pl.BlockSpec((1, d), lambda i, j: (0, 0)),
            pl.BlockSpec((d, tf), lambda i, j: (0, j)),
            pl.BlockSpec((tf, d), lambda i, j: (j, 0)),
            pl.BlockSpec((1, d), lambda i, j: (0, 0)),
        ],
        out_specs=pl.BlockSpec((tm, d), lambda i, j: (i, 0)),
        out_shape=jax.ShapeDtypeStruct((rows, d), F32),
        scratch_shapes=[pltpu.VMEM((tm, d), BF16), pltpu.VMEM((tm, d), F32)],
        compiler_params=_params(("parallel", "arbitrary")),
        name="mlp",
    )(h, g_pre, w_up, w_down, g_post)


def _col_form(c_row):
    c_col = jnp.transpose(c_row, (0, 2, 1))
    return jnp.pad(c_col, ((0, 0), (0, 0), (0, LANES - c_col.shape[2])))


def kernel(x_prompt, x_sample, cache_fox_k, cache_fox_v, cache_fox_logf, state_hgrn, meta_tokens,
           g_mix_pre, w_in, hg_lb_logits, hg_norm_gain, fox_f_bias, w_out, g_mix_post, g_mlp_pre,
           w_up, w_down, g_mlp_post):
    bp, seq, d = x_prompt.shape
    bs, dseq, _ = x_sample.shape
    n_meta = meta_tokens.shape[0]
    past = cache_fox_k.shape[2]
    nh = cache_fox_k.shape[3]
    hw = nh * HEAD_DIM
    n_main = w_in.shape[2] - nh
    assert dseq == n_meta, "sample frames and meta tokens share the small-stream kernels"
    assert state_hgrn.shape[2] == nh and n_main == 7 * hw

    w_main = w_in[0, :, :n_main].astype(BF16)
    w_f = jnp.pad(w_in[0, :, n_main:], ((0, 0), (0, LANES - nh))).astype(BF16)
    f_bias = jnp.pad(fox_f_bias[0], (0, LANES - nh))[None, :]
    w_oa = w_out[0, :hw].astype(BF16)
    w_ob = w_out[0, hw:].astype(BF16)
    w_u = w_up[0].astype(BF16)
    w_d = w_down[0].astype(BF16)
    g_pre = g_mix_pre[0][None, :]
    g_post = g_mix_post[0][None, :]
    g_mpre = g_mlp_pre[0][None, :]
    g_mpost = g_mlp_post[0][None, :]
    hg_gain = hg_norm_gain[0].reshape(nh, 1, HEAD_DIM)
    qtile, ktile, vtile = 4, 5, 6

    ns = bs + 1
    xs = jnp.concatenate([meta_tokens, x_sample.reshape(bs * dseq, d)], axis=0)
    zs, lfs = _inproj(xs, g_pre, w_main, w_f, f_bias, hg_lb_logits)
    lfs3 = lfs.reshape(ns, dseq, LANES)

    def small_kv(tile):
        rows_first = jnp.transpose(zs[tile * nh:(tile + 1) * nh], (1, 0, 2))
        return rows_first.reshape(ns, dseq, nh, HEAD_DIM)

    k_small, v_small = small_kv(ktile), small_kv(vtile)

    c_past = _cumsum_time(cache_fox_logf[0], jnp.zeros((bs, nh), F32), nh)
    init_s = jnp.concatenate([jnp.zeros((1, nh), F32), c_past[:, :, past - 1]], axis=0)
    c_small = _cumsum_time(lfs3, init_s, nh)

    s0_small = jnp.concatenate([jnp.zeros((1,) + state_hgrn.shape[2:], F32), state_hgrn[0]], axis=0)
    s0_small_t = jnp.swapaxes(s0_small, -1, -2)
    mix_hg_s, sfin_s_t = _hgrn(zs, ns, hg_gain, s0_small_t, s0_shared=False,
                               chunk=dseq, tt=dseq, hps=nh)

    mix_fox_s = _fox(zs, 1, qtile, ktile, vtile, _col_form(c_small[1:]),
                     cache_fox_k[0], cache_fox_v[0], c_past[:, :, None, :], c_small[1:, :, None, :])

    xm = x_prompt.reshape(bp * seq, d)
    zm, lfm, k_cache, v_cache = _inproj_kv(xm, g_pre, w_main, w_f, f_bias, hg_lb_logits,
                                           k_small[0], v_small[0], nseq=bp, nz=ktile)
    lfm3 = lfm.reshape(bp, seq, LANES)
    c_meta = c_small[0:1]
    init_m = jnp.broadcast_to(c_meta[:, :, n_meta - 1], (bp, nh))
    c_main = _cumsum_time(lfm3, init_m, nh)

    mix_hg_m, sfin_m_t = _hgrn(zm, bp, hg_gain, sfin_s_t[0:1], s0_shared=True,
                               chunk=_pick(seq, 128), tt=_pick(seq, 2048), hps=1)
    mix_fox_m = _fox_prompt(zm, qtile, k_cache, v_cache, c_main[:, :, None, :], _col_form(c_meta),
                            tq=_pick(seq, 1024))

    h1m = _outproj(xm, mix_hg_m, mix_fox_m, w_oa, w_ob, g_post)
    y_prompt = _mlp(h1m, g_mpre, w_u, w_d, g_mpost).reshape(bp, seq, d)
    h1s = _outproj(x_sample.reshape(bs * dseq, d), mix_hg_s[:, n_meta:, :], mix_fox_s,
                   w_oa, w_ob, g_post)
    y_sample = _mlp(h1s, g_mpre, w_u, w_d, g_mpost).reshape(bs, dseq, d)

    meta_lf = jnp.broadcast_to(lfs3[0:1, :, :nh], (bp, n_meta, nh))
    lf_p = jnp.concatenate([meta_lf, lfm3[:, :, :nh]], axis=1)[None]
    return (y_prompt, y_sample,
            k_cache[None], v_cache[None],
            lf_p,
            jnp.swapaxes(sfin_m_t, -1, -2)[None],
            k_small[1:][None], v_small[1:][None],
            lfs3[1:, :, :nh][None],
            jnp.swapaxes(sfin_s_t[1:], -1, -2)[None])
```

```python
import functools

import jax
import jax.numpy as jnp
from jax import lax
from jax.experimental import pallas as pl
from jax.experimental.pallas import tpu as pltpu

F32 = jnp.float32
BF16 = jnp.bfloat16

EPS = 1e-6
HEAD_DIM = 128
LANES = 128
NEG_BIG = -1e30
LOG2E = 1.4426950408889634
VMEM_LIMIT = 56 * 1024 * 1024


def _pick(n, pref):
    if n <= pref:
        return n
    t = pref
    while n % t:
        t //= 2
    return t


def _params(sem):
    return pltpu.CompilerParams(dimension_semantics=sem, vmem_limit_bytes=VMEM_LIMIT)


def _log_sigmoid(x):
    return jnp.minimum(x, 0.0) - jnp.log1p(jnp.exp(-jnp.abs(x)))


def _rms(x, gain):
    ms = jnp.mean(x * x, axis=-1, keepdims=True)
    return x * lax.rsqrt(ms + EPS) * gain


def _silu(t):
    return t * jax.nn.sigmoid(t)


def _log2_forget(t, lbl_ref):
    lg = lbl_ref[...]
    e = jnp.exp(lg - jnp.max(lg, axis=0, keepdims=True))
    lb = e[0:1, :] / jnp.sum(e, axis=0, keepdims=True)
    return jnp.log(lb + (1.0 - lb) * jax.nn.sigmoid(t)) * LOG2E


def _store_heads(z_ref, tile, rows=slice(None)):
    for hd in range(z_ref.shape[0]):
        z_ref[hd, rows, :] = tile[:, hd * HEAD_DIM:(hd + 1) * HEAD_DIM]


def _row_halves(n):
    return (slice(0, n // 2), slice(n // 2, n)) if n % 16 == 0 else (slice(0, n),)


def _store_z_tiles(j, project, lbl_ref, z_ref, silu_tiles=(0, 3)):
    @pl.when(functools.reduce(jnp.logical_or, [j == t for t in silu_tiles]))
    def _():
        _store_heads(z_ref, _silu(project()))

    @pl.when(j == 1)
    def _():
        _store_heads(z_ref, _log2_forget(project(), lbl_ref))

    @pl.when((j == 2) | (j == 4))
    def _():
        _store_heads(z_ref, project() * jnp.where(j == 4, HEAD_DIM ** -0.5 * LOG2E, 1.0))


def _inproj_kernel(x_ref, g_ref, w_ref, wf_ref, fb_ref, lbl_ref, z_ref, lf_ref, a_scr):
    j = pl.program_id(1)

    @pl.when(j == 0)
    def _():
        a = _rms(x_ref[...], g_ref[...]).astype(BF16)
        a_scr[...] = a
        ff = jnp.dot(a, wf_ref[...], preferred_element_type=F32) + fb_ref[...]
        lf_ref[...] = _log_sigmoid(ff)

    def project():
        return jnp.dot(a_scr[...], w_ref[...], preferred_element_type=F32)

    _store_z_tiles(j, project, lbl_ref, z_ref)

    @pl.when(j >= 5)
    def _():
        _store_heads(z_ref, project())


def _inproj(x, gain, w_main, w_f, f_bias, lb_logits):
    rows, d = x.shape
    n = w_main.shape[1]
    tm = _pick(rows, 1024)
    tn = lb_logits.shape[1]
    nh = tn // HEAD_DIM
    return pl.pallas_call(
        _inproj_kernel,
        grid=(rows // tm, n // tn),
        in_specs=[
            pl.BlockSpec((tm, d), lambda i, j: (i, 0)),
            pl.BlockSpec((1, d), lambda i, j: (0, 0)),
            pl.BlockSpec((d, tn), lambda i, j: (0, j)),
            pl.BlockSpec((d, LANES), lambda i, j: (0, 0)),
            pl.BlockSpec((1, LANES), lambda i, j: (0, 0)),
            pl.BlockSpec(lb_logits.shape, lambda i, j: (0, 0)),
        ],
        out_specs=[
            pl.BlockSpec((nh, tm, HEAD_DIM), lambda i, j: (j, i, 0)),
            pl.BlockSpec((tm, LANES), lambda i, j: (i, 0)),
        ],
        out_shape=[
            jax.ShapeDtypeStruct((n // HEAD_DIM, rows, HEAD_DIM), F32),
            jax.ShapeDtypeStruct((rows, LANES), F32),
        ],
        scratch_shapes=[pltpu.VMEM((tm, d), BF16)],
        compiler_params=_params(("parallel", "arbitrary")),
        name="inproj",
    )(x, gain, w_main, w_f, f_bias, lb_logits)


def _inproj_kv_kernel(x_hbm, g_ref, w_ref, wf_ref, fb_ref, lbl_ref, pk_ref, pv_ref,
                      z_ref, lf_ref, k_hbm, v_hbm, a_scr, stage, xbuf, sems, xsems,
                      *, nz, tiles_per_seq):
    i = pl.program_id(0)
    j = pl.program_id(1)
    tm = a_scr.shape[0]
    plen = pk_ref.shape[0]
    last = pl.num_programs(0) - 1

    def x_copy(tile):
        return pltpu.make_async_copy(x_hbm.at[pl.ds(tile * tm, tm), :], xbuf.at[tile % 2],
                                     xsems.at[tile % 2])

    nh = k_hbm.shape[2]

    class _TileCopy:
        def __init__(self, slot, dst, tile):
            row = (tile % tiles_per_seq) * tm + plen
            self.copies = [
                pltpu.make_async_copy(stage.at[slot, :, pl.ds(hd * HEAD_DIM, HEAD_DIM)],
                                      dst.at[tile // tiles_per_seq, pl.ds(row, tm), hd, :],
                                      sems.at[slot])
                for hd in range(nh)]

        def start(self):
            for c in self.copies:
                c.start()

        def wait(self):
            for c in self.copies:
                c.wait()

    tile_copy = _TileCopy

    def prefix_copy(slot, src, dst, tile):
        return pltpu.make_async_copy(src, dst.at[tile // tiles_per_seq, pl.ds(0, plen), :, :],
                                     sems.at[2 + slot])

    def opens_sequence(tile):
        return tile % tiles_per_seq == 0

    @pl.when(j == 0)
    def _():
        @pl.when(i == 0)
        def _():
            x_copy(i).start()
        x_copy(i).wait()
        for rows in _row_halves(tm):
            a = _rms(xbuf[i % 2, rows, :], g_ref[...]).astype(BF16)
            a_scr[rows, :] = a
            ff = jnp.dot(a, wf_ref[...], preferred_element_type=F32) + fb_ref[...]
            lf_ref[rows, :] = _log_sigmoid(ff)
            _store_heads(z_ref, _silu(jnp.dot(a, w_ref[...], preferred_element_type=F32)), rows)

    @pl.when((j == 1) & (i < last))
    def _():
        x_copy(i + 1).start()

    def project():
        return jnp.dot(a_scr[...], w_ref[...], preferred_element_type=F32)

    _store_z_tiles(j, project, lbl_ref, z_ref, silu_tiles=(3,))

    @pl.when(j == nz)
    def _():
        @pl.when(i > 0)
        def _():
            tile_copy(1, v_hbm, i - 1).wait()

            @pl.when(opens_sequence(i - 1))
            def _():
                prefix_copy(1, pv_ref, v_hbm, i - 1).wait()
        stage[0] = project()
        tile_copy(0, k_hbm, i).start()

        @pl.when(opens_sequence(i))
        def _():
            prefix_copy(0, pk_ref, k_hbm, i).start()

    @pl.when(j == nz + 1)
    def _():
        stage[1] = project()
        tile_copy(1, v_hbm, i).start()
        tile_copy(0, k_hbm, i).wait()

        @pl.when(opens_sequence(i))
        def _():
            prefix_copy(1, pv_ref, v_hbm, i).start()
            prefix_copy(0, pk_ref, k_hbm, i).wait()

        @pl.when(i == last)
        def _():
            tile_copy(1, v_hbm, i).wait()

            @pl.when(opens_sequence(i))
            def _():
                prefix_copy(1, pv_ref, v_hbm, i).wait()


def _inproj_kv(x, gain, w_main, w_f, f_bias, lb_logits, prefix_k, prefix_v, *, nseq, nz):
    rows, d = x.shape
    n = w_main.shape[1]
    tn = n // (nz + 2)
    seq = rows // nseq
    plen, nh, _ = prefix_k.shape
    tm = _pick(seq, 1024)
    kern = functools.partial(_inproj_kv_kernel, nz=nz, tiles_per_seq=seq // tm)
    cache = jax.ShapeDtypeStruct((nseq, plen + seq, nh, HEAD_DIM), F32)
    return pl.pallas_call(
        kern,
        grid=(rows // tm, nz + 2),
        in_specs=[
            pl.BlockSpec(memory_space=pl.ANY),
            pl.BlockSpec((1, d), lambda i, j: (0, 0)),
            pl.BlockSpec((d, tn), lambda i, j: (0, j)),
            pl.BlockSpec((d, LANES), lambda i, j: (0, 0)),
            pl.BlockSpec((1, LANES), lambda i, j: (0, 0)),
            pl.BlockSpec(lb_logits.shape, lambda i, j: (0, 0)),
            pl.BlockSpec((plen, nh, HEAD_DIM), lambda i, j: (0, 0, 0)),
            pl.BlockSpec((plen, nh, HEAD_DIM), lambda i, j: (0, 0, 0)),
        ],
        out_specs=[
            pl.BlockSpec((nh, tm, HEAD_DIM), lambda i, j: (jnp.minimum(j, nz - 1), i, 0)),
            pl.BlockSpec((tm, LANES), lambda i, j: (i, 0)),
            pl.BlockSpec(memory_space=pl.ANY),
            pl.BlockSpec(memory_space=pl.ANY),
        ],
        out_shape=[
            jax.ShapeDtypeStruct((nz * nh, rows, HEAD_DIM), F32),
            jax.ShapeDtypeStruct((rows, LANES), F32),
            cache, cache,
        ],
        scratch_shapes=[pltpu.VMEM((tm, d), BF16), pltpu.VMEM((2, tm, tn), F32),
                        pltpu.VMEM((2, tm, d), F32),
                        pltpu.SemaphoreType.DMA((4,)), pltpu.SemaphoreType.DMA((2,))],
        compiler_params=_params(("arbitrary", "arbitrary")),
        name="inproj_kv",
    )(x, gain, w_main, w_f, f_bias, lb_logits, prefix_k, prefix_v)


def _split3(x):
    hi = x.astype(BF16)
    r = x - hi.astype(F32)
    mid = r.astype(BF16)
    lo = (r - mid.astype(F32)).astype(BF16)
    return hi, mid, lo


def _cumsum_kernel(x_ref, init_ref, c_ref, *, chain):
    g, nblk, blk = x_ref.shape
    nrow = g * nblk
    upper = (lax.broadcasted_iota(jnp.int32, (blk, blk), 0)
             <= lax.broadcasted_iota(jnp.int32, (blk, blk), 1)).astype(BF16)
    c = sum(jnp.dot(p, upper, preferred_element_type=F32)
            for p in _split3(x_ref[...].reshape(nrow, blk)))
    if chain:
        shift = nblk.bit_length() - 1
        ri = lax.broadcasted_iota(jnp.int32, (nrow, nrow), 0)
        ci = lax.broadcasted_iota(jnp.int32, (nrow, nrow), 1)
        earlier = jnp.where(ci < ri, 1.0, 0.0)
        before = jnp.where((ri >> shift) == (ci >> shift), earlier, 0.0).astype(BF16)
        tot = jnp.broadcast_to(c[:, blk - 1:blk], (nrow, blk))
        c = c + sum(jnp.dot(before, p, preferred_element_type=F32) for p in _split3(tot))
    init = jnp.broadcast_to(init_ref[...], (g, nblk, blk)).reshape(nrow, blk)
    c_ref[...] = (c + init).reshape(g, nblk, blk)


def _cumsum(x, init, *, chain):
    n, nblk, blk = x.shape
    assert not chain or nblk & (nblk - 1) == 0, "chained blocks per sequence must be a power of two"
    g = _pick(n, 32)
    ni = init.shape[1]
    return pl.pallas_call(
        functools.partial(_cumsum_kernel, chain=chain),
        grid=(n // g,),
        in_specs=[
            pl.BlockSpec((g, nblk, blk), lambda i: (i, 0, 0)),
            pl.BlockSpec((g, ni, blk), lambda i: (i, 0, 0)),
        ],
        out_specs=pl.BlockSpec((g, nblk, blk), lambda i: (i, 0, 0)),
        out_shape=jax.ShapeDtypeStruct((n, nblk, blk), F32),
        compiler_params=_params(("parallel",)),
        name="cumsum",
    )(x, init)


def _cumsum_time(lf, init, nh):
    nb, t = lf.shape[0], lf.shape[1]
    blk = _pick(t, 256)
    x = jnp.transpose(lf[:, :, :nh], (0, 2, 1))
    if t == blk:
        init3 = jnp.broadcast_to(init.reshape(1, nb * nh, 1), (1, nb * nh, blk))
        return _cumsum(x.reshape(1, nb * nh, blk), init3, chain=False).reshape(nb, nh, t)
    init3 = jnp.broadcast_to(init.reshape(nb * nh, 1, 1), (nb * nh, 1, blk))
    return _cumsum(x.reshape(nb * nh, t // blk, blk), init3, chain=True).reshape(nb, nh, t)


def _block_mid(pre, half):
    c = pre.shape[0]
    if half >= 8:
        x3 = pre.reshape(c // (2 * half), 2 * half, HEAD_DIM)
        return jnp.broadcast_to(x3[:, half - 1:half, :], x3.shape).reshape(c, HEAD_DIM)
    x3 = pre.reshape(c // 8, 8, HEAD_DIM)
    sub = lax.broadcasted_iota(jnp.int32, x3.shape, 1)
    mid = None
    for blk in range(8 // (2 * half)):
        src = 2 * half * blk + half - 1
        piece = jnp.broadcast_to(x3[:, src:src + 1, :], x3.shape)
        mid = piece if mid is None else jnp.where(sub >= 2 * half * blk, piece, mid)
    return mid.reshape(c, HEAD_DIM)


def _hgrn_masks(c):
    rows = lax.broadcasted_iota(jnp.int32, (c, HEAD_DIM), 0)
    ri = lax.broadcasted_iota(jnp.int32, (c, c), 0)
    ci = lax.broadcasted_iota(jnp.int32, (c, c), 1)
    levels = c.bit_length() - 1
    upper = [(rows & (1 << (lvl - 1))) != 0 for lvl in range(1, levels + 1)]
    owner = jnp.where(ri >= ci, 32 - lax.clz(ri ^ ci), -1)
    return owner, upper


def _hgrn_chunk(q, g2, hi, st, masks):
    c = q.shape[0]
    owner, upper_masks = masks
    k = 1.0 - jnp.exp2(g2)

    scores = jnp.where(owner == 0, jnp.sum(q * k, axis=-1, keepdims=True), 0.0)
    pre = g2
    for lvl, upper in enumerate(upper_masks, start=1):
        mid = _block_mid(pre, 1 << (lvl - 1))
        w = jnp.exp2(jnp.where(upper, pre, mid - pre))
        x = (jnp.where(upper, q, k) * w).astype(BF16)
        s_l = lax.dot_general(x, x, (((1,), (1,)), ((), ())), preferred_element_type=F32)
        scores = jnp.where(owner == lvl, s_l, scores)
        pre = jnp.where(upper, pre + mid, pre)
    cum = pre
    v = hi.astype(BF16)
    qe = (q * jnp.exp2(cum)).astype(BF16)
    o = (lax.dot_general(qe, st.astype(BF16), (((1,), (1,)), ((), ())), preferred_element_type=F32)
         + jnp.dot(scores.astype(BF16), v, preferred_element_type=F32))
    last = cum[c - 1:c, :]
    kd = (k * jnp.exp2(last - cum)).astype(BF16)
    st_new = st * jnp.exp2(last) + lax.dot_general(
        v, kd, (((0,), (0,)), ((), ())), preferred_element_type=F32)
    return o, st_new


def _hgrn_kernel(hq_ref, hf_ref, hi_ref, hg_ref, gain_ref, s0_ref,
                 o_ref, sfin_ref, st_scr, *, chunk):
    ti = pl.program_id(2)
    hps = st_scr.shape[0]

    @pl.when(ti == 0)
    def _():
        st_scr[...] = s0_ref[0]

    tt = hq_ref.shape[1]
    masks = _hgrn_masks(chunk)
    for hh in range(hps):
        gain = gain_ref[hh]
        st = st_scr[hh]
        for cidx in range(tt // chunk):
            sl = slice(cidx * chunk, (cidx + 1) * chunk)
            o, st = _hgrn_chunk(hq_ref[hh, sl, :], hf_ref[hh, sl, :], hi_ref[hh, sl, :], st, masks)
            o_ref[hh, sl, :] = (_rms(o, gain) * hg_ref[hh, sl, :]).astype(o_ref.dtype)
        st_scr[hh] = st

        @pl.when(ti == pl.num_programs(2) - 1)
        def _():
            sfin_ref[0, hh] = st


def _hgrn(z, nb, gain, s0t, *, s0_shared, chunk, tt, hps):
    rows = z.shape[1]
    t = rows // nb
    nh = gain.shape[0]
    ng = nh // hps
    nt = t // tt
    s0_map = (lambda b, h, i: (0, h, 0, 0)) if s0_shared else (lambda b, h, i: (b, h, 0, 0))

    def col(tile):
        return pl.BlockSpec((hps, tt, HEAD_DIM), lambda b, h, i: (tile * ng + h, b * nt + i, 0))

    return pl.pallas_call(
        functools.partial(_hgrn_kernel, chunk=chunk),
        grid=(nb, ng, nt),
        in_specs=[
            col(0), col(1), col(2), col(3),
            pl.BlockSpec((hps, 1, HEAD_DIM), lambda b, h, i: (h, 0, 0)),
            pl.BlockSpec((1, hps, HEAD_DIM, HEAD_DIM), s0_map),
        ],
        out_specs=[
            pl.BlockSpec((hps, tt, HEAD_DIM), lambda b, h, i: (h, b * nt + i, 0)),
            pl.BlockSpec((1, hps, HEAD_DIM, HEAD_DIM), lambda b, h, i: (b, h, 0, 0)),
        ],
        out_shape=[
            jax.ShapeDtypeStruct((nh, rows, HEAD_DIM), BF16),
            jax.ShapeDtypeStruct((nb, nh, HEAD_DIM, HEAD_DIM), F32),
        ],
        scratch_shapes=[pltpu.VMEM((hps, HEAD_DIM, HEAD_DIM), F32)],
        compiler_params=_params(("parallel", "parallel", "arbitrary")),
        name="hgrn",
    )(z, z, z, z, gain, s0t)


def _slice_prefetch(k_hbm, v_hbm, kbuf, vbuf, sems, lin, nslices, nheads):
    def copies(idx):
        seq_idx, head = idx // nheads, idx % nheads
        return (pltpu.make_async_copy(k_hbm.at[seq_idx, :, head, :], kbuf.at[idx % 2],
                                      sems.at[0, idx % 2]),
                pltpu.make_async_copy(v_hbm.at[seq_idx, :, head, :], vbuf.at[idx % 2],
                                      sems.at[1, idx % 2]))

    @pl.when(lin == 0)
    def _():
        for c in copies(lin):
            c.start()
    for c in copies(lin):
        c.wait()

    @pl.when(lin + 1 < nslices)
    def _():
        for c in copies(lin + 1):
            c.start()


def _fox_kernel(q_ref, cq_ref, kp_ref, vp_ref, ckp_ref, ks_ref, vs_ref, cks_ref, o_ref,
                *, pchunk):
    tq = q_ref.shape[1]
    nheads = ckp_ref.shape[1]
    plen = ckp_ref.shape[3]
    lane = lax.broadcasted_iota(jnp.int32, (tq, LANES), 1)
    causal = (lax.broadcasted_iota(jnp.int32, (tq, tq), 1)
              <= lax.broadcasted_iota(jnp.int32, (tq, tq), 0))
    starts = [pc * pchunk for pc in range(plen // pchunk)]
    for hd in range(nheads):
        q = q_ref[hd].astype(BF16)
        cq = jnp.sum(jnp.where(lane == hd, cq_ref[0], 0.0), axis=-1, keepdims=True)

        def head_rows(ref, start):
            return ref.at[0][pl.ds(start * nheads + hd, pchunk, stride=nheads), :]

        def logits(kb, ck):
            return lax.dot_general(q, kb.astype(BF16), (((1,), (1,)), ((), ())),
                                   preferred_element_type=F32) + (cq - ck) * LOG2E

        s_list = [logits(head_rows(kp_ref, st), ckp_ref[0, hd, :, st:st + pchunk]) for st in starts]
        s_list.append(jnp.where(causal, logits(ks_ref[hd], cks_ref[0, hd]), -jnp.inf))
        v_list = [head_rows(vp_ref, st) for st in starts] + [vs_ref[hd]]
        m = functools.reduce(jnp.maximum, [jnp.max(s, axis=-1, keepdims=True) for s in s_list])
        l = jnp.zeros((tq, 1), F32)
        acc = jnp.zeros((tq, HEAD_DIM), F32)
        for s, vb in zip(s_list, v_list):
            p = jnp.exp2(s - m)
            l = l + jnp.sum(p, axis=-1, keepdims=True)
            acc = acc + jnp.dot(p.astype(BF16), vb.astype(BF16), preferred_element_type=F32)
        o_ref[hd] = (acc / l).astype(o_ref.dtype)


def _fox(z, b_off, qtile, ktile, vtile, cq, kp, vp, ckp, cks):
    nb, t, _ = cq.shape
    nh = cks.shape[1]
    plen = kp.shape[1]
    kern = functools.partial(_fox_kernel, pchunk=_pick(plen, 512))
    return pl.pallas_call(
        kern,
        grid=(nb,),
        in_specs=[
            pl.BlockSpec((nh, t, HEAD_DIM), lambda b: (qtile, b + b_off, 0)),
            pl.BlockSpec((1, t, LANES), lambda b: (b, 0, 0)),
            pl.BlockSpec((1, plen * nh, HEAD_DIM), lambda b: (b, 0, 0)),
            pl.BlockSpec((1, plen * nh, HEAD_DIM), lambda b: (b, 0, 0)),
            pl.BlockSpec((1, nh, 1, plen), lambda b: (b, 0, 0, 0)),
            pl.BlockSpec((nh, t, HEAD_DIM), lambda b: (ktile, b + b_off, 0)),
            pl.BlockSpec((nh, t, HEAD_DIM), lambda b: (vtile, b + b_off, 0)),
            pl.BlockSpec((1, nh, 1, t), lambda b: (b, 0, 0, 0)),
        ],
        out_specs=pl.BlockSpec((nh, t, HEAD_DIM), lambda b: (0, b, 0)),
        out_shape=jax.ShapeDtypeStruct((nh, nb * t, HEAD_DIM), BF16),
        compiler_params=_params(("parallel",)),
        name="fox",
    )(z, cq, kp.reshape(nb, plen * nh, HEAD_DIM), vp.reshape(nb, plen * nh, HEAD_DIM),
      ckp, z, z, cks)


def _bias_pieces(c):
    hi = c.astype(BF16).astype(F32)
    r = c - hi
    mid = r.astype(BF16).astype(F32)
    return hi, mid, r - mid


def _bias_rows(c, ones_first):
    hi, mid, lo = _bias_pieces(c)
    sub = lax.broadcasted_iota(jnp.int32, (HEAD_DIM, c.shape[1]), 0)
    third = jnp.where(sub >= 3, sub - 3, sub)
    pieces = jnp.where(third == 0, hi, jnp.where(third == 1, mid, lo))
    first, second = (1.0, pieces) if ones_first else (pieces, 1.0)
    return jnp.where(sub < 3, first, jnp.where(sub < 6, second, 0.0))


def _bias_cols(c):
    hi, mid, lo = _bias_pieces(c)
    lane = lax.broadcasted_iota(jnp.int32, (c.shape[0], LANES), 1)
    pieces = jnp.where(lane == 3, hi, jnp.where(lane == 4, mid, lo))
    return jnp.where(lane < 3, 1.0, jnp.where(lane < 6, pieces, 0.0))


def _fox_prompt_kernel(q_ref, cq_ref, k_hbm, v_hbm, ck_ref, ckp_ref, o_ref,
                       kaug_scr, vt_scr, sa_scr, sb_scr, acc_scr, state_scr, kbuf, vbuf, sems):
    assert q_ref.shape[1] == 2 * sa_scr.shape[0]
    h = pl.program_id(1)
    qi = pl.program_id(2)
    nheads = pl.num_programs(1)
    nslices = pl.num_programs(0) * nheads
    lin = pl.program_id(0) * nheads + h
    slot = lin % 2
    tq = q_ref.shape[1]
    tk = sa_scr.shape[0]
    plen = ckp_ref.shape[1]
    t = kbuf.shape[1] - plen

    @pl.when(qi == 0)
    def _():
        _slice_prefetch(k_hbm, v_hbm, kbuf, vbuf, sems, lin, nslices, nheads)

        def prep(cix, carry):
            r0 = pl.multiple_of(cix * tk, tk)
            src = pl.ds(pl.multiple_of(plen + r0, 8), tk)
            ck = ck_ref[0, 0, :, pl.ds(r0, tk)] * LOG2E
            kaug_scr[pl.ds(r0, tk), HEAD_DIM:] = _bias_rows(-ck, True).astype(BF16).T
            kaug_scr[pl.ds(r0, tk), :HEAD_DIM] = kbuf[slot, src, :].astype(BF16)
            vt_scr[:, pl.ds(r0, tk)] = vbuf[slot, src, :].astype(BF16).T
            return carry
        lax.fori_loop(0, t // tk, prep, 0)

    qt = q_ref[0].astype(BF16).T
    augq = _bias_rows(cq_ref[0, 0] * LOG2E, False).astype(BF16)
    qaug = jnp.concatenate([qt, augq], axis=0)

    def row0(blk):
        return pl.multiple_of(blk * tk, tk)

    def colmax(s):
        return jnp.max(s, axis=0, keepdims=True)

    def stage(blk, s_scr):
        s = jnp.dot(kaug_scr[pl.ds(row0(blk), tk), :], qaug, preferred_element_type=F32)
        s_scr[...] = s
        return colmax(s)

    def pv_block(blk):
        return lambda p: jnp.dot(vt_scr[:, pl.ds(row0(blk), tk)], p, preferred_element_type=F32)

    def absorb(ml, blocks, cols=None):
        m, l = ml
        m_new = m
        for _, smax, _ in blocks:
            m_new = jnp.maximum(m_new, smax)
        alpha = jnp.exp2(m - m_new)
        l = alpha * l
        pv = None
        for s, _, pv_fn in blocks:
            p = jnp.exp2(s - m_new)
            l = l + jnp.sum(p, axis=0, keepdims=True)
            term = pv_fn(p.astype(BF16))
            pv = term if pv is None else pv + term
        where = (Ellipsis,) if cols is None else (slice(None), cols)
        acc_scr[where] = alpha * acc_scr[where] + pv
        return m_new, l

    acc_scr[...] = jnp.zeros_like(acc_scr)
    ml = (jnp.full((1, tq), NEG_BIG, F32), jnp.zeros((1, tq), F32))

    def body(pair, state):
        ma, ml = state
        mb = stage(2 * pair + 1, sb_scr)
        ml = absorb(ml, [(sa_scr[...], ma, pv_block(2 * pair))])
        ma = stage(2 * pair + 2, sa_scr)
        ml = absorb(ml, [(sb_scr[...], mb, pv_block(2 * pair + 1))])
        return ma, ml

    def two_pairs(i, state):
        return body(2 * i + 1, body(2 * i, state))

    ma, (m_run, l_run) = lax.fori_loop(0, qi // 2, two_pairs, (stage(0, sa_scr), ml))
    state_scr[0:1, :], state_scr[1:2, :], state_scr[2:3, :] = ma, m_run, l_run

    @pl.when(qi % 2 == 1)
    def _():
        ma, (m_run, l_run) = body(qi - 1, (state_scr[0:1, :], (state_scr[1:2, :], state_scr[2:3, :])))
        state_scr[1:2, :], state_scr[2:3, :] = m_run, l_run

    ml = (state_scr[1:2, :], state_scr[2:3, :])

    lane = lax.broadcasted_iota(jnp.int32, ckp_ref.shape[1:], 1)
    ckp = jnp.sum(jnp.where(lane == h, ckp_ref[0], 0.0), axis=-1, keepdims=True) * LOG2E
    kp = jnp.concatenate([kbuf[slot, :plen, :].astype(BF16), _bias_cols(-ckp).astype(BF16)], axis=1)
    vp = vbuf[slot, :plen, :].astype(BF16)

    def pv_prefix(p):
        return lax.dot_general(vp, p, (((0,), (0,)), ((), ())), preferred_element_type=F32)

    first, second = slice(0, tk), slice(tk, tq)
    tri = (lax.broadcasted_iota(jnp.int32, (tk, tk), 0)
           <= lax.broadcasted_iota(jnp.int32, (tk, tk), 1))
    s_p = jnp.dot(kp, qaug, preferred_element_type=F32)
    s_a = sa_scr[...]
    stage(2 * qi + 1, sb_scr)
    s_b = sb_scr[:, second]
    m_run, l_run = ml
    l_halves = []
    for cols, logit_blocks in (
            (first, [(s_p[:, first], pv_prefix),
                     (jnp.where(tri, s_a[:, first], -jnp.inf), pv_block(2 * qi))]),
            (second, [(s_p[:, second], pv_prefix),
                      (s_a[:, second], pv_block(2 * qi)),
                      (jnp.where(tri, s_b, -jnp.inf), pv_block(2 * qi + 1))])):
        blocks = [(s, colmax(s), pv_fn) for s, pv_fn in logit_blocks]
        _, l_half = absorb((m_run[:, cols], l_run[:, cols]), blocks, cols)
        l_halves.append(l_half)
    l = jnp.concatenate(l_halves, axis=1)
    o_ref[0] = (acc_scr[...] * (1.0 / l)).astype(o_ref.dtype).T


def _fox_prompt(z, qtile, k_cache, v_cache, c_row, cp_col, *, tq):
    nb, nh, _, t = c_row.shape
    nq = t // tq
    plen = cp_col.shape[1]
    return pl.pallas_call(
        _fox_prompt_kernel,
        grid=(nb, nh, nq),
        in_specs=[
            pl.BlockSpec((1, tq, HEAD_DIM), lambda b, h, i: (qtile * nh + h, b * nq + i, 0)),
            pl.BlockSpec((1, 1, 1, tq), lambda b, h, i: (b, h, 0, i)),
            pl.BlockSpec(memory_space=pl.ANY),
            pl.BlockSpec(memory_space=pl.ANY),
            pl.BlockSpec((1, 1, 1, t), lambda b, h, i: (b, h, 0, 0)),
            pl.BlockSpec((1, plen, LANES), lambda b, h, i: (0, 0, 0)),
        ],
        out_specs=pl.BlockSpec((1, tq, HEAD_DIM), lambda b, h, i: (h, b * nq + i, 0)),
        out_shape=jax.ShapeDtypeStruct((nh, nb * t, HEAD_DIM), BF16),
        scratch_shapes=[pltpu.VMEM((t, 2 * HEAD_DIM), BF16), pltpu.VMEM((HEAD_DIM, t), BF16),
                        pltpu.VMEM((tq // 2, tq), F32), pltpu.VMEM((tq // 2, tq), F32),
                        pltpu.VMEM((HEAD_DIM, tq), F32), pltpu.VMEM((8, tq), F32),
                        pltpu.VMEM((2, plen + t, HEAD_DIM), F32),
                        pltpu.VMEM((2, plen + t, HEAD_DIM), F32),
                        pltpu.SemaphoreType.DMA((2, 2))],
        compiler_params=_params(("arbitrary", "arbitrary", "arbitrary")),
        name="fox_prompt",
    )(z, c_row, k_cache, v_cache, c_row, cp_col)


def _outproj_kernel(h_ref, a_ref, b_ref, wa_ref, wb_ref, g_ref, o_ref):
    def heads_on_lanes(ref, rows):
        return jnp.concatenate([ref[hd, rows, :] for hd in range(ref.shape[0])], axis=1)

    for rows in _row_halves(o_ref.shape[0]):
        mix = (jnp.dot(heads_on_lanes(a_ref, rows), wa_ref[...], preferred_element_type=F32)
               + jnp.dot(heads_on_lanes(b_ref, rows), wb_ref[...], preferred_element_type=F32))
        o_ref[rows, :] = h_ref[rows, :] + _rms(mix, g_ref[...])


def _outproj(h, mix_a, mix_b, w_a, w_b, gain):
    rows, d = h.shape
    nh = mix_a.shape[0]
    ka, kb = w_a.shape[0], w_b.shape[0]
    tm = _pick(rows, 512)
    return pl.pallas_call(
        _outproj_kernel,
        grid=(rows // tm,),
        in_specs=[
            pl.BlockSpec((tm, d), lambda i: (i, 0)),
            pl.BlockSpec((nh, tm, HEAD_DIM), lambda i: (0, i, 0)),
            pl.BlockSpec((nh, tm, HEAD_DIM), lambda i: (0, i, 0)),
            pl.BlockSpec((ka, d), lambda i: (0, 0)),
            pl.BlockSpec((kb, d), lambda i: (0, 0)),
            pl.BlockSpec((1, d), lambda i: (0, 0)),
        ],
        out_specs=pl.BlockSpec((tm, d), lambda i: (i, 0)),
        out_shape=jax.ShapeDtypeStruct((rows, d), F32),
        compiler_params=_params(("parallel",)),
        name="outproj",
    )(h, mix_a, mix_b, w_a, w_b, gain)


def _mlp_kernel(h_ref, gpre_ref, wu_ref, wd_ref, gpost_ref, o_ref, a_scr, acc_scr):
    j = pl.program_id(1)
    last = pl.num_programs(1) - 1
    halves = _row_halves(o_ref.shape[0])

    def up_down(rows):
        u = jnp.maximum(jnp.dot(a_scr[rows, :], wu_ref[...], preferred_element_type=F32), 0.0)
        return jnp.dot((u * u).astype(BF16), wd_ref[...], preferred_element_type=F32)

    @pl.when(j == 0)
    def _():
        for rows in halves:
            a_scr[rows, :] = _rms(h_ref[rows, :], gpre_ref[...]).astype(BF16)
            acc_scr[rows, :] = up_down(rows)

    @pl.when((j > 0) & (j < last))
    def _():
        acc_scr[...] += up_down(slice(None))

    @pl.when(j == last)
    def _():
        for rows in halves:
            acc = acc_scr[rows, :] + up_down(rows)
            o_ref[rows, :] = h_ref[rows, :] + _rms(acc, gpost_ref[...])


def _mlp(h, g_pre, w_up, w_down, g_post):
    rows, d = h.shape
    dff = w_up.shape[1]
    tm = _pick(rows, 512)
    tf = _pick(dff, 1024)
    assert dff // tf >= 2
    return pl.pallas_call(
        _mlp_kernel,
        grid=(rows // tm, dff // tf),
        in_specs=[
            pl.BlockSpec((tm, d), lambda i, j: (i, 0)),
            pl.BlockSpec((1, d), lambda i, j: (0, 0)),
            pl.BlockSpec((d, tf), lambda i, j: (0, j)),
            pl.BlockSpec((tf, d), lambda i, j: (j, 0)),
            pl.BlockSpec((1, d), lambda i, j: (0, 0)),
        ],
        out_specs=pl.BlockSpec((tm, d), lambda i, j: (i, 0)),
        out_shape=jax.ShapeDtypeStruct((rows, d), F32),
        scratch_shapes=[pltpu.VMEM((tm, d), BF16), pltpu.VMEM((tm, d), F32)],
        compiler_params=_params(("parallel", "arbitrary")),
        name="mlp",
    )(h, g_pre, w_up, w_down, g_post)


def _col_form(c_row):
    c_col = jnp.transpose(c_row, (0, 2, 1))
    return jnp.pad(c_col, ((0, 0), (0, 0), (0, LANES - c_col.shape[2])))


def kernel(x_prompt, x_sample, cache_fox_k, cache_fox_v, cache_fox_logf, state_hgrn, meta_tokens,
           g_mix_pre, w_in, hg_lb_logits, hg_norm_gain, fox_f_bias, w_out, g_mix_post, g_mlp_pre,
           w_up, w_down, g_mlp_post):
    bp, seq, d = x_prompt.shape
    bs, dseq, _ = x_sample.shape
    n_meta = meta_tokens.shape[0]
    past = cache_fox_k.shape[2]
    nh = cache_fox_k.shape[3]
    hw = nh * HEAD_DIM
    n_main = w_in.shape[2] - nh
    assert dseq == n_meta, "sample frames and meta tokens share the small-stream kernels"
    assert state_hgrn.shape[2] == nh and n_main == 7 * hw

    w_main = w_in[0, :, :n_main].astype(BF16)
    w_f = jnp.pad(w_in[0, :, n_main:], ((0, 0), (0, LANES - nh))).astype(BF16)
    f_bias = jnp.pad(fox_f_bias[0], (0, LANES - nh))[None, :]
    w_oa = w_out[0, :hw].astype(BF16)
    w_ob = w_out[0, hw:].astype(BF16)
    w_u = w_up[0].astype(BF16)
    w_d = w_down[0].astype(BF16)
    g_pre = g_mix_pre[0][None, :]
    g_post = g_mix_post[0][None, :]
    g_mpre = g_mlp_pre[0][None, :]
    g_mpost = g_mlp_post[0][None, :]
    hg_gain = hg_norm_gain[0].reshape(nh, 1, HEAD_DIM)
    qtile, ktile, vtile = 4, 5, 6

    ns = bs + 1
    xs = jnp.concatenate([meta_tokens, x_sample.reshape(bs * dseq, d)], axis=0)
    zs, lfs = _inproj(xs, g_pre, w_main, w_f, f_bias, hg_lb_logits)
    lfs3 = lfs.reshape(ns, dseq, LANES)

    def small_kv(tile):
        rows_first = jnp.transpose(zs[tile * nh:(tile + 1) * nh], (1, 0, 2))
        return rows_first.reshape(ns, dseq, nh, HEAD_DIM)

    k_small, v_small = small_kv(ktile), small_kv(vtile)

    c_past = _cumsum_time(cache_fox_logf[0], jnp.zeros((bs, nh), F32), nh)
    init_s = jnp.concatenate([jnp.zeros((1, nh), F32), c_past[:, :, past - 1]], axis=0)
    c_small = _cumsum_time(lfs3, init_s, nh)

    s0_small = jnp.concatenate([jnp.zeros((1,) + state_hgrn.shape[2:], F32), state_hgrn[0]], axis=0)
    s0_small_t = jnp.swapaxes(s0_small, -1, -2)
    mix_hg_s, sfin_s_t = _hgrn(zs, ns, hg_gain, s0_small_t, s0_shared=False,
                               chunk=dseq, tt=dseq, hps=nh)

    mix_fox_s = _fox(zs, 1, qtile, ktile, vtile, _col_form(c_small[1:]),
                     cache_fox_k[0], cache_fox_v[0], c_past[:, :, None, :], c_small[1:, :, None, :])

    xm = x_prompt.reshape(bp * seq, d)
    zm, lfm, k_cache, v_cache = _inproj_kv(xm, g_pre, w_main, w_f, f_bias, hg_lb_logits,
                                           k_small[0], v_small[0], nseq=bp, nz=ktile)
    lfm3 = lfm.reshape(bp, seq, LANES)
    c_meta = c_small[0:1]
    init_m = jnp.broadcast_to(c_meta[:, :, n_meta - 1], (bp, nh))
    c_main = _cumsum_time(lfm3, init_m, nh)

    mix_hg_m, sfin_m_t = _hgrn(zm, bp, hg_gain, sfin_s_t[0:1], s0_shared=True,
                               chunk=_pick(seq, 128), tt=_pick(seq, 2048), hps=1)
    mix_fox_m = _fox_prompt(zm, qtile, k_cache, v_cache, c_main[:, :, None, :], _col_form(c_meta),
                            tq=_pick(seq, 1024))

    h1m = _outproj(xm, mix_hg_m, mix_fox_m, w_oa, w_ob, g_post)
    y_prompt = _mlp(h1m, g_mpre, w_u, w_d, g_mpost).reshape(bp, seq, d)
    h1s = _outproj(x_sample.reshape(bs * dseq, d), mix_hg_s[:, n_meta:, :], mix_fox_s,
                   w_oa, w_ob, g_post)
    y_sample = _mlp(h1s, g_mpre, w_u, w_d, g_mpost).reshape(bs, dseq, d)

    meta_lf = jnp.broadcast_to(lfs3[0:1, :, :nh], (bp, n_meta, nh))
    lf_p = jnp.concatenate([meta_lf, lfm3[:, :, :nh]], axis=1)[None]
    return (y_prompt, y_sample,
            k_cache[None], v_cache[None],
            lf_p,
            jnp.swapaxes(sfin_m_t, -1, -2)[None],
            k_small[1:][None], v_small[1:][None],
            lfs3[1:, :, :nh][None],
            jnp.swapaxes(sfin_s_t[1:], -1, -2)[None])
```

```python
import functools

import jax
import jax.numpy as jnp
from jax import lax
from jax.experimental import pallas as pl
from jax.experimental.pallas import tpu as pltpu

F32 = jnp.float32
BF16 = jnp.bfloat16

EPS = 1e-6
HEAD_DIM = 128
LANES = 128
NEG_BIG = -1e30
LOG2E = 1.4426950408889634
VMEM_LIMIT = 56 * 1024 * 1024


def _pick(n, pref):
    if n <= pref:
        return n
    t = pref
    while n % t:
        t //= 2
    return t


def _params(sem):
    return pltpu.CompilerParams(dimension_semantics=sem, vmem_limit_bytes=VMEM_LIMIT)


def _log_sigmoid(x):
    return jnp.minimum(x, 0.0) - jnp.log1p(jnp.exp(-jnp.abs(x)))


def _rms(x, gain):
    ms = jnp.mean(x * x, axis=-1, keepdims=True)
    return x * lax.rsqrt(ms + EPS) * gain


def _silu(t):
    return t * jax.nn.sigmoid(t)


def _log2_forget(t, lbl_ref):
    lg = lbl_ref[...]
    e = jnp.exp(lg - jnp.max(lg, axis=0, keepdims=True))
    lb = e[0:1, :] / jnp.sum(e, axis=0, keepdims=True)
    return jnp.log(lb + (1.0 - lb) * jax.nn.sigmoid(t)) * LOG2E


def _store_heads(z_ref, tile, rows=slice(None)):
    for hd in range(z_ref.shape[0]):
        z_ref[hd, rows, :] = tile[:, hd * HEAD_DIM:(hd + 1) * HEAD_DIM]


def _row_halves(n):
    return (slice(0, n // 2), slice(n // 2, n)) if n % 16 == 0 else (slice(0, n),)


def _store_z_tiles(j, project, lbl_ref, z_ref, silu_tiles=(0, 3)):
    @pl.when(functools.reduce(jnp.logical_or, [j == t for t in silu_tiles]))
    def _():
        _store_heads(z_ref, _silu(project()))

    @pl.when(j == 1)
    def _():
        _store_heads(z_ref, _log2_forget(project(), lbl_ref))

    @pl.when((j == 2) | (j == 4))
    def _():
        _store_heads(z_ref, project() * jnp.where(j == 4, HEAD_DIM ** -0.5 * LOG2E, 1.0))


def _inproj_kernel(x_ref, g_ref, w_ref, wf_ref, fb_ref, lbl_ref, z_ref, lf_ref, a_scr):
    j = pl.program_id(1)

    @pl.when(j == 0)
    def _():
        a = _rms(x_ref[...], g_ref[...]).astype(BF16)
        a_scr[...] = a
        ff = jnp.dot(a, wf_ref[...], preferred_element_type=F32) + fb_ref[...]
        lf_ref[...] = _log_sigmoid(ff)

    def project():
        return jnp.dot(a_scr[...], w_ref[...], preferred_element_type=F32)

    _store_z_tiles(j, project, lbl_ref, z_ref)

    @pl.when(j >= 5)
    def _():
        _store_heads(z_ref, project())


def _inproj(x, gain, w_main, w_f, f_bias, lb_logits):
    rows, d = x.shape
    n = w_main.shape[1]
    tm = _pick(rows, 1024)
    tn = lb_logits.shape[1]
    nh = tn // HEAD_DIM
    return pl.pallas_call(
        _inproj_kernel,
        grid=(rows // tm, n // tn),
        in_specs=[
            pl.BlockSpec((tm, d), lambda i, j: (i, 0)),
            pl.BlockSpec((1, d), lambda i, j: (0, 0)),
            pl.BlockSpec((d, tn), lambda i, j: (0, j)),
            pl.BlockSpec((d, LANES), lambda i, j: (0, 0)),
            pl.BlockSpec((1, LANES), lambda i, j: (0, 0)),
            pl.BlockSpec(lb_logits.shape, lambda i, j: (0, 0)),
        ],
        out_specs=[
            pl.BlockSpec((nh, tm, HEAD_DIM), lambda i, j: (j, i, 0)),
            pl.BlockSpec((tm, LANES), lambda i, j: (i, 0)),
        ],
        out_shape=[
            jax.ShapeDtypeStruct((n // HEAD_DIM, rows, HEAD_DIM), F32),
            jax.ShapeDtypeStruct((rows, LANES), F32),
        ],
        scratch_shapes=[pltpu.VMEM((tm, d), BF16)],
        compiler_params=_params(("parallel", "arbitrary")),
        name="inproj",
    )(x, gain, w_main, w_f, f_bias, lb_logits)


def _inproj_kv_kernel(x_hbm, g_ref, w_ref, wf_ref, fb_ref, lbl_ref, pk_ref, pv_ref,
                      z_ref, lf_ref, k_hbm, v_hbm, a_scr, stage, xbuf, sems, xsems,
                      *, nz, tiles_per_seq):
    i = pl.program_id(0)
    j = pl.program_id(1)
    tm = a_scr.shape[0]
    plen = pk_ref.shape[0]
    last = pl.num_programs(0) - 1

    def x_copy(tile):
        return pltpu.make_async_copy(x_hbm.at[pl.ds(tile * tm, tm), :], xbuf.at[tile % 2],
                                     xsems.at[tile % 2])

    nh = k_hbm.shape[2]

    class _TileCopy:
        def __init__(self, slot, dst, tile):
            row = (tile % tiles_per_seq) * tm + plen
            self.copies = [
                pltpu.make_async_copy(stage.at[slot, :, pl.ds(hd * HEAD_DIM, HEAD_DIM)],
                                      dst.at[tile // tiles_per_seq, pl.ds(row, tm), hd, :],
                                      sems.at[slot])
                for hd in range(nh)]

        def start(self):
            for c in self.copies:
                c.start()

        def wait(self):
            for c in self.copies:
                c.wait()

    tile_copy = _TileCopy

    def prefix_copy(slot, src, dst, tile):
        return pltpu.make_async_copy(src, dst.at[tile // tiles_per_seq, pl.ds(0, plen), :, :],
                                     sems.at[2 + slot])

    def opens_sequence(tile):
        return tile % tiles_per_seq == 0

    @pl.when(j == 0)
    def _():
        @pl.when(i == 0)
        def _():
            x_copy(i).start()
        x_copy(i).wait()
        for rows in _row_halves(tm):
            a = _rms(xbuf[i % 2, rows, :], g_ref[...]).astype(BF16)
            a_scr[rows, :] = a
            ff = jnp.dot(a, wf_ref[...], preferred_element_type=F32) + fb_ref[...]
            lf_ref[rows, :] = _log_sigmoid(ff)
            _store_heads(z_ref, _silu(jnp.dot(a, w_ref[...], preferred_element_type=F32)), rows)

    @pl.when((j == 1) & (i < last))
    def _():
        x_copy(i + 1).start()

    def project():
        return jnp.dot(a_scr[...], w_ref[...], preferred_element_type=F32)

    _store_z_tiles(j, project, lbl_ref, z_ref, silu_tiles=(3,))

    @pl.when(j == nz)
    def _():
        @pl.when(i > 0)
        def _():
            tile_copy(1, v_hbm, i - 1).wait()

            @pl.when(opens_sequence(i - 1))
            def _():
                prefix_copy(1, pv_ref, v_hbm, i - 1).wait()
        stage[0] = project()
        tile_copy(0, k_hbm, i).start()

        @pl.when(opens_sequence(i))
        def _():
            prefix_copy(0, pk_ref, k_hbm, i).start()

    @pl.when(j == nz + 1)
    def _():
        stage[1] = project()
        tile_copy(1, v_hbm, i).start()
        tile_copy(0, k_hbm, i).wait()

        @pl.when(opens_sequence(i))
        def _():
            prefix_copy(1, pv_ref, v_hbm, i).start()
            prefix_copy(0, pk_ref, k_hbm, i).wait()

        @pl.when(i == last)
        def _():
            tile_copy(1, v_hbm, i).wait()

            @pl.when(opens_sequence(i))
            def _():
                prefix_copy(1, pv_ref, v_hbm, i).wait()


def _inproj_kv(x, gain, w_main, w_f, f_bias, lb_logits, prefix_k, prefix_v, *, nseq, nz):
    rows, d = x.shape
    n = w_main.shape[1]
    tn = n // (nz + 2)
    seq = rows // nseq
    plen, nh, _ = prefix_k.shape
    tm = _pick(seq, 1024)
    kern = functools.partial(_inproj_kv_kernel, nz=nz, tiles_per_seq=seq // tm)
    cache = jax.ShapeDtypeStruct((nseq, plen + seq, nh, HEAD_DIM), F32)
    return pl.pallas_call(
        kern,
        grid=(rows // tm, nz + 2),
        in_specs=[
            pl.BlockSpec(memory_space=pl.ANY),
            pl.BlockSpec((1, d), lambda i, j: (0, 0)),
            pl.BlockSpec((d, tn), lambda i, j: (0, j)),
            pl.BlockSpec((d, LANES), lambda i, j: (0, 0)),
            pl.BlockSpec((1, LANES), lambda i, j: (0, 0)),
            pl.BlockSpec(lb_logits.shape, lambda i, j: (0, 0)),
            pl.BlockSpec((plen, nh, HEAD_DIM), lambda i, j: (0, 0, 0)),
            pl.BlockSpec((plen, nh, HEAD_DIM), lambda i, j: (0, 0, 0)),
        ],
        out_specs=[
            pl.BlockSpec((nh, tm, HEAD_DIM), lambda i, j: (jnp.minimum(j, nz - 1), i, 0)),
            pl.BlockSpec((tm, LANES), lambda i, j: (i, 0)),
            pl.BlockSpec(memory_space=pl.ANY),
            pl.BlockSpec(memory_space=pl.ANY),
        ],
        out_shape=[
            jax.ShapeDtypeStruct((nz * nh, rows, HEAD_DIM), F32),
            jax.ShapeDtypeStruct((rows, LANES), F32),
            cache, cache,
        ],
        scratch_shapes=[pltpu.VMEM((tm, d), BF16), pltpu.VMEM((2, tm, tn), F32),
                        pltpu.VMEM((2, tm, d), F32),
                        pltpu.SemaphoreType.DMA((4,)), pltpu.SemaphoreType.DMA((2,))],
        compiler_params=_params(("arbitrary", "arbitrary")),
        name="inproj_kv",
    )(x, gain, w_main, w_f, f_bias, lb_logits, prefix_k, prefix_v)


def _split3(x):
    hi = x.astype(BF16)
    r = x - hi.astype(F32)
    mid = r.astype(BF16)
    lo = (r - mid.astype(F32)).astype(BF16)
    return hi, mid, lo


def _cumsum_kernel(x_ref, init_ref, c_ref, *, chain):
    g, nblk, blk = x_ref.shape
    nrow = g * nblk
    upper = (lax.broadcasted_iota(jnp.int32, (blk, blk), 0)
             <= lax.broadcasted_iota(jnp.int32, (blk, blk), 1)).astype(BF16)
    c = sum(jnp.dot(p, upper, preferred_element_type=F32)
            for p in _split3(x_ref[...].reshape(nrow, blk)))
    if chain:
        shift = nblk.bit_length() - 1
        ri = lax.broadcasted_iota(jnp.int32, (nrow, nrow), 0)
        ci = lax.broadcasted_iota(jnp.int32, (nrow, nrow), 1)
        earlier = jnp.where(ci < ri, 1.0, 0.0)
        before = jnp.where((ri >> shift) == (ci >> shift), earlier, 0.0).astype(BF16)
        tot = jnp.broadcast_to(c[:, blk - 1:blk], (nrow, blk))
        c = c + sum(jnp.dot(before, p, preferred_element_type=F32) for p in _split3(tot))
    init = jnp.broadcast_to(init_ref[...], (g, nblk, blk)).reshape(nrow, blk)
    c_ref[...] = (c + init).reshape(g, nblk, blk)


def _cumsum(x, init, *, chain):
    n, nblk, blk = x.shape
    assert not chain or nblk & (nblk - 1) == 0, "chained blocks per sequence must be a power of two"
    g = _pick(n, 32)
    ni = init.shape[1]
    return pl.pallas_call(
        functools.partial(_cumsum_kernel, chain=chain),
        grid=(n // g,),
        in_specs=[
            pl.BlockSpec((g, nblk, blk), lambda i: (i, 0, 0)),
            pl.BlockSpec((g, ni, blk), lambda i: (i, 0, 0)),
        ],
        out_specs=pl.BlockSpec((g, nblk, blk), lambda i: (i, 0, 0)),
        out_shape=jax.ShapeDtypeStruct((n, nblk, blk), F32),
        compiler_params=_params(("parallel",)),
        name="cumsum",
    )(x, init)


def _cumsum_time(lf, init, nh):
    nb, t = lf.shape[0], lf.shape[1]
    blk = _pick(t, 256)
    x = jnp.transpose(lf[:, :, :nh], (0, 2, 1))
    if t == blk:
        init3 = jnp.broadcast_to(init.reshape(1, nb * nh, 1), (1, nb * nh, blk))
        return _cumsum(x.reshape(1, nb * nh, blk), init3, chain=False).reshape(nb, nh, t)
    init3 = jnp.broadcast_to(init.reshape(nb * nh, 1, 1), (nb * nh, 1, blk))
    return _cumsum(x.reshape(nb * nh, t // blk, blk), init3, chain=True).reshape(nb, nh, t)


def _block_mid(pre, half):
    c = pre.shape[0]
    if half >= 8:
        x3 = pre.reshape(c // (2 * half), 2 * half, HEAD_DIM)
        return jnp.broadcast_to(x3[:, half - 1:half, :], x3.shape).reshape(c, HEAD_DIM)
    x3 = pre.reshape(c // 8, 8, HEAD_DIM)
    sub = lax.broadcasted_iota(jnp.int32, x3.shape, 1)
    mid = None
    for blk in range(8 // (2 * half)):
        src = 2 * half * blk + half - 1
        piece = jnp.broadcast_to(x3[:, src:src + 1, :], x3.shape)
        mid = piece if mid is None else jnp.where(sub >= 2 * half * blk, piece, mid)
    return mid.reshape(c, HEAD_DIM)


def _hgrn_masks(c):
    rows = lax.broadcasted_iota(jnp.int32, (c, HEAD_DIM), 0)
    ri = lax.broadcasted_iota(jnp.int32, (c, c), 0)
    ci = lax.broadcasted_iota(jnp.int32, (c, c), 1)
    levels = c.bit_length() - 1
    upper = [(rows & (1 << (lvl - 1))) != 0 for lvl in range(1, levels + 1)]
    owner = jnp.where(ri >= ci, 32 - lax.clz(ri ^ ci), -1)
    return owner, upper


def _hgrn_chunk(q, g2, hi, st, masks):
    c = q.shape[0]
    owner, upper_masks = masks
    k = 1.0 - jnp.exp2(g2)

    scores = jnp.where(owner == 0, jnp.sum(q * k, axis=-1, keepdims=True), 0.0)
    pre = g2
    for lvl, upper in enumerate(upper_masks, start=1):
        mid = _block_mid(pre, 1 << (lvl - 1))
        w = jnp.exp2(jnp.where(upper, pre, mid - pre))
        x = (jnp.where(upper, q, k) * w).astype(BF16)
        s_l = lax.dot_general(x, x, (((1,), (1,)), ((), ())), preferred_element_type=F32)
        scores = jnp.where(owner == lvl, s_l, scores)
        pre = jnp.where(upper, pre + mid, pre)
    cum = pre
    v = hi.astype(BF16)
    qe = (q * jnp.exp2(cum)).astype(BF16)
    o = (lax.dot_general(qe, st.astype(BF16), (((1,), (1,)), ((), ())), preferred_element_type=F32)
         + jnp.dot(scores.astype(BF16), v, preferred_element_type=F32))
    last = cum[c - 1:c, :]
    kd = (k * jnp.exp2(last - cum)).astype(BF16)
    st_new = st * jnp.exp2(last) + lax.dot_general(
        v, kd, (((0,), (0,)), ((), ())), preferred_element_type=F32)
    return o, st_new


def _hgrn_kernel(hq_ref, hf_ref, hi_ref, hg_ref, gain_ref, s0_ref,
                 o_ref, sfin_ref, st_scr, *, chunk):
    ti = pl.program_id(2)
    hps = st_scr.shape[0]

    @pl.when(ti == 0)
    def _():
        st_scr[...] = s0_ref[0]

    tt = hq_ref.shape[1]
    masks = _hgrn_masks(chunk)
    for hh in range(hps):
        gain = gain_ref[hh]
        st = st_scr[hh]
        for cidx in range(tt // chunk):
            sl = slice(cidx * chunk, (cidx + 1) * chunk)
            o, st = _hgrn_chunk(hq_ref[hh, sl, :], hf_ref[hh, sl, :], hi_ref[hh, sl, :], st, masks)
            o_ref[hh, sl, :] = (_rms(o, gain) * hg_ref[hh, sl, :]).astype(o_ref.dtype)
        st_scr[hh] = st

        @pl.when(ti == pl.num_programs(2) - 1)
        def _():
            sfin_ref[0, hh] = st


def _hgrn(z, nb, gain, s0t, *, s0_shared, chunk, tt, hps):
    rows = z.shape[1]
    t = rows // nb
    nh = gain.shape[0]
    ng = nh // hps
    nt = t // tt
    s0_map = (lambda b, h, i: (0, h, 0, 0)) if s0_shared else (lambda b, h, i: (b, h, 0, 0))

    def col(tile):
        return pl.BlockSpec((hps, tt, HEAD_DIM), lambda b, h, i: (tile * ng + h, b * nt + i, 0))

    return pl.pallas_call(
        functools.partial(_hgrn_kernel, chunk=chunk),
        grid=(nb, ng, nt),
        in_specs=[
            col(0), col(1), col(2), col(3),
            pl.BlockSpec((hps, 1, HEAD_DIM), lambda b, h, i: (h, 0, 0)),
            pl.BlockSpec((1, hps, HEAD_DIM, HEAD_DIM), s0_map),
        ],
        out_specs=[
            pl.BlockSpec((hps, tt, HEAD_DIM), lambda b, h, i: (h, b * nt + i, 0)),
            pl.BlockSpec((1, hps, HEAD_DIM, HEAD_DIM), lambda b, h, i: (b, h, 0, 0)),
        ],
        out_shape=[
            jax.ShapeDtypeStruct((nh, rows, HEAD_DIM), BF16),
            jax.ShapeDtypeStruct((nb, nh, HEAD_DIM, HEAD_DIM), F32),
        ],
        scratch_shapes=[pltpu.VMEM((hps, HEAD_DIM, HEAD_DIM), F32)],
        compiler_params=_params(("parallel", "parallel", "arbitrary")),
        name="hgrn",
    )(z, z, z, z, gain, s0t)


def _slice_prefetch(k_hbm, v_hbm, kbuf, vbuf, sems, lin, nslices, nheads):
    def copies(idx):
        seq_idx, head = idx // nheads, idx % nheads
        return (pltpu.make_async_copy(k_hbm.at[seq_idx, :, head, :], kbuf.at[idx % 2],
                                      sems.at[0, idx % 2]),
                pltpu.make_async_copy(v_hbm.at[seq_idx, :, head, :], vbuf.at[idx % 2],
                                      sems.at[1, idx % 2]))

    @pl.when(lin == 0)
    def _():
        for c in copies(lin):
            c.start()
    for c in copies(lin):
        c.wait()

    @pl.when(lin + 1 < nslices)
    def _():
        for c in copies(lin + 1):
            c.start()


def _fox_kernel(q_ref, cq_ref, kp_ref, vp_ref, ckp_ref, ks_ref, vs_ref, cks_ref, o_ref,
                *, pchunk):
    tq = q_ref.shape[1]
    nheads = ckp_ref.shape[1]
    plen = ckp_ref.shape[3]
    lane = lax.broadcasted_iota(jnp.int32, (tq, LANES), 1)
    causal = (lax.broadcasted_iota(jnp.int32, (tq, tq), 1)
              <= lax.broadcasted_iota(jnp.int32, (tq, tq), 0))
    starts = [pc * pchunk for pc in range(plen // pchunk)]
    for hd in range(nheads):
        q = q_ref[hd].astype(BF16)
        cq = jnp.sum(jnp.where(lane == hd, cq_ref[0], 0.0), axis=-1, keepdims=True)

        def head_rows(ref, start):
            return ref.at[0][pl.ds(start * nheads + hd, pchunk, stride=nheads), :]

        def logits(kb, ck):
            return lax.dot_general(q, kb.astype(BF16), (((1,), (1,)), ((), ())),
                                   preferred_element_type=F32) + (cq - ck) * LOG2E

        s_list = [logits(head_rows(kp_ref, st), ckp_ref[0, hd, :, st:st + pchunk]) for st in starts]
        s_list.append(jnp.where(causal, logits(ks_ref[hd], cks_ref[0, hd]), -jnp.inf))
        v_list = [head_rows(vp_ref, st) for st in starts] + [vs_ref[hd]]
        m = functools.reduce(jnp.maximum, [jnp.max(s, axis=-1, keepdims=True) for s in s_list])
        l = jnp.zeros((tq, 1), F32)
        acc = jnp.zeros((tq, HEAD_DIM), F32)
        for s, vb in zip(s_list, v_list):
            p = jnp.exp2(s - m)
            l = l + jnp.sum(p, axis=-1, keepdims=True)
            acc = acc + jnp.dot(p.astype(BF16), vb.astype(BF16), preferred_element_type=F32)
        o_ref[hd] = (acc / l).astype(o_ref.dtype)


def _fox(z, b_off, qtile, ktile, vtile, cq, kp, vp, ckp, cks):
    nb, t, _ = cq.shape
    nh = cks.shape[1]
    plen = kp.shape[1]
    kern = functools.partial(_fox_kernel, pchunk=_pick(plen, 512))
    return pl.pallas_call(
        kern,
        grid=(nb,),
        in_specs=[
            pl.BlockSpec((nh, t, HEAD_DIM), lambda b: (qtile, b + b_off, 0)),
            pl.BlockSpec((1, t, LANES), lambda b: (b, 0, 0)),
            pl.BlockSpec((1, plen * nh, HEAD_DIM), lambda b: (b, 0, 0)),
            pl.BlockSpec((1, plen * nh, HEAD_DIM), lambda b: (b, 0, 0)),
            pl.BlockSpec((1, nh, 1, plen), lambda b: (b, 0, 0, 0)),
            pl.BlockSpec((nh, t, HEAD_DIM), lambda b: (ktile, b + b_off, 0)),
            pl.BlockSpec((nh, t, HEAD_DIM), lambda b: (vtile, b + b_off, 0)),
            pl.BlockSpec((1, nh, 1, t), lambda b: (b, 0, 0, 0)),
        ],
        out_specs=pl.BlockSpec((nh, t, HEAD_DIM), lambda b: (0, b, 0)),
        out_shape=jax.ShapeDtypeStruct((nh, nb * t, HEAD_DIM), BF16),
        compiler_params=_params(("parallel",)),
        name="fox",
    )(z, cq, kp.reshape(nb, plen * nh, HEAD_DIM), vp.reshape(nb, plen * nh, HEAD_DIM),
      ckp, z, z, cks)


def _bias_pieces(c):
    hi = c.astype(BF16).astype(F32)
    r = c - hi
    mid = r.astype(BF16).astype(F32)
    return hi, mid, r - mid


def _bias_rows(c, ones_first):
    hi, mid, lo = _bias_pieces(c)
    sub = lax.broadcasted_iota(jnp.int32, (HEAD_DIM, c.shape[1]), 0)
    third = jnp.where(sub >= 3, sub - 3, sub)
    pieces = jnp.where(third == 0, hi, jnp.where(third == 1, mid, lo))
    first, second = (1.0, pieces) if ones_first else (pieces, 1.0)
    return jnp.where(sub < 3, first, jnp.where(sub < 6, second, 0.0))


def _bias_cols(c):
    hi, mid, lo = _bias_pieces(c)
    lane = lax.broadcasted_iota(jnp.int32, (c.shape[0], LANES), 1)
    pieces = jnp.where(lane == 3, hi, jnp.where(lane == 4, mid, lo))
    return jnp.where(lane < 3, 1.0, jnp.where(lane < 6, pieces, 0.0))


def _fox_prompt_kernel(q_ref, cq_ref, k_hbm, v_hbm, ck_ref, ckp_ref, o_ref,
                       kaug_scr, vt_scr, sa_scr, sb_scr, acc_scr, state_scr, kbuf, vbuf, sems):
    assert q_ref.shape[1] == 2 * sa_scr.shape[0]
    h = pl.program_id(1)
    qi = pl.program_id(2)
    nheads = pl.num_programs(1)
    nslices = pl.num_programs(0) * nheads
    lin = pl.program_id(0) * nheads + h
    slot = lin % 2
    tq = q_ref.shape[1]
    tk = sa_scr.shape[0]
    plen = ckp_ref.shape[1]
    t = kbuf.shape[1] - plen

    @pl.when(qi == 0)
    def _():
        _slice_prefetch(k_hbm, v_hbm, kbuf, vbuf, sems, lin, nslices, nheads)

        def prep(cix, carry):
            r0 = pl.multiple_of(cix * tk, tk)
            src = pl.ds(pl.multiple_of(plen + r0, 8), tk)
            ck = ck_ref[0, 0, :, pl.ds(r0, tk)] * LOG2E
            kaug_scr[pl.ds(r0, tk), HEAD_DIM:] = _bias_rows(-ck, True).astype(BF16).T
            kaug_scr[pl.ds(r0, tk), :HEAD_DIM] = kbuf[slot, src, :].astype(BF16)
            vt_scr[:, pl.ds(r0, tk)] = vbuf[slot, src, :].astype(BF16).T
            return carry
        lax.fori_loop(0, t // tk, prep, 0)

    qt = q_ref[0].astype(BF16).T
    augq = _bias_rows(cq_ref[0, 0] * LOG2E, False).astype(BF16)
    qaug = jnp.concatenate([qt, augq], axis=0)

    def row0(blk):
        return pl.multiple_of(blk * tk, tk)

    def colmax(s):
        return jnp.max(s, axis=0, keepdims=True)

    def stage(blk, s_scr):
        s = jnp.dot(kaug_scr[pl.ds(row0(blk), tk), :], qaug, preferred_element_type=F32)
        s_scr[...] = s
        return colmax(s)

    def pv_block(blk):
        return lambda p: jnp.dot(vt_scr[:, pl.ds(row0(blk), tk)], p, preferred_element_type=F32)

    def absorb(ml, blocks, cols=None):
        m, l = ml
        m_new = m
        for _, smax, _ in blocks:
            m_new = jnp.maximum(m_new, smax)
        alpha = jnp.exp2(m - m_new)
        l = alpha * l
        pv = None
        for s, _, pv_fn in blocks:
            p = jnp.exp2(s - m_new)
            l = l + jnp.sum(p, axis=0, keepdims=True)
            term = pv_fn(p.astype(BF16))
            pv = term if pv is None else pv + term
        where = (Ellipsis,) if cols is None else (slice(None), cols)
        acc_scr[where] = alpha * acc_scr[where] + pv
        return m_new, l

    acc_scr[...] = jnp.zeros_like(acc_scr)
    ml = (jnp.full((1, tq), NEG_BIG, F32), jnp.zeros((1, tq), F32))

    def body(pair, state):
        ma, ml = state
        mb = stage(2 * pair + 1, sb_scr)
        ml = absorb(ml, [(sa_scr[...], ma, pv_block(2 * pair))])
        ma = stage(2 * pair + 2, sa_scr)
        ml = absorb(ml, [(sb_scr[...], mb, pv_block(2 * pair + 1))])
        return ma, ml

    def two_pairs(i, state):
        return body(2 * i + 1, body(2 * i, state))

    ma, (m_run, l_run) = lax.fori_loop(0, qi // 2, two_pairs, (stage(0, sa_scr), ml))
    state_scr[0:1, :], state_scr[1:2, :], state_scr[2:3, :] = ma, m_run, l_run

    @pl.when(qi % 2 == 1)
    def _():
        ma, (m_run, l_run) = body(qi - 1, (state_scr[0:1, :], (state_scr[1:2, :], state_scr[2:3, :])))
        state_scr[1:2, :], state_scr[2:3, :] = m_run, l_run

    ml = (state_scr[1:2, :], state_scr[2:3, :])

    lane = lax.broadcasted_iota(jnp.int32, ckp_ref.shape[1:], 1)
    ckp = jnp.sum(jnp.where(lane == h, ckp_ref[0], 0.0), axis=-1, keepdims=True) * LOG2E
    kp = jnp.concatenate([kbuf[slot, :plen, :].astype(BF16), _bias_cols(-ckp).astype(BF16)], axis=1)
    vp = vbuf[slot, :plen, :].astype(BF16)

    def pv_prefix(p):
        return lax.dot_general(vp, p, (((0,), (0,)), ((), ())), preferred_element_type=F32)

    first, second = slice(0, tk), slice(tk, tq)
    tri = (lax.broadcasted_iota(jnp.int32, (tk, tk), 0)
           <= lax.broadcasted_iota(jnp.int32, (tk, tk), 1))
    s_p = jnp.dot(kp, qaug, preferred_element_type=F32)
    s_a = sa_scr[...]
    stage(2 * qi + 1, sb_scr)
    s_b = sb_scr[:, second]
    m_run, l_run = ml
    l_halves = []
    for cols, logit_blocks in (
            (first, [(s_p[:, first], pv_prefix),
                     (jnp.where(tri, s_a[:, first], -jnp.inf), pv_block(2 * qi))]),
            (second, [(s_p[:, second], pv_prefix),
                      (s_a[:, second], pv_block(2 * qi)),
                      (jnp.where(tri, s_b, -jnp.inf), pv_block(2 * qi + 1))])):
        blocks = [(s, colmax(s), pv_fn) for s, pv_fn in logit_blocks]
        _, l_half = absorb((m_run[:, cols], l_run[:, cols]), blocks, cols)
        l_halves.append(l_half)
    l = jnp.concatenate(l_halves, axis=1)
    o_ref[0] = (acc_scr[...] * (1.0 / l)).astype(o_ref.dtype).T


def _fox_prompt(z, qtile, k_cache, v_cache, c_row, cp_col, *, tq):
    nb, nh, _, t = c_row.shape
    nq = t // tq
    plen = cp_col.shape[1]
    return pl.pallas_call(
        _fox_prompt_kernel,
        grid=(nb, nh, nq),
        in_specs=[
            pl.BlockSpec((1, tq, HEAD_DIM), lambda b, h, i: (qtile * nh + h, b * nq + i, 0)),
            pl.BlockSpec((1, 1, 1, tq), lambda b, h, i: (b, h, 0, i)),
            pl.BlockSpec(memory_space=pl.ANY),
            pl.BlockSpec(memory_space=pl.ANY),
            pl.BlockSpec((1, 1, 1, t), lambda b, h, i: (b, h, 0, 0)),
            pl.BlockSpec((1, plen, LANES), lambda b, h, i: (0, 0, 0)),
        ],
        out_specs=pl.BlockSpec((1, tq, HEAD_DIM), lambda b, h, i: (h, b * nq + i, 0)),
        out_shape=jax.ShapeDtypeStruct((nh, nb * t, HEAD_DIM), BF16),
        scratch_shapes=[pltpu.VMEM((t, 2 * HEAD_DIM), BF16), pltpu.VMEM((HEAD_DIM, t), BF16),
                        pltpu.VMEM((tq // 2, tq), F32), pltpu.VMEM((tq // 2, tq), F32),
                        pltpu.VMEM((HEAD_DIM, tq), F32), pltpu.VMEM((8, tq), F32),
                        pltpu.VMEM((2, plen + t, HEAD_DIM), F32),
                        pltpu.VMEM((2, plen + t, HEAD_DIM), F32),
                        pltpu.SemaphoreType.DMA((2, 2))],
        compiler_params=_params(("arbitrary", "arbitrary", "arbitrary")),
        name="fox_prompt",
    )(z, c_row, k_cache, v_cache, c_row, cp_col)


def _outproj_kernel(h_ref, a_ref, b_ref, wa_ref, wb_ref, g_ref, o_ref):
    def heads_on_lanes(ref, rows):
        return jnp.concatenate([ref[hd, rows, :] for hd in range(ref.shape[0])], axis=1)

    for rows in _row_halves(o_ref.shape[0]):
        mix = (jnp.dot(heads_on_lanes(a_ref, rows), wa_ref[...], preferred_element_type=F32)
               + jnp.dot(heads_on_lanes(b_ref, rows), wb_ref[...], preferred_element_type=F32))
        o_ref[rows, :] = h_ref[rows, :] + _rms(mix, g_ref[...])


def _outproj(h, mix_a, mix_b, w_a, w_b, gain):
    rows, d = h.shape
    nh = mix_a.shape[0]
    ka, kb = w_a.shape[0], w_b.shape[0]
    tm = _pick(rows, 512)
    return pl.pallas_call(
        _outproj_kernel,
        grid=(rows // tm,),
        in_specs=[
            pl.BlockSpec((tm, d), lambda i: (i, 0)),
            pl.BlockSpec((nh, tm, HEAD_DIM), lambda i: (0, i, 0)),
            pl.BlockSpec((nh, tm, HEAD_DIM), lambda i: (0, i, 0)),
            pl.BlockSpec((ka, d), lambda i: (0, 0)),
            pl.BlockSpec((kb, d), lambda i: (0, 0)),
            pl.BlockSpec((1, d), lambda i: (0, 0)),
        ],
        out_specs=pl.BlockSpec((tm, d), lambda i: (i, 0)),
        out_shape=jax.ShapeDtypeStruct((rows, d), F32),
        compiler_params=_params(("parallel",)),
        name="outproj",
    )(h, mix_a, mix_b, w_a, w_b, gain)


def _mlp_kernel(h_hbm, gpre_ref, wu_ref, wd_ref, gpost_ref, o_ref, a_scr, acc_scr, hbuf, hsems):
    i = pl.program_id(0)
    j = pl.program_id(1)
    last = pl.num_programs(1) - 1
    tm = o_ref.shape[0]
    halves = _row_halves(tm)
    slot = i % 2

    def h_copy(tile):
        return pltpu.make_async_copy(h_hbm.at[pl.ds(tile * tm, tm), :], hbuf.at[tile % 2],
                                     hsems.at[tile % 2])

    def up_down(rows):
        u = jnp.maximum(jnp.dot(a_scr[rows, :], wu_ref[...], preferred_element_type=F32), 0.0)
        return jnp.dot((u * u).astype(BF16), wd_ref[...], preferred_element_type=F32)

    @pl.when(j == 0)
    def _():
        @pl.when(i == 0)
        def _():
            h_copy(i).start()
        h_copy(i).wait()
        for rows in halves:
            a_scr[rows, :] = _rms(hbuf[slot, rows, :], gpre_ref[...]).astype(BF16)
            acc_scr[rows, :] = up_down(rows)

    @pl.when((j == 1) & (i < pl.num_programs(0) - 1))
    def _():
        h_copy(i + 1).start()

    @pl.when((j > 0) & (j < last))
    def _():
        acc_scr[...] += up_down(slice(None))

    @pl.when(j == last)
    def _():
        for rows in halves:
            acc = acc_scr[rows, :] + up_down(rows)
            o_ref[rows, :] = hbuf[slot, rows, :] + _rms(acc, gpost_ref[...])


def _mlp(h, g_pre, w_up, w_down, g_post):
    rows, d = h.shape
    dff = w_up.shape[1]
    tm = _pick(rows, 512)
    tf = _pick(dff, 1024)
    assert dff // tf >= 2
    return pl.pallas_call(
        _mlp_kernel,
        grid=(rows // tm, dff // tf),
        in_specs=[
            pl.BlockSpec(memory_space=pl.ANY),
            pl.BlockSpec((1, d), lambda i, j: (0, 0)),
            pl.BlockSpec((d, tf), lambda i, j: (0, j)),
            pl.BlockSpec((tf, d), lambda i, j: (j, 0)),
            pl.BlockSpec((1, d), lambda i, j: (0, 0)),
        ],
        out_specs=pl.BlockSpec((tm, d), lambda i, j: (i, 0)),
        out_shape=jax.ShapeDtypeStruct((rows, d), F32),
        scratch_shapes=[pltpu.VMEM((tm, d), BF16), pltpu.VMEM((tm, d), F32),
                        pltpu.VMEM((2, tm, d), F32), pltpu.SemaphoreType.DMA((2,))],
        compiler_params=_params(("arbitrary", "arbitrary")),
        name="mlp",
    )(h, g_pre, w_up, w_down, g_post)


def _col_form(c_row):
    c_col = jnp.transpose(c_row, (0, 2, 1))
    return jnp.pad(c_col, ((0, 0), (0, 0), (0, LANES - c_col.shape[2])))


def kernel(x_prompt, x_sample, cache_fox_k, cache_fox_v, cache_fox_logf, state_hgrn, meta_tokens,
           g_mix_pre, w_in, hg_lb_logits, hg_norm_gain, fox_f_bias, w_out, g_mix_post, g_mlp_pre,
           w_up, w_down, g_mlp_post):
    bp, seq, d = x_prompt.shape
    bs, dseq, _ = x_sample.shape
    n_meta = meta_tokens.shape[0]
    past = cache_fox_k.shape[2]
    nh = cache_fox_k.shape[3]
    hw = nh * HEAD_DIM
    n_main = w_in.shape[2] - nh
    assert dseq == n_meta, "sample frames and meta tokens share the small-stream kernels"
    assert state_hgrn.shape[2] == nh and n_main == 7 * hw

    w_main = w_in[0, :, :n_main].astype(BF16)
    w_f = jnp.pad(w_in[0, :, n_main:], ((0, 0), (0, LANES - nh))).astype(BF16)
    f_bias = jnp.pad(fox_f_bias[0], (0, LANES - nh))[None, :]
    w_oa = w_out[0, :hw].astype(BF16)
    w_ob = w_out[0, hw:].astype(BF16)
    w_u = w_up[0].astype(BF16)
    w_d = w_down[0].astype(BF16)
    g_pre = g_mix_pre[0][None, :]
    g_post = g_mix_post[0][None, :]
    g_mpre = g_mlp_pre[0][None, :]
    g_mpost = g_mlp_post[0][None, :]
    hg_gain = hg_norm_gain[0].reshape(nh, 1, HEAD_DIM)
    qtile, ktile, vtile = 4, 5, 6

    ns = bs + 1
    xs = jnp.concatenate([meta_tokens, x_sample.reshape(bs * dseq, d)], axis=0)
    zs, lfs = _inproj(xs, g_pre, w_main, w_f, f_bias, hg_lb_logits)
    lfs3 = lfs.reshape(ns, dseq, LANES)

    def small_kv(tile):
        rows_first = jnp.transpose(zs[tile * nh:(tile + 1) * nh], (1, 0, 2))
        return rows_first.reshape(ns, dseq, nh, HEAD_DIM)

    k_small, v_small = small_kv(ktile), small_kv(vtile)

    c_past = _cumsum_time(cache_fox_logf[0], jnp.zeros((bs, nh), F32), nh)
    init_s = jnp.concatenate([jnp.zeros((1, nh), F32), c_past[:, :, past - 1]], axis=0)
    c_small = _cumsum_time(lfs3, init_s, nh)

    s0_small = jnp.concatenate([jnp.zeros((1,) + state_hgrn.shape[2:], F32), state_hgrn[0]], axis=0)
    s0_small_t = jnp.swapaxes(s0_small, -1, -2)
    mix_hg_s, sfin_s_t = _hgrn(zs, ns, hg_gain, s0_small_t, s0_shared=False,
                               chunk=dseq, tt=dseq, hps=nh)

    mix_fox_s = _fox(zs, 1, qtile, ktile, vtile, _col_form(c_small[1:]),
                     cache_fox_k[0], cache_fox_v[0], c_past[:, :, None, :], c_small[1:, :, None, :])

    xm = x_prompt.reshape(bp * seq, d)
    zm, lfm, k_cache, v_cache = _inproj_kv(xm, g_pre, w_main, w_f, f_bias, hg_lb_logits,
                                           k_small[0], v_small[0], nseq=bp, nz=ktile)
    lfm3 = lfm.reshape(bp, seq, LANES)
    c_meta = c_small[0:1]
    init_m = jnp.broadcast_to(c_meta[:, :, n_meta - 1], (bp, nh))
    c_main = _cumsum_time(lfm3, init_m, nh)

    mix_hg_m, sfin_m_t = _hgrn(zm, bp, hg_gain, sfin_s_t[0:1], s0_shared=True,
                               chunk=_pick(seq, 128), tt=_pick(seq, 2048), hps=1)
    mix_fox_m = _fox_prompt(zm, qtile, k_cache, v_cache, c_main[:, :, None, :], _col_form(c_meta),
                            tq=_pick(seq, 1024))

    h1m = _outproj(xm, mix_hg_m, mix_fox_m, w_oa, w_ob, g_post)
    y_prompt = _mlp(h1m, g_mpre, w_u, w_d, g_mpost).reshape(bp, seq, d)
    h1s = _outproj(x_sample.reshape(bs * dseq, d), mix_hg_s[:, n_meta:, :], mix_fox_s,
                   w_oa, w_ob, g_post)
    y_sample = _mlp(h1s, g_mpre, w_u, w_d, g_mpost).reshape(bs, dseq, d)

    meta_lf = jnp.broadcast_to(lfs3[0:1, :, :nh], (bp, n_meta, nh))
    lf_p = jnp.concatenate([meta_lf, lfm3[:, :, :nh]], axis=1)[None]
    return (y_prompt, y_sample,
            k_cache[None], v_cache[None],
            lf_p,
            jnp.swapaxes(sfin_m_t, -1, -2)[None],
            k_small[1:][None], v_small[1:][None],
            lfs3[1:, :, :nh][None],
            jnp.swapaxes(sfin_s_t[1:], -1, -2)[None])
```

```python
import functools

import jax
import jax.numpy as jnp
from jax import lax
from jax.experimental import pallas as pl
from jax.experimental.pallas import tpu as pltpu

F32 = jnp.float32
BF16 = jnp.bfloat16

EPS = 1e-6
HEAD_DIM = 128
LANES = 128
NEG_BIG = -1e30
LOG2E = 1.4426950408889634
VMEM_LIMIT = 56 * 1024 * 1024


def _pick(n, pref):
    if n <= pref:
        return n
    t = pref
    while n % t:
        t //= 2
    return t


def _params(sem):
    return pltpu.CompilerParams(dimension_semantics=sem, vmem_limit_bytes=VMEM_LIMIT)


def _log_sigmoid(x):
    return jnp.minimum(x, 0.0) - jnp.log1p(jnp.exp(-jnp.abs(x)))


def _rms(x, gain):
    ms = jnp.mean(x * x, axis=-1, keepdims=True)
    return x * lax.rsqrt(ms + EPS) * gain


def _silu(t):
    return t * jax.nn.sigmoid(t)


def _log2_forget(t, lbl_ref):
    lg = lbl_ref[...]
    e = jnp.exp(lg - jnp.max(lg, axis=0, keepdims=True))
    lb = e[0:1, :] / jnp.sum(e, axis=0, keepdims=True)
    return jnp.log(lb + (1.0 - lb) * jax.nn.sigmoid(t)) * LOG2E


def _store_heads(z_ref, tile, rows=slice(None)):
    for hd in range(z_ref.shape[0]):
        z_ref[hd, rows, :] = tile[:, hd * HEAD_DIM:(hd + 1) * HEAD_DIM]


def _row_halves(n):
    return (slice(0, n // 2), slice(n // 2, n)) if n % 16 == 0 else (slice(0, n),)


def _store_z_tiles(j, project, lbl_ref, z_ref, silu_tiles=(0, 3)):
    @pl.when(functools.reduce(jnp.logical_or, [j == t for t in silu_tiles]))
    def _():
        _store_heads(z_ref, _silu(project()))

    @pl.when(j == 1)
    def _():
        _store_heads(z_ref, _log2_forget(project(), lbl_ref))

    @pl.when((j == 2) | (j == 4))
    def _():
        _store_heads(z_ref, project() * jnp.where(j == 4, HEAD_DIM ** -0.5 * LOG2E, 1.0))


def _inproj_kernel(x_ref, g_ref, w_ref, wf_ref, fb_ref, lbl_ref, z_ref, lf_ref, a_scr):
    j = pl.program_id(1)

    @pl.when(j == 0)
    def _():
        a = _rms(x_ref[...], g_ref[...]).astype(BF16)
        a_scr[...] = a
        ff = jnp.dot(a, wf_ref[...], preferred_element_type=F32) + fb_ref[...]
        lf_ref[...] = _log_sigmoid(ff)

    def project():
        return jnp.dot(a_scr[...], w_ref[...], preferred_element_type=F32)

    _store_z_tiles(j, project, lbl_ref, z_ref)

    @pl.when(j >= 5)
    def _():
        _store_heads(z_ref, project())


def _inproj(x, gain, w_main, w_f, f_bias, lb_logits):
    rows, d = x.shape
    n = w_main.shape[1]
    tm = _pick(rows, 1024)
    tn = lb_logits.shape[1]
    nh = tn // HEAD_DIM
    return pl.pallas_call(
        _inproj_kernel,
        grid=(rows // tm, n // tn),
        in_specs=[
            pl.BlockSpec((tm, d), lambda i, j: (i, 0)),
            pl.BlockSpec((1, d), lambda i, j: (0, 0)),
            pl.BlockSpec((d, tn), lambda i, j: (0, j)),
            pl.BlockSpec((d, LANES), lambda i, j: (0, 0)),
            pl.BlockSpec((1, LANES), lambda i, j: (0, 0)),
            pl.BlockSpec(lb_logits.shape, lambda i, j: (0, 0)),
        ],
        out_specs=[
            pl.BlockSpec((nh, tm, HEAD_DIM), lambda i, j: (j, i, 0)),
            pl.BlockSpec((tm, LANES), lambda i, j: (i, 0)),
        ],
        out_shape=[
            jax.ShapeDtypeStruct((n // HEAD_DIM, rows, HEAD_DIM), F32),
            jax.ShapeDtypeStruct((rows, LANES), F32),
        ],
        scratch_shapes=[pltpu.VMEM((tm, d), BF16)],
        compiler_params=_params(("parallel", "arbitrary")),
        name="inproj",
    )(x, gain, w_main, w_f, f_bias, lb_logits)


def _inproj_kv_kernel(x_hbm, g_ref, w_ref, wf_ref, fb_ref, lbl_ref, pk_ref, pv_ref,
                      z_ref, lf_ref, k_hbm, v_hbm, a_scr, stage, xbuf, sems, xsems,
                      *, nz, tiles_per_seq):
    i = pl.program_id(0)
    j = pl.program_id(1)
    tm = a_scr.shape[0]
    plen = pk_ref.shape[0]
    last = pl.num_programs(0) - 1

    def x_copy(tile):
        return pltpu.make_async_copy(x_hbm.at[pl.ds(tile * tm, tm), :], xbuf.at[tile % 2],
                                     xsems.at[tile % 2])

    nh = k_hbm.shape[2]

    class _TileCopy:
        def __init__(self, slot, dst, tile):
            row = (tile % tiles_per_seq) * tm + plen
            self.copies = [
                pltpu.make_async_copy(stage.at[slot, :, pl.ds(hd * HEAD_DIM, HEAD_DIM)],
                                      dst.at[tile // tiles_per_seq, pl.ds(row, tm), hd, :],
                                      sems.at[slot])
                for hd in range(nh)]

        def start(self):
            for c in self.copies:
                c.start()

        def wait(self):
            for c in self.copies:
                c.wait()

    tile_copy = _TileCopy

    def prefix_copy(slot, src, dst, tile):
        return pltpu.make_async_copy(src, dst.at[tile // tiles_per_seq, pl.ds(0, plen), :, :],
                                     sems.at[2 + slot])

    def opens_sequence(tile):
        return tile % tiles_per_seq == 0

    @pl.when(j == 0)
    def _():
        @pl.when(i == 0)
        def _():
            x_copy(i).start()
        x_copy(i).wait()
        for rows in _row_halves(tm):
            a = _rms(xbuf[i % 2, rows, :], g_ref[...]).astype(BF16)
            a_scr[rows, :] = a
            ff = jnp.dot(a, wf_ref[...], preferred_element_type=F32) + fb_ref[...]
            lf_ref[rows, :] = _log_sigmoid(ff)
            _store_heads(z_ref, _silu(jnp.dot(a, w_ref[...], preferred_element_type=F32)), rows)

    @pl.when((j == 1) & (i < last))
    def _():
        x_copy(i + 1).start()

    def project():
        return jnp.dot(a_scr[...], w_ref[...], preferred_element_type=F32)

    _store_z_tiles(j, project, lbl_ref, z_ref, silu_tiles=(3,))

    @pl.when(j == nz)
    def _():
        @pl.when(i > 0)
        def _():
            tile_copy(1, v_hbm, i - 1).wait()

            @pl.when(opens_sequence(i - 1))
            def _():
                prefix_copy(1, pv_ref, v_hbm, i - 1).wait()
        stage[0] = project()
        tile_copy(0, k_hbm, i).start()

        @pl.when(opens_sequence(i))
        def _():
            prefix_copy(0, pk_ref, k_hbm, i).start()

    @pl.when(j == nz + 1)
    def _():
        stage[1] = project()
        tile_copy(1, v_hbm, i).start()
        tile_copy(0, k_hbm, i).wait()

        @pl.when(opens_sequence(i))
        def _():
            prefix_copy(1, pv_ref, v_hbm, i).start()
            prefix_copy(0, pk_ref, k_hbm, i).wait()

        @pl.when(i == last)
        def _():
            tile_copy(1, v_hbm, i).wait()

            @pl.when(opens_sequence(i))
            def _():
                prefix_copy(1, pv_ref, v_hbm, i).wait()


def _inproj_kv(x, gain, w_main, w_f, f_bias, lb_logits, prefix_k, prefix_v, *, nseq, nz):
    rows, d = x.shape
    n = w_main.shape[1]
    tn = n // (nz + 2)
    seq = rows // nseq
    plen, nh, _ = prefix_k.shape
    tm = _pick(seq, 1024)
    kern = functools.partial(_inproj_kv_kernel, nz=nz, tiles_per_seq=seq // tm)
    cache = jax.ShapeDtypeStruct((nseq, plen + seq, nh, HEAD_DIM), F32)
    return pl.pallas_call(
        kern,
        grid=(rows // tm, nz + 2),
        in_specs=[
            pl.BlockSpec(memory_space=pl.ANY),
            pl.BlockSpec((1, d), lambda i, j: (0, 0)),
            pl.BlockSpec((d, tn), lambda i, j: (0, j)),
            pl.BlockSpec((d, LANES), lambda i, j: (0, 0)),
            pl.BlockSpec((1, LANES), lambda i, j: (0, 0)),
            pl.BlockSpec(lb_logits.shape, lambda i, j: (0, 0)),
            pl.BlockSpec((plen, nh, HEAD_DIM), lambda i, j: (0, 0, 0)),
            pl.BlockSpec((plen, nh, HEAD_DIM), lambda i, j: (0, 0, 0)),
        ],
        out_specs=[
            pl.BlockSpec((nh, tm, HEAD_DIM), lambda i, j: (jnp.minimum(j, nz - 1), i, 0)),
            pl.BlockSpec((tm, LANES), lambda i, j: (i, 0)),
            pl.BlockSpec(memory_space=pl.ANY),
            pl.BlockSpec(memory_space=pl.ANY),
        ],
        out_shape=[
            jax.ShapeDtypeStruct((nz * nh, rows, HEAD_DIM), F32),
            jax.ShapeDtypeStruct((rows, LANES), F32),
            cache, cache,
        ],
        scratch_shapes=[pltpu.VMEM((tm, d), BF16), pltpu.VMEM((2, tm, tn), F32),
                        pltpu.VMEM((2, tm, d), F32),
                        pltpu.SemaphoreType.DMA((4,)), pltpu.SemaphoreType.DMA((2,))],
        compiler_params=_params(("arbitrary", "arbitrary")),
        name="inproj_kv",
    )(x, gain, w_main, w_f, f_bias, lb_logits, prefix_k, prefix_v)


def _split3(x):
    hi = x.astype(BF16)
    r = x - hi.astype(F32)
    mid = r.astype(BF16)
    lo = (r - mid.astype(F32)).astype(BF16)
    return hi, mid, lo


def _cumsum_kernel(x_ref, init_ref, c_ref, *, chain):
    g, nblk, blk = x_ref.shape
    nrow = g * nblk
    upper = (lax.broadcasted_iota(jnp.int32, (blk, blk), 0)
             <= lax.broadcasted_iota(jnp.int32, (blk, blk), 1)).astype(BF16)
    c = sum(jnp.dot(p, upper, preferred_element_type=F32)
            for p in _split3(x_ref[...].reshape(nrow, blk)))
    if chain:
        shift = nblk.bit_length() - 1
        ri = lax.broadcasted_iota(jnp.int32, (nrow, nrow), 0)
        ci = lax.broadcasted_iota(jnp.int32, (nrow, nrow), 1)
        earlier = jnp.where(ci < ri, 1.0, 0.0)
        before = jnp.where((ri >> shift) == (ci >> shift), earlier, 0.0).astype(BF16)
        tot = jnp.broadcast_to(c[:, blk - 1:blk], (nrow, blk))
        c = c + sum(jnp.dot(before, p, preferred_element_type=F32) for p in _split3(tot))
    init = jnp.broadcast_to(init_ref[...], (g, nblk, blk)).reshape(nrow, blk)
    c_ref[...] = (c + init).reshape(g, nblk, blk)


def _cumsum(x, init, *, chain):
    n, nblk, blk = x.shape
    assert not chain or nblk & (nblk - 1) == 0, "chained blocks per sequence must be a power of two"
    g = _pick(n, 32)
    ni = init.shape[1]
    return pl.pallas_call(
        functools.partial(_cumsum_kernel, chain=chain),
        grid=(n // g,),
        in_specs=[
            pl.BlockSpec((g, nblk, blk), lambda i: (i, 0, 0)),
            pl.BlockSpec((g, ni, blk), lambda i: (i, 0, 0)),
        ],
        out_specs=pl.BlockSpec((g, nblk, blk), lambda i: (i, 0, 0)),
        out_shape=jax.ShapeDtypeStruct((n, nblk, blk), F32),
        compiler_params=_params(("parallel",)),
        name="cumsum",
    )(x, init)


def _cumsum_time(lf, init, nh):
    nb, t = lf.shape[0], lf.shape[1]
    blk = _pick(t, 256)
    x = jnp.transpose(lf[:, :, :nh], (0, 2, 1))
    if t == blk:
        init3 = jnp.broadcast_to(init.reshape(1, nb * nh, 1), (1, nb * nh, blk))
        return _cumsum(x.reshape(1, nb * nh, blk), init3, chain=False).reshape(nb, nh, t)
    init3 = jnp.broadcast_to(init.reshape(nb * nh, 1, 1), (nb * nh, 1, blk))
    return _cumsum(x.reshape(nb * nh, t // blk, blk), init3, chain=True).reshape(nb, nh, t)


def _block_mid(pre, half):
    c = pre.shape[0]
    if half >= 8:
        x3 = pre.reshape(c // (2 * half), 2 * half, HEAD_DIM)
        return jnp.broadcast_to(x3[:, half - 1:half, :], x3.shape).reshape(c, HEAD_DIM)
    x3 = pre.reshape(c // 8, 8, HEAD_DIM)
    sub = lax.broadcasted_iota(jnp.int32, x3.shape, 1)
    mid = None
    for blk in range(8 // (2 * half)):
        src = 2 * half * blk + half - 1
        piece = jnp.broadcast_to(x3[:, src:src + 1, :], x3.shape)
        mid = piece if mid is None else jnp.where(sub >= 2 * half * blk, piece, mid)
    return mid.reshape(c, HEAD_DIM)


def _hgrn_masks(c):
    rows = lax.broadcasted_iota(jnp.int32, (c, HEAD_DIM), 0)
    ri = lax.broadcasted_iota(jnp.int32, (c, c), 0)
    ci = lax.broadcasted_iota(jnp.int32, (c, c), 1)
    levels = c.bit_length() - 1
    upper = [(rows & (1 << (lvl - 1))) != 0 for lvl in range(1, levels + 1)]
    owner = jnp.where(ri >= ci, 32 - lax.clz(ri ^ ci), -1)
    return owner, upper


def _hgrn_chunk(q, g2, hi, st, masks):
    c = q.shape[0]
    owner, upper_masks = masks
    k = 1.0 - jnp.exp2(g2)

    scores = jnp.where(owner == 0, jnp.sum(q * k, axis=-1, keepdims=True), 0.0)
    pre = g2
    for lvl, upper in enumerate(upper_masks, start=1):
        mid = _block_mid(pre, 1 << (lvl - 1))
        w = jnp.exp2(jnp.where(upper, pre, mid - pre))
        x = jnp.where(upper, q, k) * w
        s_l = jnp.dot(x.astype(BF16), x.T.astype(BF16), preferred_element_type=F32)
        scores = jnp.where(owner == lvl, s_l, scores)
        pre = jnp.where(upper, pre + mid, pre)
    cum = pre
    v = hi.astype(BF16)
    qe = (q * jnp.exp2(cum)).astype(BF16)
    o = (lax.dot_general(qe, st.astype(BF16), (((1,), (1,)), ((), ())), preferred_element_type=F32)
         + jnp.dot(scores.astype(BF16), v, preferred_element_type=F32))
    last = cum[c - 1:c, :]
    kd = (k * jnp.exp2(last - cum)).astype(BF16)
    st_new = st * jnp.exp2(last) + lax.dot_general(
        v, kd, (((0,), (0,)), ((), ())), preferred_element_type=F32)
    return o, st_new


def _hgrn_kernel(hq_ref, hf_ref, hi_ref, hg_ref, gain_ref, s0_ref,
                 o_ref, sfin_ref, st_scr, *, chunk):
    ti = pl.program_id(2)
    hps = st_scr.shape[0]

    @pl.when(ti == 0)
    def _():
        st_scr[...] = s0_ref[0]

    tt = hq_ref.shape[1]
    masks = _hgrn_masks(chunk)
    for hh in range(hps):
        gain = gain_ref[hh]
        st = st_scr[hh]
        for cidx in range(tt // chunk):
            sl = slice(cidx * chunk, (cidx + 1) * chunk)
            o, st = _hgrn_chunk(hq_ref[hh, sl, :], hf_ref[hh, sl, :], hi_ref[hh, sl, :], st, masks)
            o_ref[hh, sl, :] = (_rms(o, gain) * hg_ref[hh, sl, :]).astype(o_ref.dtype)
        st_scr[hh] = st

        @pl.when(ti == pl.num_programs(2) - 1)
        def _():
            sfin_ref[0, hh] = st


def _hgrn(z, nb, gain, s0t, *, s0_shared, chunk, tt, hps):
    rows = z.shape[1]
    t = rows // nb
    nh = gain.shape[0]
    ng = nh // hps
    nt = t // tt
    s0_map = (lambda b, h, i: (0, h, 0, 0)) if s0_shared else (lambda b, h, i: (b, h, 0, 0))

    def col(tile):
        return pl.BlockSpec((hps, tt, HEAD_DIM), lambda b, h, i: (tile * ng + h, b * nt + i, 0))

    return pl.pallas_call(
        functools.partial(_hgrn_kernel, chunk=chunk),
        grid=(nb, ng, nt),
        in_specs=[
            col(0), col(1), col(2), col(3),
            pl.BlockSpec((hps, 1, HEAD_DIM), lambda b, h, i: (h, 0, 0)),
            pl.BlockSpec((1, hps, HEAD_DIM, HEAD_DIM), s0_map),
        ],
        out_specs=[
            pl.BlockSpec((hps, tt, HEAD_DIM), lambda b, h, i: (h, b * nt + i, 0)),
            pl.BlockSpec((1, hps, HEAD_DIM, HEAD_DIM), lambda b, h, i: (b, h, 0, 0)),
        ],
        out_shape=[
            jax.ShapeDtypeStruct((nh, rows, HEAD_DIM), BF16),
            jax.ShapeDtypeStruct((nb, nh, HEAD_DIM, HEAD_DIM), F32),
        ],
        scratch_shapes=[pltpu.VMEM((hps, HEAD_DIM, HEAD_DIM), F32)],
        compiler_params=_params(("parallel", "parallel", "arbitrary")),
        name="hgrn",
    )(z, z, z, z, gain, s0t)


def _slice_prefetch(k_hbm, v_hbm, kbuf, vbuf, sems, lin, nslices, nheads):
    def copies(idx):
        seq_idx, head = idx // nheads, idx % nheads
        return (pltpu.make_async_copy(k_hbm.at[seq_idx, :, head, :], kbuf.at[idx % 2],
                                      sems.at[0, idx % 2]),
                pltpu.make_async_copy(v_hbm.at[seq_idx, :, head, :], vbuf.at[idx % 2],
                                      sems.at[1, idx % 2]))

    @pl.when(lin == 0)
    def _():
        for c in copies(lin):
            c.start()
    for c in copies(lin):
        c.wait()

    @pl.when(lin + 1 < nslices)
    def _():
        for c in copies(lin + 1):
            c.start()


def _fox_kernel(q_ref, cq_ref, kp_ref, vp_ref, ckp_ref, ks_ref, vs_ref, cks_ref, o_ref,
                *, pchunk):
    tq = q_ref.shape[1]
    nheads = ckp_ref.shape[1]
    plen = ckp_ref.shape[3]
    lane = lax.broadcasted_iota(jnp.int32, (tq, LANES), 1)
    causal = (lax.broadcasted_iota(jnp.int32, (tq, tq), 1)
              <= lax.broadcasted_iota(jnp.int32, (tq, tq), 0))
    starts = [pc * pchunk for pc in range(plen // pchunk)]
    for hd in range(nheads):
        q = q_ref[hd].astype(BF16)
        cq = jnp.sum(jnp.where(lane == hd, cq_ref[0], 0.0), axis=-1, keepdims=True)

        def head_rows(ref, start):
            return ref.at[0][pl.ds(start * nheads + hd, pchunk, stride=nheads), :]

        def logits(kb, ck):
            return lax.dot_general(q, kb.astype(BF16), (((1,), (1,)), ((), ())),
                                   preferred_element_type=F32) + (cq - ck) * LOG2E

        s_list = [logits(head_rows(kp_ref, st), ckp_ref[0, hd, :, st:st + pchunk]) for st in starts]
        s_list.append(jnp.where(causal, logits(ks_ref[hd], cks_ref[0, hd]), -jnp.inf))
        v_list = [head_rows(vp_ref, st) for st in starts] + [vs_ref[hd]]
        m = functools.reduce(jnp.maximum, [jnp.max(s, axis=-1, keepdims=True) for s in s_list])
        l = jnp.zeros((tq, 1), F32)
        acc = jnp.zeros((tq, HEAD_DIM), F32)
        for s, vb in zip(s_list, v_list):
            p = jnp.exp2(s - m)
            l = l + jnp.sum(p, axis=-1, keepdims=True)
            acc = acc + jnp.dot(p.astype(BF16), vb.astype(BF16), preferred_element_type=F32)
        o_ref[hd] = (acc / l).astype(o_ref.dtype)


def _fox(z, b_off, qtile, ktile, vtile, cq, kp, vp, ckp, cks):
    nb, t, _ = cq.shape
    nh = cks.shape[1]
    plen = kp.shape[1]
    kern = functools.partial(_fox_kernel, pchunk=_pick(plen, 512))
    return pl.pallas_call(
        kern,
        grid=(nb,),
        in_specs=[
            pl.BlockSpec((nh, t, HEAD_DIM), lambda b: (qtile, b + b_off, 0)),
            pl.BlockSpec((1, t, LANES), lambda b: (b, 0, 0)),
            pl.BlockSpec((1, plen * nh, HEAD_DIM), lambda b: (b, 0, 0)),
            pl.BlockSpec((1, plen * nh, HEAD_DIM), lambda b: (b, 0, 0)),
            pl.BlockSpec((1, nh, 1, plen), lambda b: (b, 0, 0, 0)),
            pl.BlockSpec((nh, t, HEAD_DIM), lambda b: (ktile, b + b_off, 0)),
            pl.BlockSpec((nh, t, HEAD_DIM), lambda b: (vtile, b + b_off, 0)),
            pl.BlockSpec((1, nh, 1, t), lambda b: (b, 0, 0, 0)),
        ],
        out_specs=pl.BlockSpec((nh, t, HEAD_DIM), lambda b: (0, b, 0)),
        out_shape=jax.ShapeDtypeStruct((nh, nb * t, HEAD_DIM), BF16),
        compiler_params=_params(("parallel",)),
        name="fox",
    )(z, cq, kp.reshape(nb, plen * nh, HEAD_DIM), vp.reshape(nb, plen * nh, HEAD_DIM),
      ckp, z, z, cks)


def _bias_pieces(c):
    hi = c.astype(BF16).astype(F32)
    r = c - hi
    mid = r.astype(BF16).astype(F32)
    return hi, mid, r - mid


def _bias_rows(c, ones_first):
    hi, mid, lo = _bias_pieces(c)
    sub = lax.broadcasted_iota(jnp.int32, (HEAD_DIM, c.shape[1]), 0)
    third = jnp.where(sub >= 3, sub - 3, sub)
    pieces = jnp.where(third == 0, hi, jnp.where(third == 1, mid, lo))
    first, second = (1.0, pieces) if ones_first else (pieces, 1.0)
    return jnp.where(sub < 3, first, jnp.where(sub < 6, second, 0.0))


def _bias_cols(c):
    hi, mid, lo = _bias_pieces(c)
    lane = lax.broadcasted_iota(jnp.int32, (c.shape[0], LANES), 1)
    pieces = jnp.where(lane == 3, hi, jnp.where(lane == 4, mid, lo))
    return jnp.where(lane < 3, 1.0, jnp.where(lane < 6, pieces, 0.0))


def _fox_prompt_kernel(q_ref, cq_ref, k_hbm, v_hbm, ck_ref, ckp_ref, o_ref,
                       kaug_scr, vt_scr, sa_scr, sb_scr, acc_scr, state_scr, kbuf, vbuf, sems):
    assert q_ref.shape[1] == 2 * sa_scr.shape[0]
    h = pl.program_id(1)
    qi = pl.program_id(2)
    nheads = pl.num_programs(1)
    nslices = pl.num_programs(0) * nheads
    lin = pl.program_id(0) * nheads + h
    slot = lin % 2
    tq = q_ref.shape[1]
    tk = sa_scr.shape[0]
    plen = ckp_ref.shape[1]
    t = kbuf.shape[1] - plen

    @pl.when(qi == 0)
    def _():
        _slice_prefetch(k_hbm, v_hbm, kbuf, vbuf, sems, lin, nslices, nheads)

        def prep(cix, carry):
            r0 = pl.multiple_of(cix * tk, tk)
            src = pl.ds(pl.multiple_of(plen + r0, 8), tk)
            ck = ck_ref[0, 0, :, pl.ds(r0, tk)] * LOG2E
            kaug_scr[pl.ds(r0, tk), HEAD_DIM:] = _bias_rows(-ck, True).astype(BF16).T
            kaug_scr[pl.ds(r0, tk), :HEAD_DIM] = kbuf[slot, src, :].astype(BF16)
            vt_scr[:, pl.ds(r0, tk)] = vbuf[slot, src, :].astype(BF16).T
            return carry
        lax.fori_loop(0, t // tk, prep, 0)

    qt = q_ref[0].astype(BF16).T
    augq = _bias_rows(cq_ref[0, 0] * LOG2E, False).astype(BF16)
    qaug = jnp.concatenate([qt, augq], axis=0)

    def row0(blk):
        return pl.multiple_of(blk * tk, tk)

    def colmax(s):
        return jnp.max(s, axis=0, keepdims=True)

    def stage(blk, s_scr):
        s = jnp.dot(kaug_scr[pl.ds(row0(blk), tk), :], qaug, preferred_element_type=F32)
        s_scr[...] = s
        return colmax(s)

    def pv_block(blk):
        return lambda p: jnp.dot(vt_scr[:, pl.ds(row0(blk), tk)], p, preferred_element_type=F32)

    def absorb(ml, blocks, cols=None):
        m, l = ml
        m_new = m
        for _, smax, _ in blocks:
            m_new = jnp.maximum(m_new, smax)
        alpha = jnp.exp2(m - m_new)
        l = alpha * l
        pv = None
        for s, _, pv_fn in blocks:
            p = jnp.exp2(s - m_new)
            l = l + jnp.sum(p, axis=0, keepdims=True)
            term = pv_fn(p.astype(BF16))
            pv = term if pv is None else pv + term
        where = (Ellipsis,) if cols is None else (slice(None), cols)
        acc_scr[where] = alpha * acc_scr[where] + pv
        return m_new, l

    acc_scr[...] = jnp.zeros_like(acc_scr)
    ml = (jnp.full((1, tq), NEG_BIG, F32), jnp.zeros((1, tq), F32))

    def body(pair, state):
        ma, ml = state
        mb = stage(2 * pair + 1, sb_scr)
        ml = absorb(ml, [(sa_scr[...], ma, pv_block(2 * pair))])
        ma = stage(2 * pair + 2, sa_scr)
        ml = absorb(ml, [(sb_scr[...], mb, pv_block(2 * pair + 1))])
        return ma, ml

    def two_pairs(i, state):
        return body(2 * i + 1, body(2 * i, state))

    ma, (m_run, l_run) = lax.fori_loop(0, qi // 2, two_pairs, (stage(0, sa_scr), ml))
    state_scr[0:1, :], state_scr[1:2, :], state_scr[2:3, :] = ma, m_run, l_run

    @pl.when(qi % 2 == 1)
    def _():
        ma, (m_run, l_run) = body(qi - 1, (state_scr[0:1, :], (state_scr[1:2, :], state_scr[2:3, :])))
        state_scr[1:2, :], state_scr[2:3, :] = m_run, l_run

    ml = (state_scr[1:2, :], state_scr[2:3, :])

    lane = lax.broadcasted_iota(jnp.int32, ckp_ref.shape[1:], 1)
    ckp = jnp.sum(jnp.where(lane == h, ckp_ref[0], 0.0), axis=-1, keepdims=True) * LOG2E
    kp = jnp.concatenate([kbuf[slot, :plen, :].astype(BF16), _bias_cols(-ckp).astype(BF16)], axis=1)
    vp = vbuf[slot, :plen, :].astype(BF16)

    def pv_prefix(p):
        return lax.dot_general(vp, p, (((0,), (0,)), ((), ())), preferred_element_type=F32)

    first, second = slice(0, tk), slice(tk, tq)
    tri = (lax.broadcasted_iota(jnp.int32, (tk, tk), 0)
           <= lax.broadcasted_iota(jnp.int32, (tk, tk), 1))
    s_p = jnp.dot(kp, qaug, preferred_element_type=F32)
    s_a = sa_scr[...]
    stage(2 * qi + 1, sb_scr)
    s_b = sb_scr[:, second]
    m_run, l_run = ml
    l_halves = []
    for cols, logit_blocks in (
            (first, [(s_p[:, first], pv_prefix),
                     (jnp.where(tri, s_a[:, first], -jnp.inf), pv_block(2 * qi))]),
            (second, [(s_p[:, second], pv_prefix),
                      (s_a[:, second], pv_block(2 * qi)),
                      (jnp.where(tri, s_b, -jnp.inf), pv_block(2 * qi + 1))])):
        blocks = [(s, colmax(s), pv_fn) for s, pv_fn in logit_blocks]
        _, l_half = absorb((m_run[:, cols], l_run[:, cols]), blocks, cols)
        l_halves.append(l_half)
    l = jnp.concatenate(l_halves, axis=1)
    o_ref[0] = (acc_scr[...] * (1.0 / l)).astype(o_ref.dtype).T


def _fox_prompt(z, qtile, k_cache, v_cache, c_row, cp_col, *, tq):
    nb, nh, _, t = c_row.shape
    nq = t // tq
    plen = cp_col.shape[1]
    return pl.pallas_call(
        _fox_prompt_kernel,
        grid=(nb, nh, nq),
        in_specs=[
            pl.BlockSpec((1, tq, HEAD_DIM), lambda b, h, i: (qtile * nh + h, b * nq + i, 0)),
            pl.BlockSpec((1, 1, 1, tq), lambda b, h, i: (b, h, 0, i)),
            pl.BlockSpec(memory_space=pl.ANY),
            pl.BlockSpec(memory_space=pl.ANY),
            pl.BlockSpec((1, 1, 1, t), lambda b, h, i: (b, h, 0, 0)),
            pl.BlockSpec((1, plen, LANES), lambda b, h, i: (0, 0, 0)),
        ],
        out_specs=pl.BlockSpec((1, tq, HEAD_DIM), lambda b, h, i: (h, b * nq + i, 0)),
        out_shape=jax.ShapeDtypeStruct((nh, nb * t, HEAD_DIM), BF16),
        scratch_shapes=[pltpu.VMEM((t, 2 * HEAD_DIM), BF16), pltpu.VMEM((HEAD_DIM, t), BF16),
                        pltpu.VMEM((tq // 2, tq), F32), pltpu.VMEM((tq // 2, tq), F32),
                        pltpu.VMEM((HEAD_DIM, tq), F32), pltpu.VMEM((8, tq), F32),
                        pltpu.VMEM((2, plen + t, HEAD_DIM), F32),
                        pltpu.VMEM((2, plen + t, HEAD_DIM), F32),
                        pltpu.SemaphoreType.DMA((2, 2))],
        compiler_params=_params(("arbitrary", "arbitrary", "arbitrary")),
        name="fox_prompt",
    )(z, c_row, k_cache, v_cache, c_row, cp_col)


def _outproj_kernel(h_ref, a_ref, b_ref, wa_ref, wb_ref, g_ref, o_ref):
    def heads_on_lanes(ref, rows):
        return jnp.concatenate([ref[hd, rows, :] for hd in range(ref.shape[0])], axis=1)

    for rows in _row_halves(o_ref.shape[0]):
        mix = (jnp.dot(heads_on_lanes(a_ref, rows), wa_ref[...], preferred_element_type=F32)
               + jnp.dot(heads_on_lanes(b_ref, rows), wb_ref[...], preferred_element_type=F32))
        o_ref[rows, :] = h_ref[rows, :] + _rms(mix, g_ref[...])


def _outproj(h, mix_a, mix_b, w_a, w_b, gain):
    rows, d = h.shape
    nh = mix_a.shape[0]
    ka, kb = w_a.shape[0], w_b.shape[0]
    tm = _pick(rows, 512)
    return pl.pallas_call(
        _outproj_kernel,
        grid=(rows // tm,),
        in_specs=[
            pl.BlockSpec((tm, d), lambda i: (i, 0)),
            pl.BlockSpec((nh, tm, HEAD_DIM), lambda i: (0, i, 0)),
            pl.BlockSpec((nh, tm, HEAD_DIM), lambda i: (0, i, 0)),
            pl.BlockSpec((ka, d), lambda i: (0, 0)),
            pl.BlockSpec((kb, d), lambda i: (0, 0)),
            pl.BlockSpec((1, d), lambda i: (0, 0)),
        ],
        out_specs=pl.BlockSpec((tm, d), lambda i: (i, 0)),
        out_shape=jax.ShapeDtypeStruct((rows, d), F32),
        compiler_params=_params(("parallel",)),
        name="outproj",
    )(h, mix_a, mix_b, w_a, w_b, gain)


def _mlp_kernel(h_hbm, gpre_ref, wu_ref, wd_ref, gpost_ref, o_ref, a_scr, acc_scr, hbuf, hsems):
    i = pl.program_id(0)
    j = pl.program_id(1)
    last = pl.num_programs(1) - 1
    tm = o_ref.shape[0]
    halves = _row_halves(tm)
    slot = i % 2

    def h_copy(tile):
        return pltpu.make_async_copy(h_hbm.at[pl.ds(tile * tm, tm), :], hbuf.at[tile % 2],
                                     hsems.at[tile % 2])

    def up_down(rows):
        u = jnp.maximum(jnp.dot(a_scr[rows, :], wu_ref[...], preferred_element_type=F32), 0.0)
        return jnp.dot((u * u).astype(BF16), wd_ref[...], preferred_element_type=F32)

    @pl.when(j == 0)
    def _():
        @pl.when(i == 0)
        def _():
            h_copy(i).start()
        h_copy(i).wait()
        for rows in halves:
            a_scr[rows, :] = _rms(hbuf[slot, rows, :], gpre_ref[...]).astype(BF16)
            acc_scr[rows, :] = up_down(rows)

    @pl.when((j == 1) & (i < pl.num_programs(0) - 1))
    def _():
        h_copy(i + 1).start()

    @pl.when((j > 0) & (j < last))
    def _():
        acc_scr[...] += up_down(slice(None))

    @pl.when(j == last)
    def _():
        for rows in halves:
            acc = acc_scr[rows, :] + up_down(rows)
            o_ref[rows, :] = hbuf[slot, rows, :] + _rms(acc, gpost_ref[...])


def _mlp(h, g_pre, w_up, w_down, g_post):
    rows, d = h.shape
    dff = w_up.shape[1]
    tm = _pick(rows, 512)
    tf = _pick(dff, 1024)
    assert dff // tf >= 2
    return pl.pallas_call(
        _mlp_kernel,
        grid=(rows // tm, dff // tf),
        in_specs=[
            pl.BlockSpec(memory_space=pl.ANY),
            pl.BlockSpec((1, d), lambda i, j: (0, 0)),
            pl.BlockSpec((d, tf), lambda i, j: (0, j)),
            pl.BlockSpec((tf, d), lambda i, j: (j, 0)),
            pl.BlockSpec((1, d), lambda i, j: (0, 0)),
        ],
        out_specs=pl.BlockSpec((tm, d), lambda i, j: (i, 0)),
        out_shape=jax.ShapeDtypeStruct((rows, d), F32),
        scratch_shapes=[pltpu.VMEM((tm, d), BF16), pltpu.VMEM((tm, d), F32),
                        pltpu.VMEM((2, tm, d), F32), pltpu.SemaphoreType.DMA((2,))],
        compiler_params=_params(("arbitrary", "arbitrary")),
        name="mlp",
    )(h, g_pre, w_up, w_down, g_post)


def _col_form(c_row):
    c_col = jnp.transpose(c_row, (0, 2, 1))
    return jnp.pad(c_col, ((0, 0), (0, 0), (0, LANES - c_col.shape[2])))


def kernel(x_prompt, x_sample, cache_fox_k, cache_fox_v, cache_fox_logf, state_hgrn, meta_tokens,
           g_mix_pre, w_in, hg_lb_logits, hg_norm_gain, fox_f_bias, w_out, g_mix_post, g_mlp_pre,
           w_up, w_down, g_mlp_post):
    bp, seq, d = x_prompt.shape
    bs, dseq, _ = x_sample.shape
    n_meta = meta_tokens.shape[0]
    past = cache_fox_k.shape[2]
    nh = cache_fox_k.shape[3]
    hw = nh * HEAD_DIM
    n_main = w_in.shape[2] - nh
    assert dseq == n_meta, "sample frames and meta tokens share the small-stream kernels"
    assert state_hgrn.shape[2] == nh and n_main == 7 * hw

    w_main = w_in[0, :, :n_main].astype(BF16)
    w_f = jnp.pad(w_in[0, :, n_main:], ((0, 0), (0, LANES - nh))).astype(BF16)
    f_bias = jnp.pad(fox_f_bias[0], (0, LANES - nh))[None, :]
    w_oa = w_out[0, :hw].astype(BF16)
    w_ob = w_out[0, hw:].astype(BF16)
    w_u = w_up[0].astype(BF16)
    w_d = w_down[0].astype(BF16)
    g_pre = g_mix_pre[0][None, :]
    g_post = g_mix_post[0][None, :]
    g_mpre = g_mlp_pre[0][None, :]
    g_mpost = g_mlp_post[0][None, :]
    hg_gain = hg_norm_gain[0].reshape(nh, 1, HEAD_DIM)
    qtile, ktile, vtile = 4, 5, 6

    ns = bs + 1
    xs = jnp.concatenate([meta_tokens, x_sample.reshape(bs * dseq, d)], axis=0)
    zs, lfs = _inproj(xs, g_pre, w_main, w_f, f_bias, hg_lb_logits)
    lfs3 = lfs.reshape(ns, dseq, LANES)

    def small_kv(tile):
        rows_first = jnp.transpose(zs[tile * nh:(tile + 1) * nh], (1, 0, 2))
        return rows_first.reshape(ns, dseq, nh, HEAD_DIM)

    k_small, v_small = small_kv(ktile), small_kv(vtile)

    c_past = _cumsum_time(cache_fox_logf[0], jnp.zeros((bs, nh), F32), nh)
    init_s = jnp.concatenate([jnp.zeros((1, nh), F32), c_past[:, :, past - 1]], axis=0)
    c_small = _cumsum_time(lfs3, init_s, nh)

    s0_small = jnp.concatenate([jnp.zeros((1,) + state_hgrn.shape[2:], F32), state_hgrn[0]], axis=0)
    s0_small_t = jnp.swapaxes(s0_small, -1, -2)
    mix_hg_s, sfin_s_t = _hgrn(zs, ns, hg_gain, s0_small_t, s0_shared=False,
                               chunk=dseq, tt=dseq, hps=nh)

    mix_fox_s = _fox(zs, 1, qtile, ktile, vtile, _col_form(c_small[1:]),
                     cache_fox_k[0], cache_fox_v[0], c_past[:, :, None, :], c_small[1:, :, None, :])

    xm = x_prompt.reshape(bp * seq, d)
    zm, lfm, k_cache, v_cache = _inproj_kv(xm, g_pre, w_main, w_f, f_bias, hg_lb_logits,
                                           k_small[0], v_small[0], nseq=bp, nz=ktile)
    lfm3 = lfm.reshape(bp, seq, LANES)
    c_meta = c_small[0:1]
    init_m = jnp.broadcast_to(c_meta[:, :, n_meta - 1], (bp, nh))
    c_main = _cumsum_time(lfm3, init_m, nh)

    mix_hg_m, sfin_m_t = _hgrn(zm, bp, hg_gain, sfin_s_t[0:1], s0_shared=True,
                               chunk=_pick(seq, 128), tt=_pick(seq, 2048), hps=1)
    mix_fox_m = _fox_prompt(zm, qtile, k_cache, v_cache, c_main[:, :, None, :], _col_form(c_meta),
                            tq=_pick(seq, 1024))

    h1m = _outproj(xm, mix_hg_m, mix_fox_m, w_oa, w_ob, g_post)
    y_prompt = _mlp(h1m, g_mpre, w_u, w_d, g_mpost).reshape(bp, seq, d)
    h1s = _outproj(x_sample.reshape(bs * dseq, d), mix_hg_s[:, n_meta:, :], mix_fox_s,
                   w_oa, w_ob, g_post)
    y_sample = _mlp(h1s, g_mpre, w_u, w_d, g_mpost).reshape(bs, dseq, d)

    meta_lf = jnp.broadcast_to(lfs3[0:1, :, :nh], (bp, n_meta, nh))
    lf_p = jnp.concatenate([meta_lf, lfm3[:, :, :nh]], axis=1)[None]
    return (y_prompt, y_sample,
            k_cache[None], v_cache[None],
            lf_p,
            jnp.swapaxes(sfin_m_t, -1, -2)[None],
            k_small[1:][None], v_small[1:][None],
            lfs3[1:, :, :nh][None],
            jnp.swapaxes(sfin_s_t[1:], -1, -2)[None])
```

```python
import functools

import jax
import jax.numpy as jnp
from jax import lax
from jax.experimental import pallas as pl
from jax.experimental.pallas import tpu as pltpu

F32 = jnp.float32
BF16 = jnp.bfloat16

EPS = 1e-6
HEAD_DIM = 128
LANES = 128
NEG_BIG = -1e30
LOG2E = 1.4426950408889634
VMEM_LIMIT = 56 * 1024 * 1024


def _pick(n, pref):
    if n <= pref:
        return n
    t = pref
    while n % t:
        t //= 2
    return t


def _params(sem):
    return pltpu.CompilerParams(dimension_semantics=sem, vmem_limit_bytes=VMEM_LIMIT)


def _log_sigmoid(x):
    return jnp.minimum(x, 0.0) - jnp.log1p(jnp.exp(-jnp.abs(x)))


def _rms(x, gain):
    ms = jnp.mean(x * x, axis=-1, keepdims=True)
    return x * lax.rsqrt(ms + EPS) * gain


def _silu(t):
    return t * jax.nn.sigmoid(t)


def _log2_forget(t, lbl_ref):
    lg = lbl_ref[...]
    e = jnp.exp(lg - jnp.max(lg, axis=0, keepdims=True))
    lb = e[0:1, :] / jnp.sum(e, axis=0, keepdims=True)
    return jnp.log(lb + (1.0 - lb) * jax.nn.sigmoid(t)) * LOG2E


def _store_heads(z_ref, tile, rows=slice(None)):
    for hd in range(z_ref.shape[0]):
        z_ref[hd, rows, :] = tile[:, hd * HEAD_DIM:(hd + 1) * HEAD_DIM]


def _row_halves(n):
    return (slice(0, n // 2), slice(n // 2, n)) if n % 16 == 0 else (slice(0, n),)


def _store_z_tiles(j, project, lbl_ref, z_ref, silu_tiles=(0, 3)):
    @pl.when(functools.reduce(jnp.logical_or, [j == t for t in silu_tiles]))
    def _():
        _store_heads(z_ref, _silu(project()))

    @pl.when(j == 1)
    def _():
        _store_heads(z_ref, _log2_forget(project(), lbl_ref))

    @pl.when((j == 2) | (j == 4))
    def _():
        _store_heads(z_ref, project() * jnp.where(j == 4, HEAD_DIM ** -0.5 * LOG2E, 1.0))


def _inproj_kernel(x_ref, g_ref, w_ref, wf_ref, fb_ref, lbl_ref, z_ref, lf_ref, a_scr):
    j = pl.program_id(1)

    @pl.when(j == 0)
    def _():
        a = _rms(x_ref[...], g_ref[...]).astype(BF16)
        a_scr[...] = a
        ff = jnp.dot(a, wf_ref[...], preferred_element_type=F32) + fb_ref[...]
        lf_ref[...] = _log_sigmoid(ff)

    def project():
        return jnp.dot(a_scr[...], w_ref[...], preferred_element_type=F32)

    _store_z_tiles(j, project, lbl_ref, z_ref)

    @pl.when(j >= 5)
    def _():
        _store_heads(z_ref, project())


def _inproj(x, gain, w_main, w_f, f_bias, lb_logits):
    rows, d = x.shape
    n = w_main.shape[1]
    tm = _pick(rows, 1024)
    tn = lb_logits.shape[1]
    nh = tn // HEAD_DIM
    return pl.pallas_call(
        _inproj_kernel,
        grid=(rows // tm, n // tn),
        in_specs=[
            pl.BlockSpec((tm, d), lambda i, j: (i, 0)),
            pl.BlockSpec((1, d), lambda i, j: (0, 0)),
            pl.BlockSpec((d, tn), lambda i, j: (0, j)),
            pl.BlockSpec((d, LANES), lambda i, j: (0, 0)),
            pl.BlockSpec((1, LANES), lambda i, j: (0, 0)),
            pl.BlockSpec(lb_logits.shape, lambda i, j: (0, 0)),
        ],
        out_specs=[
            pl.BlockSpec((nh, tm, HEAD_DIM), lambda i, j: (j, i, 0)),
            pl.BlockSpec((tm, LANES), lambda i, j: (i, 0)),
        ],
        out_shape=[
            jax.ShapeDtypeStruct((n // HEAD_DIM, rows, HEAD_DIM), F32),
            jax.ShapeDtypeStruct((rows, LANES), F32),
        ],
        scratch_shapes=[pltpu.VMEM((tm, d), BF16)],
        compiler_params=_params(("parallel", "arbitrary")),
        name="inproj",
    )(x, gain, w_main, w_f, f_bias, lb_logits)


def _inproj_kv_kernel(x_hbm, g_ref, w_ref, wf_ref, fb_ref, lbl_ref, pk_ref, pv_ref,
                      z_ref, lf_ref, k_hbm, v_hbm, a_scr, stage, xbuf, sems, xsems,
                      *, nz, tiles_per_seq):
    i = pl.program_id(0)
    j = pl.program_id(1)
    tm = a_scr.shape[0]
    plen = pk_ref.shape[0]
    last = pl.num_programs(0) - 1

    def x_copy(tile):
        return pltpu.make_async_copy(x_hbm.at[pl.ds(tile * tm, tm), :], xbuf.at[tile % 2],
                                     xsems.at[tile % 2])

    nh = k_hbm.shape[2]

    class _TileCopy:
        def __init__(self, slot, dst, tile):
            row = (tile % tiles_per_seq) * tm + plen
            self.copies = [
                pltpu.make_async_copy(stage.at[slot, :, pl.ds(hd * HEAD_DIM, HEAD_DIM)],
                                      dst.at[tile // tiles_per_seq, pl.ds(row, tm), hd, :],
                                      sems.at[slot])
                for hd in range(nh)]

        def start(self):
            for c in self.copies:
                c.start()

        def wait(self):
            for c in self.copies:
                c.wait()

    tile_copy = _TileCopy

    def prefix_copy(slot, src, dst, tile):
        return pltpu.make_async_copy(src, dst.at[tile // tiles_per_seq, pl.ds(0, plen), :, :],
                                     sems.at[2 + slot])

    def opens_sequence(tile):
        return tile % tiles_per_seq == 0

    @pl.when(j == 0)
    def _():
        @pl.when(i == 0)
        def _():
            x_copy(i).start()
        x_copy(i).wait()
        for rows in _row_halves(tm):
            a = _rms(xbuf[i % 2, rows, :], g_ref[...]).astype(BF16)
            a_scr[rows, :] = a
            ff = jnp.dot(a, wf_ref[...], preferred_element_type=F32) + fb_ref[...]
            lf_ref[rows, :] = _log_sigmoid(ff)
            _store_heads(z_ref, _silu(jnp.dot(a, w_ref[...], preferred_element_type=F32)), rows)

    @pl.when((j == 1) & (i < last))
    def _():
        x_copy(i + 1).start()

    def project():
        return jnp.dot(a_scr[...], w_ref[...], preferred_element_type=F32)

    _store_z_tiles(j, project, lbl_ref, z_ref, silu_tiles=(3,))

    @pl.when(j == nz)
    def _():
        @pl.when(i > 0)
        def _():
            tile_copy(1, v_hbm, i - 1).wait()

            @pl.when(opens_sequence(i - 1))
            def _():
                prefix_copy(1, pv_ref, v_hbm, i - 1).wait()
        stage[0] = project()
        tile_copy(0, k_hbm, i).start()

        @pl.when(opens_sequence(i))
        def _():
            prefix_copy(0, pk_ref, k_hbm, i).start()

    @pl.when(j == nz + 1)
    def _():
        stage[1] = project()
        tile_copy(1, v_hbm, i).start()
        tile_copy(0, k_hbm, i).wait()

        @pl.when(opens_sequence(i))
        def _():
            prefix_copy(1, pv_ref, v_hbm, i).start()
            prefix_copy(0, pk_ref, k_hbm, i).wait()

        @pl.when(i == last)
        def _():
            tile_copy(1, v_hbm, i).wait()

            @pl.when(opens_sequence(i))
            def _():
                prefix_copy(1, pv_ref, v_hbm, i).wait()


def _inproj_kv(x, gain, w_main, w_f, f_bias, lb_logits, prefix_k, prefix_v, *, nseq, nz):
    rows, d = x.shape
    n = w_main.shape[1]
    tn = n // (nz + 2)
    seq = rows // nseq
    plen, nh, _ = prefix_k.shape
    tm = _pick(seq, 1024)
    kern = functools.partial(_inproj_kv_kernel, nz=nz, tiles_per_seq=seq // tm)
    cache = jax.ShapeDtypeStruct((nseq, plen + seq, nh, HEAD_DIM), F32)
    return pl.pallas_call(
        kern,
        grid=(rows // tm, nz + 2),
        in_specs=[
            pl.BlockSpec(memory_space=pl.ANY),
            pl.BlockSpec((1, d), lambda i, j: (0, 0)),
            pl.BlockSpec((d, tn), lambda i, j: (0, j)),
            pl.BlockSpec((d, LANES), lambda i, j: (0, 0)),
            pl.BlockSpec((1, LANES), lambda i, j: (0, 0)),
            pl.BlockSpec(lb_logits.shape, lambda i, j: (0, 0)),
            pl.BlockSpec((plen, nh, HEAD_DIM), lambda i, j: (0, 0, 0)),
            pl.BlockSpec((plen, nh, HEAD_DIM), lambda i, j: (0, 0, 0)),
        ],
        out_specs=[
            pl.BlockSpec((nh, tm, HEAD_DIM), lambda i, j: (jnp.minimum(j, nz - 1), i, 0)),
            pl.BlockSpec((tm, LANES), lambda i, j: (i, 0)),
            pl.BlockSpec(memory_space=pl.ANY),
            pl.BlockSpec(memory_space=pl.ANY),
        ],
        out_shape=[
            jax.ShapeDtypeStruct((nz * nh, rows, HEAD_DIM), F32),
            jax.ShapeDtypeStruct((rows, LANES), F32),
            cache, cache,
        ],
        scratch_shapes=[pltpu.VMEM((tm, d), BF16), pltpu.VMEM((2, tm, tn), F32),
                        pltpu.VMEM((2, tm, d), F32),
                        pltpu.SemaphoreType.DMA((4,)), pltpu.SemaphoreType.DMA((2,))],
        compiler_params=_params(("arbitrary", "arbitrary")),
        name="inproj_kv",
    )(x, gain, w_main, w_f, f_bias, lb_logits, prefix_k, prefix_v)


def _split3(x):
    hi = x.astype(BF16)
    r = x - hi.astype(F32)
    mid = r.astype(BF16)
    lo = (r - mid.astype(F32)).astype(BF16)
    return hi, mid, lo


def _cumsum_kernel(x_ref, init_ref, c_ref, *, chain):
    g, nblk, blk = x_ref.shape
    nrow = g * nblk
    upper = (lax.broadcasted_iota(jnp.int32, (blk, blk), 0)
             <= lax.broadcasted_iota(jnp.int32, (blk, blk), 1)).astype(BF16)
    c = sum(jnp.dot(p, upper, preferred_element_type=F32)
            for p in _split3(x_ref[...].reshape(nrow, blk)))
    if chain:
        shift = nblk.bit_length() - 1
        ri = lax.broadcasted_iota(jnp.int32, (nrow, nrow), 0)
        ci = lax.broadcasted_iota(jnp.int32, (nrow, nrow), 1)
        earlier = jnp.where(ci < ri, 1.0, 0.0)
        before = jnp.where((ri >> shift) == (ci >> shift), earlier, 0.0).astype(BF16)
        tot = jnp.broadcast_to(c[:, blk - 1:blk], (nrow, blk))
        c = c + sum(jnp.dot(before, p, preferred_element_type=F32) for p in _split3(tot))
    init = jnp.broadcast_to(init_ref[...], (g, nblk, blk)).reshape(nrow, blk)
    c_ref[...] = (c + init).reshape(g, nblk, blk)


def _cumsum(x, init, *, chain):
    n, nblk, blk = x.shape
    assert not chain or nblk & (nblk - 1) == 0, "chained blocks per sequence must be a power of two"
    g = _pick(n, 32)
    ni = init.shape[1]
    return pl.pallas_call(
        functools.partial(_cumsum_kernel, chain=chain),
        grid=(n // g,),
        in_specs=[
            pl.BlockSpec((g, nblk, blk), lambda i: (i, 0, 0)),
            pl.BlockSpec((g, ni, blk), lambda i: (i, 0, 0)),
        ],
        out_specs=pl.BlockSpec((g, nblk, blk), lambda i: (i, 0, 0)),
        out_shape=jax.ShapeDtypeStruct((n, nblk, blk), F32),
        compiler_params=_params(("parallel",)),
        name="cumsum",
    )(x, init)


def _cumsum_time(lf, init, nh):
    nb, t = lf.shape[0], lf.shape[1]
    blk = _pick(t, 256)
    x = jnp.transpose(lf[:, :, :nh], (0, 2, 1))
    if t == blk:
        init3 = jnp.broadcast_to(init.reshape(1, nb * nh, 1), (1, nb * nh, blk))
        return _cumsum(x.reshape(1, nb * nh, blk), init3, chain=False).reshape(nb, nh, t)
    init3 = jnp.broadcast_to(init.reshape(nb * nh, 1, 1), (nb * nh, 1, blk))
    return _cumsum(x.reshape(nb * nh, t // blk, blk), init3, chain=True).reshape(nb, nh, t)


def _block_mid(pre, half):
    c = pre.shape[0]
    if half >= 8:
        x3 = pre.reshape(c // (2 * half), 2 * half, HEAD_DIM)
        return jnp.broadcast_to(x3[:, half - 1:half, :], x3.shape).reshape(c, HEAD_DIM)
    x3 = pre.reshape(c // 8, 8, HEAD_DIM)
    sub = lax.broadcasted_iota(jnp.int32, x3.shape, 1)
    mid = None
    for blk in range(8 // (2 * half)):
        src = 2 * half * blk + half - 1
        piece = jnp.broadcast_to(x3[:, src:src + 1, :], x3.shape)
        mid = piece if mid is None else jnp.where(sub >= 2 * half * blk, piece, mid)
    return mid.reshape(c, HEAD_DIM)


def _hgrn_masks(c):
    rows = lax.broadcasted_iota(jnp.int32, (c, HEAD_DIM), 0)
    ri = lax.broadcasted_iota(jnp.int32, (c, c), 0)
    ci = lax.broadcasted_iota(jnp.int32, (c, c), 1)
    levels = c.bit_length() - 1
    upper = [(rows & (1 << (lvl - 1))) != 0 for lvl in range(1, levels + 1)]
    owner = jnp.where(ri >= ci, 32 - lax.clz(ri ^ ci), -1)
    return owner, upper


def _times_transposed(a, b):
    if a.shape[0] % HEAD_DIM == 0 and b.shape[0] % HEAD_DIM == 0:
        return jnp.dot(a.astype(BF16), b.T.astype(BF16), preferred_element_type=F32)
    return lax.dot_general(a.astype(BF16), b.astype(BF16), (((1,), (1,)), ((), ())),
                           preferred_element_type=F32)


def _hgrn_chunk(q, g2, hi, st, masks):
    c = q.shape[0]
    owner, upper_masks = masks
    k = 1.0 - jnp.exp2(g2)

    scores = jnp.where(owner == 0, jnp.sum(q * k, axis=-1, keepdims=True), 0.0)
    pre = g2
    for lvl, upper in enumerate(upper_masks, start=1):
        mid = _block_mid(pre, 1 << (lvl - 1))
        w = jnp.exp2(jnp.where(upper, pre, mid - pre))
        x = jnp.where(upper, q, k) * w
        s_l = _times_transposed(x, x)
        scores = jnp.where(owner == lvl, s_l, scores)
        pre = jnp.where(upper, pre + mid, pre)
    cum = pre
    v = hi.astype(BF16)
    qe = (q * jnp.exp2(cum)).astype(BF16)
    o = (lax.dot_general(qe, st.astype(BF16), (((1,), (1,)), ((), ())), preferred_element_type=F32)
         + jnp.dot(scores.astype(BF16), v, preferred_element_type=F32))
    last = cum[c - 1:c, :]
    kd = (k * jnp.exp2(last - cum)).astype(BF16)
    st_new = st * jnp.exp2(last) + lax.dot_general(
        v, kd, (((0,), (0,)), ((), ())), preferred_element_type=F32)
    return o, st_new


def _hgrn_kernel(hq_ref, hf_ref, hi_ref, hg_ref, gain_ref, s0_ref,
                 o_ref, sfin_ref, st_scr, *, chunk):
    ti = pl.program_id(2)
    hps = st_scr.shape[0]

    @pl.when(ti == 0)
    def _():
        st_scr[...] = s0_ref[0]

    tt = hq_ref.shape[1]
    masks = _hgrn_masks(chunk)
    for hh in range(hps):
        gain = gain_ref[hh]
        st = st_scr[hh]
        for cidx in range(tt // chunk):
            sl = slice(cidx * chunk, (cidx + 1) * chunk)
            o, st = _hgrn_chunk(hq_ref[hh, sl, :], hf_ref[hh, sl, :], hi_ref[hh, sl, :], st, masks)
            o_ref[hh, sl, :] = (_rms(o, gain) * hg_ref[hh, sl, :]).astype(o_ref.dtype)
        st_scr[hh] = st

        @pl.when(ti == pl.num_programs(2) - 1)
        def _():
            sfin_ref[0, hh] = st


def _hgrn(z, nb, gain, s0t, *, s0_shared, chunk, tt, hps):
    rows = z.shape[1]
    t = rows // nb
    nh = gain.shape[0]
    ng = nh // hps
    nt = t // tt
    s0_map = (lambda b, h, i: (0, h, 0, 0)) if s0_shared else (lambda b, h, i: (b, h, 0, 0))

    def col(tile):
        return pl.BlockSpec((hps, tt, HEAD_DIM), lambda b, h, i: (tile * ng + h, b * nt + i, 0))

    return pl.pallas_call(
        functools.partial(_hgrn_kernel, chunk=chunk),
        grid=(nb, ng, nt),
        in_specs=[
            col(0), col(1), col(2), col(3),
            pl.BlockSpec((hps, 1, HEAD_DIM), lambda b, h, i: (h, 0, 0)),
            pl.BlockSpec((1, hps, HEAD_DIM, HEAD_DIM), s0_map),
        ],
        out_specs=[
            pl.BlockSpec((hps, tt, HEAD_DIM), lambda b, h, i: (h, b * nt + i, 0)),
            pl.BlockSpec((1, hps, HEAD_DIM, HEAD_DIM), lambda b, h, i: (b, h, 0, 0)),
        ],
        out_shape=[
            jax.ShapeDtypeStruct((nh, rows, HEAD_DIM), BF16),
            jax.ShapeDtypeStruct((nb, nh, HEAD_DIM, HEAD_DIM), F32),
        ],
        scratch_shapes=[pltpu.VMEM((hps, HEAD_DIM, HEAD_DIM), F32)],
        compiler_params=_params(("parallel", "parallel", "arbitrary")),
        name="hgrn",
    )(z, z, z, z, gain, s0t)


def _slice_prefetch(k_hbm, v_hbm, kbuf, vbuf, sems, lin, nslices, nheads):
    def copies(idx):
        seq_idx, head = idx // nheads, idx % nheads
        return (pltpu.make_async_copy(k_hbm.at[seq_idx, :, head, :], kbuf.at[idx % 2],
                                      sems.at[0, idx % 2]),
                pltpu.make_async_copy(v_hbm.at[seq_idx, :, head, :], vbuf.at[idx % 2],
                                      sems.at[1, idx % 2]))

    @pl.when(lin == 0)
    def _():
        for c in copies(lin):
            c.start()
    for c in copies(lin):
        c.wait()

    @pl.when(lin + 1 < nslices)
    def _():
        for c in copies(lin + 1):
            c.start()


def _fox_kernel(q_ref, cq_ref, kp_ref, vp_ref, ckp_ref, ks_ref, vs_ref, cks_ref, o_ref,
                *, pchunk):
    tq = q_ref.shape[1]
    nheads = ckp_ref.shape[1]
    plen = ckp_ref.shape[3]
    lane = lax.broadcasted_iota(jnp.int32, (tq, LANES), 1)
    causal = (lax.broadcasted_iota(jnp.int32, (tq, tq), 1)
              <= lax.broadcasted_iota(jnp.int32, (tq, tq), 0))
    starts = [pc * pchunk for pc in range(plen // pchunk)]
    for hd in range(nheads):
        q = q_ref[hd].astype(BF16)
        cq = jnp.sum(jnp.where(lane == hd, cq_ref[0], 0.0), axis=-1, keepdims=True)

        def head_rows(ref, start):
            return ref.at[0][pl.ds(start * nheads + hd, pchunk, stride=nheads), :]

        def logits(kb, ck):
            return lax.dot_general(q, kb.astype(BF16), (((1,), (1,)), ((), ())),
                                   preferred_element_type=F32) + (cq - ck) * LOG2E

        s_list = [logits(head_rows(kp_ref, st), ckp_ref[0, hd, :, st:st + pchunk]) for st in starts]
        s_list.append(jnp.where(causal, logits(ks_ref[hd], cks_ref[0, hd]), -jnp.inf))
        v_list = [head_rows(vp_ref, st) for st in starts] + [vs_ref[hd]]
        m = functools.reduce(jnp.maximum, [jnp.max(s, axis=-1, keepdims=True) for s in s_list])
        l = jnp.zeros((tq, 1), F32)
        acc = jnp.zeros((tq, HEAD_DIM), F32)
        for s, vb in zip(s_list, v_list):
            p = jnp.exp2(s - m)
            l = l + jnp.sum(p, axis=-1, keepdims=True)
            acc = acc + jnp.dot(p.astype(BF16), vb.astype(BF16), preferred_element_type=F32)
        o_ref[hd] = (acc / l).astype(o_ref.dtype)


def _fox(z, b_off, qtile, ktile, vtile, cq, kp, vp, ckp, cks):
    nb, t, _ = cq.shape
    nh = cks.shape[1]
    plen = kp.shape[1]
    kern = functools.partial(_fox_kernel, pchunk=_pick(plen, 512))
    return pl.pallas_call(
        kern,
        grid=(nb,),
        in_specs=[
            pl.BlockSpec((nh, t, HEAD_DIM), lambda b: (qtile, b + b_off, 0)),
            pl.BlockSpec((1, t, LANES), lambda b: (b, 0, 0)),
            pl.BlockSpec((1, plen * nh, HEAD_DIM), lambda b: (b, 0, 0)),
            pl.BlockSpec((1, plen * nh, HEAD_DIM), lambda b: (b, 0, 0)),
            pl.BlockSpec((1, nh, 1, plen), lambda b: (b, 0, 0, 0)),
            pl.BlockSpec((nh, t, HEAD_DIM), lambda b: (ktile, b + b_off, 0)),
            pl.BlockSpec((nh, t, HEAD_DIM), lambda b: (vtile, b + b_off, 0)),
            pl.BlockSpec((1, nh, 1, t), lambda b: (b, 0, 0, 0)),
        ],
        out_specs=pl.BlockSpec((nh, t, HEAD_DIM), lambda b: (0, b, 0)),
        out_shape=jax.ShapeDtypeStruct((nh, nb * t, HEAD_DIM), BF16),
        compiler_params=_params(("parallel",)),
        name="fox",
    )(z, cq, kp.reshape(nb, plen * nh, HEAD_DIM), vp.reshape(nb, plen * nh, HEAD_DIM),
      ckp, z, z, cks)


def _bias_pieces(c):
    hi = c.astype(BF16).astype(F32)
    r = c - hi
    mid = r.astype(BF16).astype(F32)
    return hi, mid, r - mid


def _bias_rows(c, ones_first):
    hi, mid, lo = _bias_pieces(c)
    sub = lax.broadcasted_iota(jnp.int32, (HEAD_DIM, c.shape[1]), 0)
    third = jnp.where(sub >= 3, sub - 3, sub)
    pieces = jnp.where(third == 0, hi, jnp.where(third == 1, mid, lo))
    first, second = (1.0, pieces) if ones_first else (pieces, 1.0)
    return jnp.where(sub < 3, first, jnp.where(sub < 6, second, 0.0))


def _bias_cols(c):
    hi, mid, lo = _bias_pieces(c)
    lane = lax.broadcasted_iota(jnp.int32, (c.shape[0], LANES), 1)
    pieces = jnp.where(lane == 3, hi, jnp.where(lane == 4, mid, lo))
    return jnp.where(lane < 3, 1.0, jnp.where(lane < 6, pieces, 0.0))


def _fox_prompt_kernel(q_ref, cq_ref, k_hbm, v_hbm, ck_ref, ckp_ref, o_ref,
                       kaug_scr, vt_scr, sa_scr, sb_scr, acc_scr, state_scr, kbuf, vbuf, sems):
    assert q_ref.shape[1] == 2 * sa_scr.shape[0]
    h = pl.program_id(1)
    qi = pl.program_id(2)
    nheads = pl.num_programs(1)
    nslices = pl.num_programs(0) * nheads
    lin = pl.program_id(0) * nheads + h
    slot = lin % 2
    tq = q_ref.shape[1]
    tk = sa_scr.shape[0]
    plen = ckp_ref.shape[1]
    t = kbuf.shape[1] - plen

    @pl.when(qi == 0)
    def _():
        _slice_prefetch(k_hbm, v_hbm, kbuf, vbuf, sems, lin, nslices, nheads)

        def prep(cix, carry):
            r0 = pl.multiple_of(cix * tk, tk)
            src = pl.ds(pl.multiple_of(plen + r0, 8), tk)
            ck = ck_ref[0, 0, :, pl.ds(r0, tk)] * LOG2E
            kaug_scr[pl.ds(r0, tk), HEAD_DIM:] = _bias_rows(-ck, True).astype(BF16).T
            kaug_scr[pl.ds(r0, tk), :HEAD_DIM] = kbuf[slot, src, :].astype(BF16)
            vt_scr[:, pl.ds(r0, tk)] = vbuf[slot, src, :].astype(BF16).T
            return carry
        lax.fori_loop(0, t // tk, prep, 0)

    qt = q_ref[0].astype(BF16).T
    augq = _bias_rows(cq_ref[0, 0] * LOG2E, False).astype(BF16)
    qaug = jnp.concatenate([qt, augq], axis=0)

    def row0(blk):
        return pl.multiple_of(blk * tk, tk)

    def colmax(s):
        return jnp.max(s, axis=0, keepdims=True)

    def stage(blk, s_scr):
        s = jnp.dot(kaug_scr[pl.ds(row0(blk), tk), :], qaug, preferred_element_type=F32)
        s_scr[...] = s
        return colmax(s)

    def pv_block(blk):
        return lambda p: jnp.dot(vt_scr[:, pl.ds(row0(blk), tk)], p, preferred_element_type=F32)

    def absorb(ml, blocks, cols=None):
        m, l = ml
        m_new = m
        for _, smax, _ in blocks:
            m_new = jnp.maximum(m_new, smax)
        alpha = jnp.exp2(m - m_new)
        l = alpha * l
        pv = None
        for s, _, pv_fn in blocks:
            p = jnp.exp2(s - m_new)
            l = l + jnp.sum(p, axis=0, keepdims=True)
            term = pv_fn(p.astype(BF16))
            pv = term if pv is None else pv + term
        where = (Ellipsis,) if cols is None else (slice(None), cols)
        acc_scr[where] = alpha * acc_scr[where] + pv
        return m_new, l

    acc_scr[...] = jnp.zeros_like(acc_scr)
    ml = (jnp.full((1, tq), NEG_BIG, F32), jnp.zeros((1, tq), F32))

    def body(pair, state):
        ma, ml = state
        mb = stage(2 * pair + 1, sb_scr)
        ml = absorb(ml, [(sa_scr[...], ma, pv_block(2 * pair))])
        ma = stage(2 * pair + 2, sa_scr)
        ml = absorb(ml, [(sb_scr[...], mb, pv_block(2 * pair + 1))])
        return ma, ml

    def two_pairs(i, state):
        return body(2 * i + 1, body(2 * i, state))

    ma, (m_run, l_run) = lax.fori_loop(0, qi // 2, two_pairs, (stage(0, sa_scr), ml))
    state_scr[0:1, :], state_scr[1:2, :], state_scr[2:3, :] = ma, m_run, l_run

    @pl.when(qi % 2 == 1)
    def _():
        ma, (m_run, l_run) = body(qi - 1, (state_scr[0:1, :], (state_scr[1:2, :], state_scr[2:3, :])))
        state_scr[1:2, :], state_scr[2:3, :] = m_run, l_run

    ml = (state_scr[1:2, :], state_scr[2:3, :])

    lane = lax.broadcasted_iota(jnp.int32, ckp_ref.shape[1:], 1)
    ckp = jnp.sum(jnp.where(lane == h, ckp_ref[0], 0.0), axis=-1, keepdims=True) * LOG2E
    kp = jnp.concatenate([kbuf[slot, :plen, :].astype(BF16), _bias_cols(-ckp).astype(BF16)], axis=1)
    vp = vbuf[slot, :plen, :].astype(BF16)

    def pv_prefix(p):
        return lax.dot_general(vp, p, (((0,), (0,)), ((), ())), preferred_element_type=F32)

    first, second = slice(0, tk), slice(tk, tq)
    tri = (lax.broadcasted_iota(jnp.int32, (tk, tk), 0)
           <= lax.broadcasted_iota(jnp.int32, (tk, tk), 1))
    s_p = jnp.dot(kp, qaug, preferred_element_type=F32)
    s_a = sa_scr[...]
    stage(2 * qi + 1, sb_scr)
    s_b = sb_scr[:, second]
    m_run, l_run = ml
    l_halves = []
    for cols, logit_blocks in (
            (first, [(s_p[:, first], pv_prefix),
                     (jnp.where(tri, s_a[:, first], -jnp.inf), pv_block(2 * qi))]),
            (second, [(s_p[:, second], pv_prefix),
                      (s_a[:, second], pv_block(2 * qi)),
                      (jnp.where(tri, s_b, -jnp.inf), pv_block(2 * qi + 1))])):
        blocks = [(s, colmax(s), pv_fn) for s, pv_fn in logit_blocks]
        _, l_half = absorb((m_run[:, cols], l_run[:, cols]), blocks, cols)
        l_halves.append(l_half)
    l = jnp.concatenate(l_halves, axis=1)
    o_ref[0] = (acc_scr[...] * (1.0 / l)).astype(o_ref.dtype).T


def _fox_prompt(z, qtile, k_cache, v_cache, c_row, cp_col, *, tq):
    nb, nh, _, t = c_row.shape
    nq = t // tq
    plen = cp_col.shape[1]
    return pl.pallas_call(
        _fox_prompt_kernel,
        grid=(nb, nh, nq),
        in_specs=[
            pl.BlockSpec((1, tq, HEAD_DIM), lambda b, h, i: (qtile * nh + h, b * nq + i, 0)),
            pl.BlockSpec((1, 1, 1, tq), lambda b, h, i: (b, h, 0, i)),
            pl.BlockSpec(memory_space=pl.ANY),
            pl.BlockSpec(memory_space=pl.ANY),
            pl.BlockSpec((1, 1, 1, t), lambda b, h, i: (b, h, 0, 0)),
            pl.BlockSpec((1, plen, LANES), lambda b, h, i: (0, 0, 0)),
        ],
        out_specs=pl.BlockSpec((1, tq, HEAD_DIM), lambda b, h, i: (h, b * nq + i, 0)),
        out_shape=jax.ShapeDtypeStruct((nh, nb * t, HEAD_DIM), BF16),
        scratch_shapes=[pltpu.VMEM((t, 2 * HEAD_DIM), BF16), pltpu.VMEM((HEAD_DIM, t), BF16),
                        pltpu.VMEM((tq // 2, tq), F32), pltpu.VMEM((tq // 2, tq), F32),
                        pltpu.VMEM((HEAD_DIM, tq), F32), pltpu.VMEM((8, tq), F32),
                        pltpu.VMEM((2, plen + t, HEAD_DIM), F32),
                        pltpu.VMEM((2, plen + t, HEAD_DIM), F32),
                        pltpu.SemaphoreType.DMA((2, 2))],
        compiler_params=_params(("arbitrary", "arbitrary", "arbitrary")),
        name="fox_prompt",
    )(z, c_row, k_cache, v_cache, c_row, cp_col)


def _outproj_kernel(h_ref, a_ref, b_ref, wa_ref, wb_ref, g_ref, o_ref):
    def heads_on_lanes(ref, rows):
        return jnp.concatenate([ref[hd, rows, :] for hd in range(ref.shape[0])], axis=1)

    for rows in _row_halves(o_ref.shape[0]):
        mix = (jnp.dot(heads_on_lanes(a_ref, rows), wa_ref[...], preferred_element_type=F32)
               + jnp.dot(heads_on_lanes(b_ref, rows), wb_ref[...], preferred_element_type=F32))
        o_ref[rows, :] = h_ref[rows, :] + _rms(mix, g_ref[...])


def _outproj(h, mix_a, mix_b, w_a, w_b, gain):
    rows, d = h.shape
    nh = mix_a.shape[0]
    ka, kb = w_a.shape[0], w_b.shape[0]
    tm = _pick(rows, 512)
    return pl.pallas_call(
        _outproj_kernel,
        grid=(rows // tm,),
        in_specs=[
            pl.BlockSpec((tm, d), lambda i: (i, 0)),
            pl.BlockSpec((nh, tm, HEAD_DIM), lambda i: (0, i, 0)),
            pl.BlockSpec((nh, tm, HEAD_DIM), lambda i: (0, i, 0)),
            pl.BlockSpec((ka, d), lambda i: (0, 0)),
            pl.BlockSpec((kb, d), lambda i: (0, 0)),
            pl.BlockSpec((1, d), lambda i: (0, 0)),
        ],
        out_specs=pl.BlockSpec((tm, d), lambda i: (i, 0)),
        out_shape=jax.ShapeDtypeStruct((rows, d), F32),
        compiler_params=_params(("parallel",)),
        name="outproj",
    )(h, mix_a, mix_b, w_a, w_b, gain)


def _mlp_kernel(h_hbm, gpre_ref, wu_ref, wd_ref, gpost_ref, o_ref, a_scr, acc_scr, hbuf, hsems):
    i = pl.program_id(0)
    j = pl.program_id(1)
    last = pl.num_programs(1) - 1
    tm = o_ref.shape[0]
    halves = _row_halves(tm)
    slot = i % 2

    def h_copy(tile):
        return pltpu.make_async_copy(h_hbm.at[pl.ds(tile * tm, tm), :], hbuf.at[tile % 2],
                                     hsems.at[tile % 2])

    def up_down(rows):
        u = jnp.maximum(jnp.dot(a_scr[rows, :], wu_ref[...], preferred_element_type=F32), 0.0)
        return jnp.dot((u * u).astype(BF16), wd_ref[...], preferred_element_type=F32)

    @pl.when(j == 0)
    def _():
        @pl.when(i == 0)
        def _():
            h_copy(i).start()
        h_copy(i).wait()
        for rows in halves:
            a_scr[rows, :] = _rms(hbuf[slot, rows, :], gpre_ref[...]).astype(BF16)
            acc_scr[rows, :] = up_down(rows)

    @pl.when((j == 1) & (i < pl.num_programs(0) - 1))
    def _():
        h_copy(i + 1).start()

    @pl.when((j > 0) & (j < last))
    def _():
        acc_scr[...] += up_down(slice(None))

    @pl.when(j == last)
    def _():
        for rows in halves:
            acc = acc_scr[rows, :] + up_down(rows)
            o_ref[rows, :] = hbuf[slot, rows, :] + _rms(acc, gpost_ref[...])


def _mlp(h, g_pre, w_up, w_down, g_post):
    rows, d = h.shape
    dff = w_up.shape[1]
    tm = _pick(rows, 512)
    tf = _pick(dff, 1024)
    assert dff // tf >= 2
    return pl.pallas_call(
        _mlp_kernel,
        grid=(rows // tm, dff // tf),
        in_specs=[
            pl.BlockSpec(memory_space=pl.ANY),
            pl.BlockSpec((1, d), lambda i, j: (0, 0)),
            pl.BlockSpec((d, tf), lambda i, j: (0, j)),
            pl.BlockSpec((tf, d), lambda i, j: (j, 0)),
            pl.BlockSpec((1, d), lambda i, j: (0, 0)),
        ],
        out_specs=pl.BlockSpec((tm, d), lambda i, j: (i, 0)),
        out_shape=jax.ShapeDtypeStruct((rows, d), F32),
        scratch_shapes=[pltpu.VMEM((tm, d), BF16), pltpu.VMEM((tm, d), F32),
                        pltpu.VMEM((2, tm, d), F32), pltpu.SemaphoreType.DMA((2,))],
        compiler_params=_params(("arbitrary", "arbitrary")),
        name="mlp",
    )(h, g_pre, w_up, w_down, g_post)


def _col_form(c_row):
    c_col = jnp.transpose(c_row, (0, 2, 1))
    return jnp.pad(c_col, ((0, 0), (0, 0), (0, LANES - c_col.shape[2])))


def kernel(x_prompt, x_sample, cache_fox_k, cache_fox_v, cache_fox_logf, state_hgrn, meta_tokens,
           g_mix_pre, w_in, hg_lb_logits, hg_norm_gain, fox_f_bias, w_out, g_mix_post, g_mlp_pre,
           w_up, w_down, g_mlp_post):
    bp, seq, d = x_prompt.shape
    bs, dseq, _ = x_sample.shape
    n_meta = meta_tokens.shape[0]
    past = cache_fox_k.shape[2]
    nh = cache_fox_k.shape[3]
    hw = nh * HEAD_DIM
    n_main = w_in.shape[2] - nh
    assert dseq == n_meta, "sample frames and meta tokens share the small-stream kernels"
    assert state_hgrn.shape[2] == nh and n_main == 7 * hw

    w_main = w_in[0, :, :n_main].astype(BF16)
    w_f = jnp.pad(w_in[0, :, n_main:], ((0, 0), (0, LANES - nh))).astype(BF16)
    f_bias = jnp.pad(fox_f_bias[0], (0, LANES - nh))[None, :]
    w_oa = w_out[0, :hw].astype(BF16)
    w_ob = w_out[0, hw:].astype(BF16)
    w_u = w_up[0].astype(BF16)
    w_d = w_down[0].astype(BF16)
    g_pre = g_mix_pre[0][None, :]
    g_post = g_mix_post[0][None, :]
    g_mpre = g_mlp_pre[0][None, :]
    g_mpost = g_mlp_post[0][None, :]
    hg_gain = hg_norm_gain[0].reshape(nh, 1, HEAD_DIM)
    qtile, ktile, vtile = 4, 5, 6

    ns = bs + 1
    xs = jnp.concatenate([meta_tokens, x_sample.reshape(bs * dseq, d)], axis=0)
    zs, lfs = _inproj(xs, g_pre, w_main, w_f, f_bias, hg_lb_logits)
    lfs3 = lfs.reshape(ns, dseq, LANES)

    def small_kv(tile):
        rows_first = jnp.transpose(zs[tile * nh:(tile + 1) * nh], (1, 0, 2))
        return rows_first.reshape(ns, dseq, nh, HEAD_DIM)

    k_small, v_small = small_kv(ktile), small_kv(vtile)

    c_past = _cumsum_time(cache_fox_logf[0], jnp.zeros((bs, nh), F32), nh)
    init_s = jnp.concatenate([jnp.zeros((1, nh), F32), c_past[:, :, past - 1]], axis=0)
    c_small = _cumsum_time(lfs3, init_s, nh)

    s0_small = jnp.concatenate([jnp.zeros((1,) + state_hgrn.shape[2:], F32), state_hgrn[0]], axis=0)
    s0_small_t = jnp.swapaxes(s0_small, -1, -2)
    mix_hg_s, sfin_s_t = _hgrn(zs, ns, hg_gain, s0_small_t, s0_shared=False,
                               chunk=dseq, tt=dseq, hps=nh)

    mix_fox_s = _fox(zs, 1, qtile, ktile, vtile, _col_form(c_small[1:]),
                     cache_fox_k[0], cache_fox_v[0], c_past[:, :, None, :], c_small[1:, :, None, :])

    xm = x_prompt.reshape(bp * seq, d)
    zm, lfm, k_cache, v_cache = _inproj_kv(xm, g_pre, w_main, w_f, f_bias, hg_lb_logits,
                                           k_small[0], v_small[0], nseq=bp, nz=ktile)
    lfm3 = lfm.reshape(bp, seq, LANES)
    c_meta = c_small[0:1]
    init_m = jnp.broadcast_to(c_meta[:, :, n_meta - 1], (bp, nh))
    c_main = _cumsum_time(lfm3, init_m, nh)

    mix_hg_m, sfin_m_t = _hgrn(zm, bp, hg_gain, sfin_s_t[0:1], s0_shared=True,
                               chunk=_pick(seq, 128), tt=_pick(seq, 2048), hps=1)
    mix_fox_m = _fox_prompt(zm, qtile, k_cache, v_cache, c_main[:, :, None, :], _col_form(c_meta),
                            tq=_pick(seq, 1024))

    h1m = _outproj(xm, mix_hg_m, mix_fox_m, w_oa, w_ob, g_post)
    y_prompt = _mlp(h1m, g_mpre, w_u, w_d, g_mpost).reshape(bp, seq, d)
    h1s = _outproj(x_sample.reshape(bs * dseq, d), mix_hg_s[:, n_meta:, :], mix_fox_s,
                   w_oa, w_ob, g_post)
    y_sample = _mlp(h1s, g_mpre, w_u, w_d, g_mpost).reshape(bs, dseq, d)

    meta_lf = jnp.broadcast_to(lfs3[0:1, :, :nh], (bp, n_meta, nh))
    lf_p = jnp.concatenate([meta_lf, lfm3[:, :, :nh]], axis=1)[None]
    return (y_prompt, y_sample,
            k_cache[None], v_cache[None],
            lf_p,
            jnp.swapaxes(sfin_m_t, -1, -2)[None],
            k_small[1:][None], v_small[1:][None],
            lfs3[1:, :, :nh][None],
            jnp.swapaxes(sfin_s_t[1:], -1, -2)[None])
```

```python
import functools

import jax
import jax.numpy as jnp
from jax import lax
from jax.experimental import pallas as pl
from jax.experimental.pallas import tpu as pltpu

F32 = jnp.float32
BF16 = jnp.bfloat16

EPS = 1e-6
HEAD_DIM = 128
LANES = 128
NEG_BIG = -1e30
LOG2E = 1.4426950408889634
VMEM_LIMIT = 56 * 1024 * 1024


def _pick(n, pref):
    if n <= pref:
        return n
    t = pref
    while n % t:
        t //= 2
    return t


def _params(sem):
    return pltpu.CompilerParams(dimension_semantics=sem, vmem_limit_bytes=VMEM_LIMIT)


def _log_sigmoid(x):
    return jnp.minimum(x, 0.0) - jnp.log1p(jnp.exp(-jnp.abs(x)))


def _rms(x, gain):
    ms = jnp.mean(x * x, axis=-1, keepdims=True)
    return x * lax.rsqrt(ms + EPS) * gain


def _silu(t):
    return t * jax.nn.sigmoid(t)


def _log2_forget(t, lbl_ref):
    lg = lbl_ref[...]
    e = jnp.exp(lg - jnp.max(lg, axis=0, keepdims=True))
    lb = e[0:1, :] / jnp.sum(e, axis=0, keepdims=True)
    return jnp.log(lb + (1.0 - lb) * jax.nn.sigmoid(t)) * LOG2E


def _store_heads(z_ref, tile, rows=slice(None)):
    for hd in range(z_ref.shape[0]):
        z_ref[hd, rows, :] = tile[:, hd * HEAD_DIM:(hd + 1) * HEAD_DIM]


def _row_halves(n):
    return (slice(0, n // 2), slice(n // 2, n)) if n % 16 == 0 else (slice(0, n),)


def _store_z_tiles(j, project, lbl_ref, z_ref, silu_tiles=(0, 3)):
    @pl.when(functools.reduce(jnp.logical_or, [j == t for t in silu_tiles]))
    def _():
        _store_heads(z_ref, _silu(project()))

    @pl.when(j == 1)
    def _():
        _store_heads(z_ref, _log2_forget(project(), lbl_ref))

    @pl.when((j == 2) | (j == 4))
    def _():
        _store_heads(z_ref, project() * jnp.where(j == 4, HEAD_DIM ** -0.5 * LOG2E, 1.0))


def _inproj_kernel(x_ref, g_ref, w_ref, wf_ref, fb_ref, lbl_ref, z_ref, lf_ref, a_scr):
    j = pl.program_id(1)

    @pl.when(j == 0)
    def _():
        a = _rms(x_ref[...], g_ref[...]).astype(BF16)
        a_scr[...] = a
        ff = jnp.dot(a, wf_ref[...], preferred_element_type=F32) + fb_ref[...]
        lf_ref[...] = _log_sigmoid(ff)

    def project():
        return jnp.dot(a_scr[...], w_ref[...], preferred_element_type=F32)

    _store_z_tiles(j, project, lbl_ref, z_ref)

    @pl.when(j >= 5)
    def _():
        _store_heads(z_ref, project())


def _inproj(x, gain, w_main, w_f, f_bias, lb_logits):
    rows, d = x.shape
    n = w_main.shape[1]
    tm = _pick(rows, 1024)
    tn = lb_logits.shape[1]
    nh = tn // HEAD_DIM
    return pl.pallas_call(
        _inproj_kernel,
        grid=(rows // tm, n // tn),
        in_specs=[
            pl.BlockSpec((tm, d), lambda i, j: (i, 0)),
            pl.BlockSpec((1, d), lambda i, j: (0, 0)),
            pl.BlockSpec((d, tn), lambda i, j: (0, j)),
            pl.BlockSpec((d, LANES), lambda i, j: (0, 0)),
            pl.BlockSpec((1, LANES), lambda i, j: (0, 0)),
            pl.BlockSpec(lb_logits.shape, lambda i, j: (0, 0)),
        ],
        out_specs=[
            pl.BlockSpec((nh, tm, HEAD_DIM), lambda i, j: (j, i, 0)),
            pl.BlockSpec((tm, LANES), lambda i, j: (i, 0)),
        ],
        out_shape=[
            jax.ShapeDtypeStruct((n // HEAD_DIM, rows, HEAD_DIM), F32),
            jax.ShapeDtypeStruct((rows, LANES), F32),
        ],
        scratch_shapes=[pltpu.VMEM((tm, d), BF16)],
        compiler_params=_params(("parallel", "arbitrary")),
        name="inproj",
    )(x, gain, w_main, w_f, f_bias, lb_logits)


def _inproj_kv_kernel(x_hbm, g_ref, w_ref, wf_ref, fb_ref, lbl_ref, pk_ref, pv_ref,
                      z_ref, lf_ref, k_hbm, v_hbm, a_scr, stage, xbuf, sems, xsems,
                      *, nz, tiles_per_seq):
    i = pl.program_id(0)
    j = pl.program_id(1)
    tm = a_scr.shape[0]
    plen = pk_ref.shape[0]
    last = pl.num_programs(0) - 1

    def x_copy(tile):
        return pltpu.make_async_copy(x_hbm.at[pl.ds(tile * tm, tm), :], xbuf.at[tile % 2],
                                     xsems.at[tile % 2])

    nh = k_hbm.shape[2]

    class _TileCopy:
        def __init__(self, slot, dst, tile):
            row = (tile % tiles_per_seq) * tm + plen
            self.copies = [
                pltpu.make_async_copy(stage.at[slot, :, pl.ds(hd * HEAD_DIM, HEAD_DIM)],
                                      dst.at[tile // tiles_per_seq, pl.ds(row, tm), hd, :],
                                      sems.at[slot])
                for hd in range(nh)]

        def start(self):
            for c in self.copies:
                c.start()

        def wait(self):
            for c in self.copies:
                c.wait()

    tile_copy = _TileCopy

    def prefix_copy(slot, src, dst, tile):
        return pltpu.make_async_copy(src, dst.at[tile // tiles_per_seq, pl.ds(0, plen), :, :],
                                     sems.at[2 + slot])

    def opens_sequence(tile):
        return tile % tiles_per_seq == 0

    @pl.when(j == 0)
    def _():
        @pl.when(i == 0)
        def _():
            x_copy(i).start()
        x_copy(i).wait()
        for rows in _row_halves(tm):
            a = _rms(xbuf[i % 2, rows, :], g_ref[...]).astype(BF16)
            a_scr[rows, :] = a
            ff = jnp.dot(a, wf_ref[...], preferred_element_type=F32) + fb_ref[...]
            lf_ref[rows, :] = _log_sigmoid(ff)
            _store_heads(z_ref, _silu(jnp.dot(a, w_ref[...], preferred_element_type=F32)), rows)

    @pl.when((j == 1) & (i < last))
    def _():
        x_copy(i + 1).start()

    def project():
        return jnp.dot(a_scr[...], w_ref[...], preferred_element_type=F32)

    _store_z_tiles(j, project, lbl_ref, z_ref, silu_tiles=(3,))

    @pl.when(j == nz)
    def _():
        @pl.when(i > 0)
        def _():
            tile_copy(1, v_hbm, i - 1).wait()

            @pl.when(opens_sequence(i - 1))
            def _():
                prefix_copy(1, pv_ref, v_hbm, i - 1).wait()
        stage[0] = project()
        tile_copy(0, k_hbm, i).start()

        @pl.when(opens_sequence(i))
        def _():
            prefix_copy(0, pk_ref, k_hbm, i).start()

    @pl.when(j == nz + 1)
    def _():
        stage[1] = project()
        tile_copy(1, v_hbm, i).start()
        tile_copy(0, k_hbm, i).wait()

        @pl.when(opens_sequence(i))
        def _():
            prefix_copy(1, pv_ref, v_hbm, i).start()
            prefix_copy(0, pk_ref, k_hbm, i).wait()

        @pl.when(i == last)
        def _():
            tile_copy(1, v_hbm, i).wait()

            @pl.when(opens_sequence(i))
            def _():
                prefix_copy(1, pv_ref, v_hbm, i).wait()


def _inproj_kv(x, gain, w_main, w_f, f_bias, lb_logits, prefix_k, prefix_v, *, nseq, nz):
    rows, d = x.shape
    n = w_main.shape[1]
    tn = n // (nz + 2)
    seq = rows // nseq
    plen, nh, _ = prefix_k.shape
    tm = _pick(seq, 1024)
    kern = functools.partial(_inproj_kv_kernel, nz=nz, tiles_per_seq=seq // tm)
    cache = jax.ShapeDtypeStruct((nseq, plen + seq, nh, HEAD_DIM), F32)
    return pl.pallas_call(
        kern,
        grid=(rows // tm, nz + 2),
        in_specs=[
            pl.BlockSpec(memory_space=pl.ANY),
            pl.BlockSpec((1, d), lambda i, j: (0, 0)),
            pl.BlockSpec((d, tn), lambda i, j: (0, j)),
            pl.BlockSpec((d, LANES), lambda i, j: (0, 0)),
            pl.BlockSpec((1, LANES), lambda i, j: (0, 0)),
            pl.BlockSpec(lb_logits.shape, lambda i, j: (0, 0)),
            pl.BlockSpec((plen, nh, HEAD_DIM), lambda i, j: (0, 0, 0)),
            pl.BlockSpec((plen, nh, HEAD_DIM), lambda i, j: (0, 0, 0)),
        ],
        out_specs=[
            pl.BlockSpec((nh, tm, HEAD_DIM), lambda i, j: (jnp.minimum(j, nz - 1), i, 0)),
            pl.BlockSpec((tm, LANES), lambda i, j: (i, 0)),
            pl.BlockSpec(memory_space=pl.ANY),
            pl.BlockSpec(memory_space=pl.ANY),
        ],
        out_shape=[
            jax.ShapeDtypeStruct((nz * nh, rows, HEAD_DIM), F32),
            jax.ShapeDtypeStruct((rows, LANES), F32),
            cache, cache,
        ],
        scratch_shapes=[pltpu.VMEM((tm, d), BF16), pltpu.VMEM((2, tm, tn), F32),
                        pltpu.VMEM((2, tm, d), F32),
                        pltpu.SemaphoreType.DMA((4,)), pltpu.SemaphoreType.DMA((2,))],
        compiler_params=_params(("arbitrary", "arbitrary")),
        name="inproj_kv",
    )(x, gain, w_main, w_f, f_bias, lb_logits, prefix_k, prefix_v)


def _split3(x):
    hi = x.astype(BF16)
    r = x - hi.astype(F32)
    mid = r.astype(BF16)
    lo = (r - mid.astype(F32)).astype(BF16)
    return hi, mid, lo


def _cumsum_kernel(x_ref, init_ref, c_ref, *, chain):
    g, nblk, blk = x_ref.shape
    nrow = g * nblk
    upper = (lax.broadcasted_iota(jnp.int32, (blk, blk), 0)
             <= lax.broadcasted_iota(jnp.int32, (blk, blk), 1)).astype(BF16)
    c = sum(jnp.dot(p, upper, preferred_element_type=F32)
            for p in _split3(x_ref[...].reshape(nrow, blk)))
    if chain:
        shift = nblk.bit_length() - 1
        ri = lax.broadcasted_iota(jnp.int32, (nrow, nrow), 0)
        ci = lax.broadcasted_iota(jnp.int32, (nrow, nrow), 1)
        earlier = jnp.where(ci < ri, 1.0, 0.0)
        before = jnp.where((ri >> shift) == (ci >> shift), earlier, 0.0).astype(BF16)
        tot = jnp.broadcast_to(c[:, blk - 1:blk], (nrow, blk))
        c = c + sum(jnp.dot(before, p, preferred_element_type=F32) for p in _split3(tot))
    init = jnp.broadcast_to(init_ref[...], (g, nblk, blk)).reshape(nrow, blk)
    c_ref[...] = (c + init).reshape(g, nblk, blk)


def _cumsum(x, init, *, chain):
    n, nblk, blk = x.shape
    assert not chain or nblk & (nblk - 1) == 0, "chained blocks per sequence must be a power of two"
    g = _pick(n, 32)
    ni = init.shape[1]
    return pl.pallas_call(
        functools.partial(_cumsum_kernel, chain=chain),
        grid=(n // g,),
        in_specs=[
            pl.BlockSpec((g, nblk, blk), lambda i: (i, 0, 0)),
            pl.BlockSpec((g, ni, blk), lambda i: (i, 0, 0)),
        ],
        out_specs=pl.BlockSpec((g, nblk, blk), lambda i: (i, 0, 0)),
        out_shape=jax.ShapeDtypeStruct((n, nblk, blk), F32),
        compiler_params=_params(("parallel",)),
        name="cumsum",
    )(x, init)


def _cumsum_time(lf, init, nh):
    nb, t = lf.shape[0], lf.shape[1]
    blk = _pick(t, 256)
    x = jnp.transpose(lf[:, :, :nh], (0, 2, 1))
    if t == blk:
        init3 = jnp.broadcast_to(init.reshape(1, nb * nh, 1), (1, nb * nh, blk))
        return _cumsum(x.reshape(1, nb * nh, blk), init3, chain=False).reshape(nb, nh, t)
    init3 = jnp.broadcast_to(init.reshape(nb * nh, 1, 1), (nb * nh, 1, blk))
    return _cumsum(x.reshape(nb * nh, t // blk, blk), init3, chain=True).reshape(nb, nh, t)


def _block_mid(pre, half):
    c = pre.shape[0]
    if half >= 8:
        x3 = pre.reshape(c // (2 * half), 2 * half, HEAD_DIM)
        return jnp.broadcast_to(x3[:, half - 1:half, :], x3.shape).reshape(c, HEAD_DIM)
    x3 = pre.reshape(c // 8, 8, HEAD_DIM)
    sub = lax.broadcasted_iota(jnp.int32, x3.shape, 1)
    mid = None
    for blk in range(8 // (2 * half)):
        src = 2 * half * blk + half - 1
        piece = jnp.broadcast_to(x3[:, src:src + 1, :], x3.shape)
        mid = piece if mid is None else jnp.where(sub >= 2 * half * blk, piece, mid)
    return mid.reshape(c, HEAD_DIM)


def _hgrn_masks(c):
    rows = lax.broadcasted_iota(jnp.int32, (c, HEAD_DIM), 0)
    ri = lax.broadcasted_iota(jnp.int32, (c, c), 0)
    ci = lax.broadcasted_iota(jnp.int32, (c, c), 1)
    levels = c.bit_length() - 1
    upper = [(rows & (1 << (lvl - 1))) != 0 for lvl in range(1, levels + 1)]
    owner = jnp.where(ri >= ci, 32 - lax.clz(ri ^ ci), -1)
    return owner, upper


def _times_transposed(a, b):
    if a.shape[0] % HEAD_DIM == 0 and b.shape[0] % HEAD_DIM == 0:
        return jnp.dot(a.astype(BF16), b.T.astype(BF16), preferred_element_type=F32)
    return lax.dot_general(a.astype(BF16), b.astype(BF16), (((1,), (1,)), ((), ())),
                           preferred_element_type=F32)


def _hgrn_chunk(q, g2, hi, st, masks):
    c = q.shape[0]
    owner, upper_masks = masks
    k = 1.0 - jnp.exp2(g2)

    scores = jnp.where(owner == 0, jnp.sum(q * k, axis=-1, keepdims=True), 0.0)
    pre = g2
    for lvl, upper in enumerate(upper_masks, start=1):
        mid = _block_mid(pre, 1 << (lvl - 1))
        w = jnp.exp2(jnp.where(upper, pre, mid - pre))
        x = jnp.where(upper, q, k) * w
        s_l = _times_transposed(x, x)
        scores = jnp.where(owner == lvl, s_l, scores)
        pre = jnp.where(upper, pre + mid, pre)
    cum = pre
    v = hi.astype(BF16)
    qe = (q * jnp.exp2(cum)).astype(BF16)
    o = (lax.dot_general(qe, st.astype(BF16), (((1,), (1,)), ((), ())), preferred_element_type=F32)
         + jnp.dot(scores.astype(BF16), v, preferred_element_type=F32))
    last = cum[c - 1:c, :]
    kd = (k * jnp.exp2(last - cum)).astype(BF16)
    st_new = st * jnp.exp2(last) + lax.dot_general(
        v, kd, (((0,), (0,)), ((), ())), preferred_element_type=F32)
    return o, st_new


def _hgrn_kernel(hq_ref, hf_ref, hi_ref, hg_ref, gain_ref, s0_ref,
                 o_ref, sfin_ref, st_scr, *, chunk):
    ti = pl.program_id(2)
    hps = st_scr.shape[0]

    @pl.when(ti == 0)
    def _():
        st_scr[...] = s0_ref[0]

    tt = hq_ref.shape[1]
    masks = _hgrn_masks(chunk)
    for hh in range(hps):
        gain = gain_ref[hh]
        st = st_scr[hh]
        for cidx in range(tt // chunk):
            sl = slice(cidx * chunk, (cidx + 1) * chunk)
            o, st = _hgrn_chunk(hq_ref[hh, sl, :], hf_ref[hh, sl, :], hi_ref[hh, sl, :], st, masks)
            o_ref[hh, sl, :] = (_rms(o, gain) * hg_ref[hh, sl, :]).astype(o_ref.dtype)
        st_scr[hh] = st

        @pl.when(ti == pl.num_programs(2) - 1)
        def _():
            sfin_ref[0, hh] = st


def _hgrn(z, nb, gain, s0t, *, s0_shared, chunk, tt, hps):
    rows = z.shape[1]
    t = rows // nb
    nh = gain.shape[0]
    ng = nh // hps
    nt = t // tt
    s0_map = (lambda b, h, i: (0, h, 0, 0)) if s0_shared else (lambda b, h, i: (b, h, 0, 0))

    def col(tile):
        return pl.BlockSpec((hps, tt, HEAD_DIM), lambda b, h, i: (tile * ng + h, b * nt + i, 0))

    return pl.pallas_call(
        functools.partial(_hgrn_kernel, chunk=chunk),
        grid=(nb, ng, nt),
        in_specs=[
            col(0), col(1), col(2), col(3),
            pl.BlockSpec((hps, 1, HEAD_DIM), lambda b, h, i: (h, 0, 0)),
            pl.BlockSpec((1, hps, HEAD_DIM, HEAD_DIM), s0_map),
        ],
        out_specs=[
            pl.BlockSpec((hps, tt, HEAD_DIM), lambda b, h, i: (h, b * nt + i, 0)),
            pl.BlockSpec((1, hps, HEAD_DIM, HEAD_DIM), lambda b, h, i: (b, h, 0, 0)),
        ],
        out_shape=[
            jax.ShapeDtypeStruct((nh, rows, HEAD_DIM), BF16),
            jax.ShapeDtypeStruct((nb, nh, HEAD_DIM, HEAD_DIM), F32),
        ],
        scratch_shapes=[pltpu.VMEM((hps, HEAD_DIM, HEAD_DIM), F32)],
        compiler_params=_params(("parallel", "parallel", "arbitrary")),
        name="hgrn",
    )(z, z, z, z, gain, s0t)


def _slice_prefetch(k_hbm, v_hbm, kbuf, vbuf, sems, lin, nslices, nheads):
    def copies(idx):
        seq_idx, head = idx // nheads, idx % nheads
        return (pltpu.make_async_copy(k_hbm.at[seq_idx, :, head, :], kbuf.at[idx % 2],
                                      sems.at[0, idx % 2]),
                pltpu.make_async_copy(v_hbm.at[seq_idx, :, head, :], vbuf.at[idx % 2],
                                      sems.at[1, idx % 2]))

    @pl.when(lin == 0)
    def _():
        for c in copies(lin):
            c.start()
    for c in copies(lin):
        c.wait()

    @pl.when(lin + 1 < nslices)
    def _():
        for c in copies(lin + 1):
            c.start()


def _fox_kernel(q_ref, cq_ref, kp_ref, vp_ref, ckp_ref, ks_ref, vs_ref, cks_ref, o_ref,
                *, pchunk):
    tq = q_ref.shape[1]
    nheads = ckp_ref.shape[1]
    plen = ckp_ref.shape[3]
    lane = lax.broadcasted_iota(jnp.int32, (tq, LANES), 1)
    causal = (lax.broadcasted_iota(jnp.int32, (tq, tq), 1)
              <= lax.broadcasted_iota(jnp.int32, (tq, tq), 0))
    starts = [pc * pchunk for pc in range(plen // pchunk)]
    for hd in range(nheads):
        q = q_ref[hd].astype(BF16)
        cq = jnp.sum(jnp.where(lane == hd, cq_ref[0], 0.0), axis=-1, keepdims=True)

        def head_rows(ref, start):
            return ref.at[0][pl.ds(start * nheads + hd, pchunk, stride=nheads), :]

        def logits(kb, ck):
            return lax.dot_general(q, kb.astype(BF16), (((1,), (1,)), ((), ())),
                                   preferred_element_type=F32) + (cq - ck) * LOG2E

        s_list = [logits(head_rows(kp_ref, st), ckp_ref[0, hd, :, st:st + pchunk]) for st in starts]
        s_list.append(jnp.where(causal, logits(ks_ref[hd], cks_ref[0, hd]), -jnp.inf))
        v_list = [head_rows(vp_ref, st) for st in starts] + [vs_ref[hd]]
        m = functools.reduce(jnp.maximum, [jnp.max(s, axis=-1, keepdims=True) for s in s_list])
        l = jnp.zeros((tq, 1), F32)
        acc = jnp.zeros((tq, HEAD_DIM), F32)
        for s, vb in zip(s_list, v_list):
            p = jnp.exp2(s - m)
            l = l + jnp.sum(p, axis=-1, keepdims=True)
            acc = acc + jnp.dot(p.astype(BF16), vb.astype(BF16), preferred_element_type=F32)
        o_ref[hd] = (acc / l).astype(o_ref.dtype)


def _fox(z, b_off, qtile, ktile, vtile, cq, kp, vp, ckp, cks):
    nb, t, _ = cq.shape
    nh = cks.shape[1]
    plen = kp.shape[1]
    kern = functools.partial(_fox_kernel, pchunk=_pick(plen, 512))
    return pl.pallas_call(
        kern,
        grid=(nb,),
        in_specs=[
            pl.BlockSpec((nh, t, HEAD_DIM), lambda b: (qtile, b + b_off, 0)),
            pl.BlockSpec((1, t, LANES), lambda b: (b, 0, 0)),
            pl.BlockSpec((1, plen * nh, HEAD_DIM), lambda b: (b, 0, 0)),
            pl.BlockSpec((1, plen * nh, HEAD_DIM), lambda b: (b, 0, 0)),
            pl.BlockSpec((1, nh, 1, plen), lambda b: (b, 0, 0, 0)),
            pl.BlockSpec((nh, t, HEAD_DIM), lambda b: (ktile, b + b_off, 0)),
            pl.BlockSpec((nh, t, HEAD_DIM), lambda b: (vtile, b + b_off, 0)),
            pl.BlockSpec((1, nh, 1, t), lambda b: (b, 0, 0, 0)),
        ],
        out_specs=pl.BlockSpec((nh, t, HEAD_DIM), lambda b: (0, b, 0)),
        out_shape=jax.ShapeDtypeStruct((nh, nb * t, HEAD_DIM), BF16),
        compiler_params=_params(("parallel",)),
        name="fox",
    )(z, cq, kp.reshape(nb, plen * nh, HEAD_DIM), vp.reshape(nb, plen * nh, HEAD_DIM),
      ckp, z, z, cks)


def _bias_pieces(c):
    hi = c.astype(BF16).astype(F32)
    r = c - hi
    mid = r.astype(BF16).astype(F32)
    return hi, mid, r - mid


def _bias_rows(c, ones_first):
    hi, mid, lo = _bias_pieces(c)
    sub = lax.broadcasted_iota(jnp.int32, (HEAD_DIM, c.shape[1]), 0)
    third = jnp.where(sub >= 3, sub - 3, sub)
    pieces = jnp.where(third == 0, hi, jnp.where(third == 1, mid, lo))
    first, second = (1.0, pieces) if ones_first else (pieces, 1.0)
    return jnp.where(sub < 3, first, jnp.where(sub < 6, second, 0.0))


def _bias_cols(c):
    hi, mid, lo = _bias_pieces(c)
    lane = lax.broadcasted_iota(jnp.int32, (c.shape[0], LANES), 1)
    pieces = jnp.where(lane == 3, hi, jnp.where(lane == 4, mid, lo))
    return jnp.where(lane < 3, 1.0, jnp.where(lane < 6, pieces, 0.0))


def _fox_prompt_kernel(q_ref, cq_ref, k_hbm, v_hbm, ck_ref, ckp_ref, o_ref,
                       kaug_scr, vt_scr, sa_scr, sb_scr, acc_scr, state_scr, kbuf, vbuf, sems):
    assert q_ref.shape[1] == 2 * sa_scr.shape[0]
    h = pl.program_id(1)
    qi = pl.program_id(2)
    nheads = pl.num_programs(1)
    nslices = pl.num_programs(0) * nheads
    lin = pl.program_id(0) * nheads + h
    slot = lin % 2
    tq = q_ref.shape[1]
    tk = sa_scr.shape[0]
    plen = ckp_ref.shape[1]
    t = kbuf.shape[1] - plen

    @pl.when(qi == 0)
    def _():
        _slice_prefetch(k_hbm, v_hbm, kbuf, vbuf, sems, lin, nslices, nheads)

        def prep(cix, carry):
            r0 = pl.multiple_of(cix * tk, tk)
            src = pl.ds(pl.multiple_of(plen + r0, 8), tk)
            ck = ck_ref[0, 0, :, pl.ds(r0, tk)] * LOG2E
            kaug_scr[pl.ds(r0, tk), HEAD_DIM:] = _bias_rows(-ck, True).astype(BF16).T
            kaug_scr[pl.ds(r0, tk), :HEAD_DIM] = kbuf[slot, src, :].astype(BF16)
            vt_scr[:, pl.ds(r0, tk)] = vbuf[slot, src, :].astype(BF16).T
            return carry
        lax.fori_loop(0, t // tk, prep, 0)

    qt = q_ref[0].astype(BF16).T
    augq = _bias_rows(cq_ref[0, 0] * LOG2E, False).astype(BF16)
    qaug = jnp.concatenate([qt, augq], axis=0)

    def row0(blk):
        return pl.multiple_of(blk * tk, tk)

    def colmax(s):
        return jnp.max(s, axis=0, keepdims=True)

    def stage(blk, s_scr):
        s = jnp.dot(kaug_scr[pl.ds(row0(blk), tk), :], qaug, preferred_element_type=F32)
        s_scr[...] = s
        return colmax(s)

    def pv_block(blk):
        return lambda p: jnp.dot(vt_scr[:, pl.ds(row0(blk), tk)], p, preferred_element_type=F32)

    def absorb(ml, blocks, cols=None):
        m, l = ml
        m_new = m
        for _, smax, _ in blocks:
            m_new = jnp.maximum(m_new, smax)
        alpha = jnp.exp2(m - m_new)
        l = alpha * l
        pv = None
        for s, _, pv_fn in blocks:
            p = jnp.exp2(s - m_new)
            l = l + jnp.sum(p, axis=0, keepdims=True)
            term = pv_fn(p.astype(BF16))
            pv = term if pv is None else pv + term
        where = (Ellipsis,) if cols is None else (slice(None), cols)
        acc_scr[where] = alpha * acc_scr[where] + pv
        return m_new, l

    acc_scr[...] = jnp.zeros_like(acc_scr)
    ml = (jnp.full((1, tq), NEG_BIG, F32), jnp.zeros((1, tq), F32))

    def body(pair, state):
        ma, ml = state
        mb = stage(2 * pair + 1, sb_scr)
        ml = absorb(ml, [(sa_scr[...], ma, pv_block(2 * pair))])
        ma = stage(2 * pair + 2, sa_scr)
        ml = absorb(ml, [(sb_scr[...], mb, pv_block(2 * pair + 1))])
        return ma, ml

    def two_pairs(i, state):
        return body(2 * i + 1, body(2 * i, state))

    ma, (m_run, l_run) = lax.fori_loop(0, qi // 2, two_pairs, (stage(0, sa_scr), ml))
    state_scr[0:1, :], state_scr[1:2, :], state_scr[2:3, :] = ma, m_run, l_run

    @pl.when(qi % 2 == 1)
    def _():
        ma, (m_run, l_run) = body(qi - 1, (state_scr[0:1, :], (state_scr[1:2, :], state_scr[2:3, :])))
        state_scr[1:2, :], state_scr[2:3, :] = m_run, l_run

    ml = (state_scr[1:2, :], state_scr[2:3, :])

    lane = lax.broadcasted_iota(jnp.int32, ckp_ref.shape[1:], 1)
    ckp = jnp.sum(jnp.where(lane == h, ckp_ref[0], 0.0), axis=-1, keepdims=True) * LOG2E
    kp = jnp.concatenate([kbuf[slot, :plen, :].astype(BF16), _bias_cols(-ckp).astype(BF16)], axis=1)
    vp = vbuf[slot, :plen, :].astype(BF16)

    def pv_prefix(p):
        return lax.dot_general(vp, p, (((0,), (0,)), ((), ())), preferred_element_type=F32)

    first, second = slice(0, tk), slice(tk, tq)
    tri = (lax.broadcasted_iota(jnp.int32, (tk, tk), 0)
           <= lax.broadcasted_iota(jnp.int32, (tk, tk), 1))
    s_p = jnp.dot(kp, qaug, preferred_element_type=F32)
    s_a = sa_scr[...]
    stage(2 * qi + 1, sb_scr)
    s_b = sb_scr[:, second]
    m_run, l_run = ml
    l_halves = []
    for cols, logit_blocks in (
            (first, [(s_p[:, first], pv_prefix),
                     (jnp.where(tri, s_a[:, first], -jnp.inf), pv_block(2 * qi))]),
            (second, [(s_p[:, second], pv_prefix),
                      (s_a[:, second], pv_block(2 * qi)),
                      (jnp.where(tri, s_b, -jnp.inf), pv_block(2 * qi + 1))])):
        blocks = [(s, colmax(s), pv_fn) for s, pv_fn in logit_blocks]
        _, l_half = absorb((m_run[:, cols], l_run[:, cols]), blocks, cols)
        l_halves.append(l_half)
    l = jnp.concatenate(l_halves, axis=1)
    o_ref[0] = (acc_scr[...] * (1.0 / l)).astype(o_ref.dtype).T


def _fox_prompt(z, qtile, k_cache, v_cache, c_row, cp_col, *, tq):
    nb, nh, _, t = c_row.shape
    nq = t // tq
    plen = cp_col.shape[1]
    return pl.pallas_call(
        _fox_prompt_kernel,
        grid=(nb, nh, nq),
        in_specs=[
            pl.BlockSpec((1, tq, HEAD_DIM), lambda b, h, i: (qtile * nh + h, b * nq + i, 0)),
            pl.BlockSpec((1, 1, 1, tq), lambda b, h, i: (b, h, 0, i)),
            pl.BlockSpec(memory_space=pl.ANY),
            pl.BlockSpec(memory_space=pl.ANY),
            pl.BlockSpec((1, 1, 1, t), lambda b, h, i: (b, h, 0, 0)),
            pl.BlockSpec((1, plen, LANES), lambda b, h, i: (0, 0, 0)),
        ],
        out_specs=pl.BlockSpec((1, tq, HEAD_DIM), lambda b, h, i: (h, b * nq + i, 0)),
        out_shape=jax.ShapeDtypeStruct((nh, nb * t, HEAD_DIM), BF16),
        scratch_shapes=[pltpu.VMEM((t, 2 * HEAD_DIM), BF16), pltpu.VMEM((HEAD_DIM, t), BF16),
                        pltpu.VMEM((tq // 2, tq), F32), pltpu.VMEM((tq // 2, tq), F32),
                        pltpu.VMEM((HEAD_DIM, tq), F32), pltpu.VMEM((8, tq), F32),
                        pltpu.VMEM((2, plen + t, HEAD_DIM), F32),
                        pltpu.VMEM((2, plen + t, HEAD_DIM), F32),
                        pltpu.SemaphoreType.DMA((2, 2))],
        compiler_params=_params(("arbitrary", "arbitrary", "arbitrary")),
        name="fox_prompt",
    )(z, c_row, k_cache, v_cache, c_row, cp_col)


def _outproj_kernel(h_ref, a_ref, b_ref, wa_ref, wb_ref, g_ref, o_ref):
    def heads_on_lanes(ref, rows):
        return jnp.concatenate([ref[hd, rows, :] for hd in range(ref.shape[0])], axis=1)

    for rows in _row_halves(o_ref.shape[0]):
        mix = (jnp.dot(heads_on_lanes(a_ref, rows), wa_ref[...], preferred_element_type=F32)
               + jnp.dot(heads_on_lanes(b_ref, rows), wb_ref[...], preferred_element_type=F32))
        o_ref[rows, :] = h_ref[rows, :] + _rms(mix, g_ref[...])


def _outproj(h, mix_a, mix_b, w_a, w_b, gain):
    rows, d = h.shape
    nh = mix_a.shape[0]
    ka, kb = w_a.shape[0], w_b.shape[0]
    tm = _pick(rows, 1024)
    resident = dict(pipeline_mode=pl.Buffered(1))
    return pl.pallas_call(
        _outproj_kernel,
        grid=(rows // tm,),
        in_specs=[
            pl.BlockSpec((tm, d), lambda i: (i, 0)),
            pl.BlockSpec((nh, tm, HEAD_DIM), lambda i: (0, i, 0)),
            pl.BlockSpec((nh, tm, HEAD_DIM), lambda i: (0, i, 0)),
            pl.BlockSpec((ka, d), lambda i: (0, 0), **resident),
            pl.BlockSpec((kb, d), lambda i: (0, 0), **resident),
            pl.BlockSpec((1, d), lambda i: (0, 0), **resident),
        ],
        out_specs=pl.BlockSpec((tm, d), lambda i: (i, 0)),
        out_shape=jax.ShapeDtypeStruct((rows, d), F32),
        compiler_params=_params(("parallel",)),
        name="outproj",
    )(h, mix_a, mix_b, w_a, w_b, gain)


def _mlp_kernel(h_hbm, gpre_ref, wu_ref, wd_ref, gpost_ref, o_ref, a_scr, acc_scr, hbuf, hsems):
    i = pl.program_id(0)
    j = pl.program_id(1)
    last = pl.num_programs(1) - 1
    tm = o_ref.shape[0]
    halves = _row_halves(tm)
    slot = i % 2

    def h_copy(tile):
        return pltpu.make_async_copy(h_hbm.at[pl.ds(tile * tm, tm), :], hbuf.at[tile % 2],
                                     hsems.at[tile % 2])

    def up_down(rows):
        u = jnp.maximum(jnp.dot(a_scr[rows, :], wu_ref[...], preferred_element_type=F32), 0.0)
        return jnp.dot((u * u).astype(BF16), wd_ref[...], preferred_element_type=F32)

    @pl.when(j == 0)
    def _():
        @pl.when(i == 0)
        def _():
            h_copy(i).start()
        h_copy(i).wait()
        for rows in halves:
            a_scr[rows, :] = _rms(hbuf[slot, rows, :], gpre_ref[...]).astype(BF16)
            acc_scr[rows, :] = up_down(rows)

    @pl.when((j == 1) & (i < pl.num_programs(0) - 1))
    def _():
        h_copy(i + 1).start()

    @pl.when((j > 0) & (j < last))
    def _():
        acc_scr[...] += up_down(slice(None))

    @pl.when(j == last)
    def _():
        for rows in halves:
            acc = acc_scr[rows, :] + up_down(rows)
            o_ref[rows, :] = hbuf[slot, rows, :] + _rms(acc, gpost_ref[...])


def _mlp(h, g_pre, w_up, w_down, g_post):
    rows, d = h.shape
    dff = w_up.shape[1]
    tm = _pick(rows, 512)
    tf = _pick(dff, 1024)
    assert dff // tf >= 2
    return pl.pallas_call(
        _mlp_kernel,
        grid=(rows // tm, dff // tf),
        in_specs=[
            pl.BlockSpec(memory_space=pl.ANY),
            pl.BlockSpec((1, d), lambda i, j: (0, 0)),
            pl.BlockSpec((d, tf), lambda i, j: (0, j)),
            pl.BlockSpec((tf, d), lambda i, j: (j, 0)),
            pl.BlockSpec((1, d), lambda i, j: (0, 0)),
        ],
        out_specs=pl.BlockSpec((tm, d), lambda i, j: (i, 0)),
        out_shape=jax.ShapeDtypeStruct((rows, d), F32),
        scratch_shapes=[pltpu.VMEM((tm, d), BF16), pltpu.VMEM((tm, d), F32),
                        pltpu.VMEM((2, tm, d), F32), pltpu.SemaphoreType.DMA((2,))],
        compiler_params=_params(("arbitrary", "arbitrary")),
        name="mlp",
    )(h, g_pre, w_up, w_down, g_post)


def _col_form(c_row):
    c_col = jnp.transpose(c_row, (0, 2, 1))
    return jnp.pad(c_col, ((0, 0), (0, 0), (0, LANES - c_col.shape[2])))


def kernel(x_prompt, x_sample, cache_fox_k, cache_fox_v, cache_fox_logf, state_hgrn, meta_tokens,
           g_mix_pre, w_in, hg_lb_logits, hg_norm_gain, fox_f_bias, w_out, g_mix_post, g_mlp_pre,
           w_up, w_down, g_mlp_post):
    bp, seq, d = x_prompt.shape
    bs, dseq, _ = x_sample.shape
    n_meta = meta_tokens.shape[0]
    past = cache_fox_k.shape[2]
    nh = cache_fox_k.shape[3]
    hw = nh * HEAD_DIM
    n_main = w_in.shape[2] - nh
    assert dseq == n_meta, "sample frames and meta tokens share the small-stream kernels"
    assert state_hgrn.shape[2] == nh and n_main == 7 * hw

    w_main = w_in[0, :, :n_main].astype(BF16)
    w_f = jnp.pad(w_in[0, :, n_main:], ((0, 0), (0, LANES - nh))).astype(BF16)
    f_bias = jnp.pad(fox_f_bias[0], (0, LANES - nh))[None, :]
    w_oa = w_out[0, :hw].astype(BF16)
    w_ob = w_out[0, hw:].astype(BF16)
    w_u = w_up[0].astype(BF16)
    w_d = w_down[0].astype(BF16)
    g_pre = g_mix_pre[0][None, :]
    g_post = g_mix_post[0][None, :]
    g_mpre = g_mlp_pre[0][None, :]
    g_mpost = g_mlp_post[0][None, :]
    hg_gain = hg_norm_gain[0].reshape(nh, 1, HEAD_DIM)
    qtile, ktile, vtile = 4, 5, 6

    ns = bs + 1
    xs = jnp.concatenate([meta_tokens, x_sample.reshape(bs * dseq, d)], axis=0)
    zs, lfs = _inproj(xs, g_pre, w_main, w_f, f_bias, hg_lb_logits)
    lfs3 = lfs.reshape(ns, dseq, LANES)

    def small_kv(tile):
        rows_first = jnp.transpose(zs[tile * nh:(tile + 1) * nh], (1, 0, 2))
        return rows_first.reshape(ns, dseq, nh, HEAD_DIM)

    k_small, v_small = small_kv(ktile), small_kv(vtile)

    c_past = _cumsum_time(cache_fox_logf[0], jnp.zeros((bs, nh), F32), nh)
    init_s = jnp.concatenate([jnp.zeros((1, nh), F32), c_past[:, :, past - 1]], axis=0)
    c_small = _cumsum_time(lfs3, init_s, nh)

    s0_small = jnp.concatenate([jnp.zeros((1,) + state_hgrn.shape[2:], F32), state_hgrn[0]], axis=0)
    s0_small_t = jnp.swapaxes(s0_small, -1, -2)
    mix_hg_s, sfin_s_t = _hgrn(zs, ns, hg_gain, s0_small_t, s0_shared=False,
                               chunk=dseq, tt=dseq, hps=nh)

    mix_fox_s = _fox(zs, 1, qtile, ktile, vtile, _col_form(c_small[1:]),
                     cache_fox_k[0], cache_fox_v[0], c_past[:, :, None, :], c_small[1:, :, None, :])

    xm = x_prompt.reshape(bp * seq, d)
    zm, lfm, k_cache, v_cache = _inproj_kv(xm, g_pre, w_main, w_f, f_bias, hg_lb_logits,
                                           k_small[0], v_small[0], nseq=bp, nz=ktile)
    lfm3 = lfm.reshape(bp, seq, LANES)
    c_meta = c_small[0:1]
    init_m = jnp.broadcast_to(c_meta[:, :, n_meta - 1], (bp, nh))
    c_main = _cumsum_time(lfm3, init_m, nh)

    mix_hg_m, sfin_m_t = _hgrn(zm, bp, hg_gain, sfin_s_t[0:1], s0_shared=True,
                               chunk=_pick(seq, 128), tt=_pick(seq, 2048), hps=1)
    mix_fox_m = _fox_prompt(zm, qtile, k_cache, v_cache, c_main[:, :, None, :], _col_form(c_meta),
                            tq=_pick(seq, 1024))

    h1m = _outproj(xm, mix_hg_m, mix_fox_m, w_oa, w_ob, g_post)
    y_prompt = _mlp(h1m, g_mpre, w_u, w_d, g_mpost).reshape(bp, seq, d)
    h1s = _outproj(x_sample.reshape(bs * dseq, d), mix_hg_s[:, n_meta:, :], mix_fox_s,
                   w_oa, w_ob, g_post)
    y_sample = _mlp(h1s, g_mpre, w_u, w_d, g_mpost).reshape(bs, dseq, d)

    meta_lf = jnp.broadcast_to(lfs3[0:1, :, :nh], (bp, n_meta, nh))
    lf_p = jnp.concatenate([meta_lf, lfm3[:, :, :nh]], axis=1)[None]
    return (y_prompt, y_sample,
            k_cache[None], v_cache[None],
            lf_p,
            jnp.swapaxes(sfin_m_t, -1, -2)[None],
            k_small[1:][None], v_small[1:][None],
            lfs3[1:, :, :nh][None],
            jnp.swapaxes(sfin_s_t[1:], -1, -2)[None])
```
